```python
import jax, jax.numpy as jnp
from jax import lax
import numpy as np

D_MODEL = 2048
BATCH = 1
SEQ = 8192
DEPTH = 2

GLA_HEADS = 4
GLA_DK = 64
GLA_DV = 128
GLA_RANK = 16
GLA_TAU = 16.0
GLA_CHUNK = 64
GLA_QK = GLA_HEADS * GLA_DK
GLA_V = GLA_HEADS * GLA_DV
FOX_HEADS = 6
FOX_DH = 128
FOX_W = FOX_HEADS * FOX_DH
SB_HEADS = 6
SB_DH = 128
SB_W = SB_HEADS * SB_DH
Q_BLOCK = 128
D_FF = 5632
N_BRANCH = 3
LN_EPS = 1e-5
ALPHA = (2 * DEPTH) ** 0.25
BETA = (8 * DEPTH) ** -0.25
IN_COLS = 2 * GLA_QK + 2 * GLA_V + GLA_RANK + 3 * FOX_W + FOX_HEADS + 3 * SB_W + N_BRANCH * D_MODEL

kernel_name = "hybrid_gla_fox_stickbreaking_macaron_deepnorm"


def _layernorm(x, g, b):
    xf = x.astype(jnp.float32)
    mu = jnp.mean(xf, axis=-1, keepdims=True)
    var = jnp.mean(jnp.square(xf - mu), axis=-1, keepdims=True)
    return ((xf - mu) * lax.rsqrt(var + LN_EPS) * g + b).astype(x.dtype)


def _swiglu(x, w_gate, w_up, w_down):
    return (jax.nn.silu(x @ w_gate) * (x @ w_up)) @ w_down


def _gla(q, k, v, log_a):
    B, S, H, dk = q.shape
    dv = v.shape[-1]
    nc = S // GLA_CHUNK

    def chunked(t):
        t = jnp.moveaxis(t.astype(jnp.float32).reshape(B, nc, GLA_CHUNK, H, t.shape[-1]), 1, 0)
        return t.transpose(0, 1, 3, 2, 4)

    qc = chunked(q) * (dk ** -0.5)
    kc = chunked(k)
    vc = chunked(v)
    bc = lax.cumsum(chunked(log_a), axis=3)
    causal = jnp.tril(jnp.ones((GLA_CHUNK, GLA_CHUNK), dtype=bool))[:, :, None]

    def step(state, inp):
        qi, ki, vi, bi = inp
        o_inter = jnp.einsum('bhtk,bhkv->bhtv', qi * jnp.exp(bi), state)
        diff = bi[:, :, :, None, :] - bi[:, :, None, :, :]
        decay = jnp.exp(jnp.where(causal, diff, -jnp.inf))
        att = jnp.einsum('bhtk,bhsk,bhtsk->bhts', qi, ki, decay)
        o_intra = jnp.einsum('bhts,bhsv->bhtv', att, vi)
        b_last = bi[:, :, -1:, :]
        new_state = state * jnp.exp(b_last[:, :, 0, :])[..., None] + jnp.einsum(
            'bhsk,bhsv->bhkv', ki * jnp.exp(b_last - bi), vi)
        return new_state, o_inter + o_intra

    state0 = jnp.zeros((B, H, dk, dv), jnp.float32)
    _, o = lax.scan(step, state0, (qc, kc, vc, bc))
    return o.transpose(1, 0, 3, 2, 4).reshape(B, S, H, dv)


def _fox(q, k, v, log_f):
    B, S, H, d = q.shape
    nb = S // Q_BLOCK
    scale = d ** -0.5
    c = jnp.cumsum(log_f.astype(jnp.float32), axis=1)
    ck = c.transpose(0, 2, 1)
    kf = k.astype(jnp.float32)
    vf = v.astype(jnp.float32)
    qb = jnp.moveaxis(q.astype(jnp.float32).reshape(B, nb, Q_BLOCK, H, d), 1, 0)
    cb = jnp.moveaxis(c.reshape(B, nb, Q_BLOCK, H), 1, 0)
    tb = jnp.arange(S).reshape(nb, Q_BLOCK)
    kpos = jnp.arange(S)

    def block(args):
        qi, ci, ti = args
        s = jnp.einsum('bqhd,bkhd->bhqk', qi, kf) * scale
        s = s + ci.transpose(0, 2, 1)[..., None] - ck[:, :, None, :]
        s = jnp.where(kpos[None, :] <= ti[:, None], s, -jnp.inf)
        p = jax.nn.softmax(s, axis=-1)
        return jnp.einsum('bhqk,bkhd->bqhd', p, vf)

    o = lax.map(block, (qb, cb, tb))
    return jnp.moveaxis(o, 0, 1).reshape(B, S, H, d)


def _stick_breaking(q, k, v):
    B, S, H, d = q.shape
    nb = S // Q_BLOCK
    scale = d ** -0.5
    kf = k.astype(jnp.float32)
    vf = v.astype(jnp.float32)
    qb = jnp.moveaxis(q.astype(jnp.float32).reshape(B, nb, Q_BLOCK, H, d), 1, 0)
    tb = jnp.arange(S).reshape(nb, Q_BLOCK)
    kpos = jnp.arange(S)

    def block(args):
        qi, ti = args
        z = jnp.einsum('bqhd,bkhd->bhqk', qi, kf) * scale
        mask = kpos[None, :] < ti[:, None]
        log_1mb = jnp.where(mask, jax.nn.log_sigmoid(-z), 0.0)
        rest = lax.cumsum(log_1mb, axis=3, reverse=True) - log_1mb
        a = jnp.where(mask, jnp.exp(jax.nn.log_sigmoid(z) + rest), 0.0)
        return jnp.einsum('bhqk,bkhd->bqhd', a, vf)

    o = lax.map(block, (qb, tb))
    return jnp.moveaxis(o, 0, 1).reshape(B, S, H, d)


def _mixer(h, w_in, gla_w_a2, gla_b_a, gla_norm_g, fox_b_f, w_br_gla, w_br_fox, w_br_sb, w_out):
    B, S, _ = h.shape
    u = h @ w_in
    sizes = (GLA_QK, GLA_QK, GLA_V, GLA_V, GLA_RANK, FOX_W, FOX_W, FOX_W, FOX_HEADS,
             SB_W, SB_W, SB_W, N_BRANCH * D_MODEL)
    idx = [int(i) for i in np.cumsum(sizes)[:-1]]
    gq, gk, gv, gr, ga, fq, fk, fv, ff, sq, sk, sv, gates = jnp.split(u, idx, axis=-1)

    log_a = jax.nn.log_sigmoid((ga @ gla_w_a2 + gla_b_a).astype(jnp.float32)) / GLA_TAU
    o = _gla(gq.reshape(B, S, GLA_HEADS, GLA_DK), gk.reshape(B, S, GLA_HEADS, GLA_DK),
             gv.reshape(B, S, GLA_HEADS, GLA_DV), log_a.reshape(B, S, GLA_HEADS, GLA_DK))
    o = o * lax.rsqrt(jnp.mean(jnp.square(o), axis=-1, keepdims=True) + LN_EPS) * gla_norm_g
    o_gla = (o.reshape(B, S, GLA_V) * jax.nn.silu(gr.astype(jnp.float32))).astype(h.dtype)

    log_f = jax.nn.log_sigmoid((ff + fox_b_f).astype(jnp.float32))
    o_fox = _fox(fq.reshape(B, S, FOX_HEADS, FOX_DH), fk.reshape(B, S, FOX_HEADS, FOX_DH),
                 fv.reshape(B, S, FOX_HEADS, FOX_DH), log_f).reshape(B, S, FOX_W).astype(h.dtype)

    o_sb = _stick_breaking(sq.reshape(B, S, SB_HEADS, SB_DH), sk.reshape(B, S, SB_HEADS, SB_DH),
                           sv.reshape(B, S, SB_HEADS, SB_DH)).reshape(B, S, SB_W).astype(h.dtype)

    g = jax.nn.sigmoid(gates.reshape(B, S, N_BRANCH, D_MODEL))
    m = g[:, :, 0] * (o_gla @ w_br_gla) + g[:, :, 1] * (o_fox @ w_br_fox) + g[:, :, 2] * (o_sb @ w_br_sb)
    return m @ w_out


def setup_inputs(seed: int = 0) -> dict:
    key = jax.random.key(seed)
    ks = jax.random.split(key, 22)
    f32 = jnp.float32
    L, D, F = DEPTH, D_MODEL, D_FF

    def nrm(k, shape, scale):
        return jax.random.normal(k, shape, f32) * scale

    return {
        "x": nrm(ks[0], (BATCH, SEQ, D), 1.0),
        "ffn1_w_gate": nrm(ks[1], (L, D, F), D ** -0.5),
        "ffn1_w_up": nrm(ks[2], (L, D, F), D ** -0.5),
        "ffn1_w_down": nrm(ks[3], (L, F, D), BETA * F ** -0.5),
        "ln1_g": 1.0 + nrm(ks[4], (L, D), 0.02),
        "ln1_b": nrm(ks[5], (L, D), 0.02),
        "w_in": nrm(ks[6], (L, D, IN_COLS), D ** -0.5),
        "gla_w_a2": nrm(ks[7], (L, GLA_RANK, GLA_QK), GLA_RANK ** -0.5),
        "gla_b_a": nrm(ks[8], (L, GLA_QK), 0.02),
        "gla_norm_g": 1.0 + nrm(ks[9], (L, GLA_DV), 0.02),
        "fox_b_f": 2.0 + nrm(ks[10], (L, FOX_HEADS), 0.5),
        "w_br_gla": nrm(ks[11], (L, GLA_V, D), BETA * GLA_V ** -0.5),
        "w_br_fox": nrm(ks[12], (L, FOX_W, D), BETA * FOX_W ** -0.5),
        "w_br_sb": nrm(ks[13], (L, SB_W, D), BETA * SB_W ** -0.5),
        "w_out": nrm(ks[14], (L, D, D), BETA * D ** -0.5),
        "ln2_g": 1.0 + nrm(ks[15], (L, D), 0.02),
        "ln2_b": nrm(ks[16], (L, D), 0.02),
        "ffn2_w_gate": nrm(ks[17], (L, D, F), D ** -0.5),
        "ffn2_w_up": nrm(ks[18], (L, D, F), D ** -0.5),
        "ffn2_w_down": nrm(ks[19], (L, F, D), BETA * F ** -0.5),
        "ln3_g": 1.0 + nrm(ks[20], (L, D), 0.02),
        "ln3_b": nrm(ks[21], (L, D), 0.02),
    }


def reference(x, ffn1_w_gate, ffn1_w_up, ffn1_w_down, ln1_g, ln1_b, w_in, gla_w_a2, gla_b_a,
              gla_norm_g, fox_b_f, w_br_gla, w_br_fox, w_br_sb, w_out, ln2_g, ln2_b,
              ffn2_w_gate, ffn2_w_up, ffn2_w_down, ln3_g, ln3_b):
    for l in range(DEPTH):
        x = _layernorm(ALPHA * x + 0.5 * _swiglu(x, ffn1_w_gate[l], ffn1_w_up[l], ffn1_w_down[l]),
                       ln1_g[l], ln1_b[l])
        x = _layernorm(ALPHA * x + _mixer(x, w_in[l], gla_w_a2[l], gla_b_a[l], gla_norm_g[l], fox_b_f[l],
                                          w_br_gla[l], w_br_fox[l], w_br_sb[l], w_out[l]),
                       ln2_g[l], ln2_b[l])
        x = _layernorm(ALPHA * x + 0.5 * _swiglu(x, ffn2_w_gate[l], ffn2_w_up[l], ffn2_w_down[l]),
                       ln3_g[l], ln3_b[l])
    return x
```

```python
import functools

import jax
import jax.numpy as jnp
from jax import lax
from jax.experimental import pallas as pl
from jax.experimental.pallas import tpu as pltpu

F32 = jnp.float32
BF16 = jnp.bfloat16

LN_EPS = 1e-5
GLA_HEADS = 4
GLA_DK = 64
GLA_DV = 128
GLA_RANK = 16
GLA_TAU = 16.0
GLA_CHUNK = 64
GLA_QK = GLA_HEADS * GLA_DK
GLA_V = GLA_HEADS * GLA_DV
FOX_HEADS = 6
SB_HEADS = 6
HEAD_DIM = 128
FOX_W = FOX_HEADS * HEAD_DIM
SB_W = SB_HEADS * HEAD_DIM
N_BRANCH = 3
LANES = 128

FFN_TM = 512
FFN_TF = 512
PROJ_TM = 512
PROJ_TN = 512
MERGE_TM = 512
MERGE_TN = 256
GLA_ROWS = 256
CUM_ROWS = 256
ATT_BLOCK = 256
VMEM_LIMIT = 52 * 1024 * 1024
MASK_VALUE = -1e30


def _dot(a, b):
    return jnp.dot(a, b, preferred_element_type=F32)


def _dot_nt(a, b):
    return lax.dot_general(a, b, (((1,), (1,)), ((), ())), preferred_element_type=F32)


def _dot_tn(a, b):
    return lax.dot_general(a, b, (((0,), (0,)), ((), ())), preferred_element_type=F32)


def _split2(x):
    hi = x.astype(BF16)
    lo = (x - hi.astype(F32)).astype(BF16)
    return hi, lo


def _split3(x):
    hi = x.astype(BF16)
    r = x - hi.astype(F32)
    mid = r.astype(BF16)
    lo = (r - mid.astype(F32)).astype(BF16)
    return hi, mid, lo


def _softplus(x):
    return jnp.maximum(x, 0.0) + jnp.log(1.0 + jnp.exp(-jnp.abs(x)))


def _sigmoid(x):
    return 1.0 / (1.0 + jnp.exp(-x))


def _layernorm(y, g, b):
    mu = jnp.mean(y, axis=-1, keepdims=True)
    d = y - mu
    var = jnp.mean(d * d, axis=-1, keepdims=True)
    return d * lax.rsqrt(var + LN_EPS) * g + b


def _params(*sem):
    return pltpu.CompilerParams(dimension_semantics=sem, vmem_limit_bytes=VMEM_LIMIT)


def _ffn_kernel(x_ref, wg_ref, wu_ref, wd_ref, g_ref, b_ref, o_ref, xb_ref, acc_ref, *, alpha):
    f = pl.program_id(1)

    @pl.when(f == 0)
    def _():
        xb_ref[...] = x_ref[...].astype(BF16)
        acc_ref[...] = jnp.zeros_like(acc_ref)

    xb = xb_ref[...]
    gate = _dot(xb, wg_ref[...])
    up = _dot(xb, wu_ref[...])
    h = gate * _sigmoid(gate) * up
    acc_ref[...] += _dot(h.astype(BF16), wd_ref[...])

    @pl.when(f == pl.num_programs(1) - 1)
    def _():
        y = alpha * x_ref[...] + 0.5 * acc_ref[...]
        o_ref[...] = _layernorm(y, g_ref[...], b_ref[...])


def _ffn_ln(x, wg, wu, wd, g, b, alpha):
    s, d = x.shape
    f = wg.shape[1]
    tm, tf = min(FFN_TM, s), min(FFN_TF, f)
    return pl.pallas_call(
        functools.partial(_ffn_kernel, alpha=alpha),
        out_shape=jax.ShapeDtypeStruct((s, d), F32),
        grid=(s // tm, f // tf),
        in_specs=[
            pl.BlockSpec((tm, d), lambda i, j: (i, 0)),
            pl.BlockSpec((d, tf), lambda i, j: (0, j)),
            pl.BlockSpec((d, tf), lambda i, j: (0, j)),
            pl.BlockSpec((tf, d), lambda i, j: (j, 0)),
            pl.BlockSpec((1, d), lambda i, j: (0, 0)),
            pl.BlockSpec((1, d), lambda i, j: (0, 0)),
        ],
        out_specs=pl.BlockSpec((tm, d), lambda i, j: (i, 0)),
        scratch_shapes=[pltpu.VMEM((tm, d), BF16), pltpu.VMEM((tm, d), F32)],
        compiler_params=_params("parallel", "arbitrary"),
        name="ffn_ln",
    )(x, wg, wu, wd, g.reshape(1, d), b.reshape(1, d))


def _proj_kernel(x_ref, w_ref, o_ref, xb_ref):
    @pl.when(pl.program_id(1) == 0)
    def _():
        xb_ref[...] = x_ref[...].astype(BF16)

    o_ref[...] = _dot(xb_ref[...], w_ref[...]).astype(o_ref.dtype)


def _proj(x, w, out_dtype, tn):
    s, d = x.shape
    n = w.shape[1]
    tm = min(PROJ_TM, s)
    return pl.pallas_call(
        _proj_kernel,
        out_shape=jax.ShapeDtypeStruct((s, n), out_dtype),
        grid=(s // tm, n // tn),
        in_specs=[
            pl.BlockSpec((tm, d), lambda i, j: (i, 0)),
            pl.BlockSpec((d, tn), lambda i, j: (0, j)),
        ],
        out_specs=pl.BlockSpec((tm, tn), lambda i, j: (i, j)),
        scratch_shapes=[pltpu.VMEM((tm, d), BF16)],
        compiler_params=_params("parallel", "arbitrary"),
        name="in_proj",
    )(x, w)


def _gla_kernel(q_ref, k_ref, v_ref, gr_ref, ga_ref, wa_ref, ba_ref, ng_ref, o_ref,
                state_ref, b_s, k_s, v_s, acc_s):
    rows = q_ref.shape[0]
    n_chunks = rows // GLA_CHUNK

    @pl.when(pl.program_id(0) == 0)
    def _():
        state_ref[...] = jnp.zeros_like(state_ref)

    ga_hi, ga_lo = _split2(ga_ref[...])
    wa_hi, wa_lo = _split2(wa_ref[...])
    xa = _dot(ga_hi, wa_hi) + _dot(ga_hi, wa_lo) + _dot(ga_lo, wa_hi) + ba_ref[...]
    la = -_softplus(-xa) * (1.0 / GLA_TAU)

    r_i = lax.broadcasted_iota(jnp.int32, (rows, rows), 0)
    c_i = lax.broadcasted_iota(jnp.int32, (rows, rows), 1)
    same_chunk = (c_i >> GLA_CHUNK.bit_length() - 1) == (r_i >> GLA_CHUNK.bit_length() - 1)
    tri = jnp.where((c_i <= r_i) & same_chunk, 1.0, 0.0).astype(BF16)
    la_hi, la_mid, la_lo = _split3(la)
    b_all = _dot(tri, la_hi) + _dot(tri, la_mid) + _dot(tri, la_lo)
    b_s[...] = b_all
    k_s[...] = k_ref[...].astype(F32)
    v_s[...] = v_ref[...].astype(F32)

    e_r = lax.broadcasted_iota(jnp.int32, (GLA_QK, GLA_V), 0)
    e_c = lax.broadcasted_iota(jnp.int32, (GLA_QK, GLA_V), 1)
    same_head = (e_r >> GLA_DK.bit_length() - 1) == (e_c >> GLA_DV.bit_length() - 1)
    expand = jnp.where(same_head, 1.0, 0.0).astype(BF16)
    t_idx = lax.broadcasted_iota(jnp.int32, (GLA_CHUNK, GLA_QK), 0)
    ones_col = jnp.ones((GLA_CHUNK, GLA_DV), BF16)

    for c in range(n_chunks):
        lo_r = c * GLA_CHUNK
        sl = slice(lo_r, lo_r + GLA_CHUNK)
        q_c = q_ref[sl, :].astype(F32) * (GLA_DK ** -0.5)
        k_c = k_s[sl, :]
        b_c = b_all[sl, :]
        b_last = b_c[GLA_CHUNK - 1:GLA_CHUNK, :]
        v_cb = v_ref[sl, :]

        acc_s[...] = jnp.zeros_like(acc_s)

        def intra(s, _):
            b_row = b_s[pl.ds(lo_r + s, 1), :]
            k_row = k_s[pl.ds(lo_r + s, 1), :]
            v_row = v_s[pl.ds(lo_r + s, 1), :]
            decay = jnp.exp(jnp.where(t_idx >= s, b_c - b_row, -jnp.inf))
            p = (q_c * k_row) * decay
            acc_s[...] += _dot(p.astype(BF16), expand) * v_row
            return 0

        lax.fori_loop(0, GLA_CHUNK, intra, 0)

        qe = q_c * jnp.exp(b_c)
        kt = k_c * jnp.exp(b_last - b_c)
        la_c_hi, la_c_lo = la_hi[sl, :], la_mid[sl, :]
        outs = []
        for h in range(GLA_HEADS):
            ks = slice(h * GLA_DK, (h + 1) * GLA_DK)
            vs = slice(h * GLA_DV, (h + 1) * GLA_DV)
            st = state_ref[h]
            o_h = acc_s[:, vs] + _dot(qe[:, ks].astype(BF16), st.astype(BF16))
            tot = _dot_tn(la_c_hi[:, ks], ones_col) + _dot_tn(la_c_lo[:, ks], ones_col)
            kv = _dot_tn(kt[:, ks].astype(BF16), v_cb[:, vs])
            state_ref[h] = st * jnp.exp(tot) + kv
            ms = jnp.mean(o_h * o_h, axis=-1, keepdims=True)
            outs.append(o_h * lax.rsqrt(ms + LN_EPS) * ng_ref[...])
        o_c = jnp.concatenate(outs, axis=-1)
        gr = gr_ref[sl, :]
        o_ref[sl, :] = (o_c * (gr * _sigmoid(gr))).astype(o_ref.dtype)


def _gla(qkv, small, wa_pad, b_a, norm_g):
    s = qkv.shape[0]
    rows = min(GLA_ROWS, s)
    return pl.pallas_call(
        _gla_kernel,
        out_shape=jax.ShapeDtypeStruct((s, GLA_V), BF16),
        grid=(s // rows,),
        in_specs=[
            pl.BlockSpec((rows, GLA_QK), lambda i: (i, 0)),
            pl.BlockSpec((rows, GLA_QK), lambda i: (i, 1)),
            pl.BlockSpec((rows, GLA_V), lambda i: (i, 1)),
            pl.BlockSpec((rows, GLA_V), lambda i: (i, 0)),
            pl.BlockSpec((rows, LANES), lambda i: (i, GLA_V // LANES)),
            pl.BlockSpec((LANES, GLA_QK), lambda i: (0, 0)),
            pl.BlockSpec((1, GLA_QK), lambda i: (0, 0)),
            pl.BlockSpec((1, GLA_DV), lambda i: (0, 0)),
        ],
        out_specs=pl.BlockSpec((rows, GLA_V), lambda i: (i, 0)),
        scratch_shapes=[
            pltpu.VMEM((GLA_HEADS, GLA_DK, GLA_DV), F32),
            pltpu.VMEM((rows, GLA_QK), F32),
            pltpu.VMEM((rows, GLA_QK), F32),
            pltpu.VMEM((rows, GLA_V), F32),
            pltpu.VMEM((GLA_CHUNK, GLA_V), F32),
        ],
        compiler_params=_params("arbitrary"),
        name="gla",
    )(qkv, qkv, qkv, small, small, wa_pad, b_a.reshape(1, GLA_QK), norm_g.reshape(1, GLA_DV))


def _cumsum_kernel(ff_ref, bf_ref, c_ref, carry_ref):
    rows = ff_ref.shape[0]

    @pl.when(pl.program_id(0) == 0)
    def _():
        carry_ref[...] = jnp.zeros_like(carry_ref)

    log_f = -_softplus(-(ff_ref[...] + bf_ref[...]))
    r_i = lax.broadcasted_iota(jnp.int32, (rows, rows), 0)
    c_i = lax.broadcasted_iota(jnp.int32, (rows, rows), 1)
    tri = jnp.where(c_i <= r_i, 1.0, 0.0).astype(BF16)
    hi, mid, lo = _split3(log_f)
    c = _dot(tri, hi) + _dot(tri, mid) + _dot(tri, lo) + carry_ref[...]
    c_ref[...] = c
    carry_ref[...] = c[rows - 1:rows, :]


def _fox_cumsum(small, bf_pad):
    s = small.shape[0]
    rows = min(CUM_ROWS, s)
    return pl.pallas_call(
        _cumsum_kernel,
        out_shape=jax.ShapeDtypeStruct((s, LANES), F32),
        grid=(s // rows,),
        in_specs=[
            pl.BlockSpec((rows, LANES), lambda i: (i, GLA_V // LANES)),
            pl.BlockSpec((1, LANES), lambda i: (0, 0)),
        ],
        out_specs=pl.BlockSpec((rows, LANES), lambda i: (i, 0)),
        scratch_shapes=[pltpu.VMEM((1, LANES), F32)],
        compiler_params=_params("arbitrary"),
        name="fox_cumsum",
    )(small, bf_pad)


def _fox_kernel(q_ref, k_ref, v_ref, cq_ref, ck_ref, o_ref, *, scale):
    tb = q_ref.shape[0]
    qi = pl.program_id(1)
    q = q_ref[...]
    cq = cq_ref[0]
    row = lax.broadcasted_iota(jnp.int32, (tb, tb), 0)
    col = lax.broadcasted_iota(jnp.int32, (tb, tb), 1)

    def block(j, m, l, acc, masked):
        start = pl.multiple_of(j * tb, tb)
        kj = k_ref[pl.ds(start, tb), :]
        vj = v_ref[pl.ds(start, tb), :]
        s = _dot_nt(q, kj) * scale + cq - ck_ref[0, j]
        if masked:
            s = jnp.where(col <= row, s, MASK_VALUE)
        m_new = jnp.maximum(m, jnp.max(s, axis=-1, keepdims=True))
        p = jnp.exp(s - m_new)
        corr = jnp.exp(m - m_new)
        l = corr * l + jnp.sum(p, axis=-1, keepdims=True)
        acc = corr * acc + _dot(p.astype(BF16), vj)
        return m_new, l, acc

    m0 = jnp.full((tb, 1), MASK_VALUE, F32)
    l0 = jnp.zeros((tb, 1), F32)
    acc0 = jnp.zeros((tb, HEAD_DIM), F32)
    carry = block(qi, m0, l0, acc0, True)

    def body(jj, c):
        return block(qi - 1 - jj, *c, False)

    m, l, acc = lax.fori_loop(0, qi, body, carry)
    o_ref[...] = (acc / l).astype(o_ref.dtype)


def _fox(qkv, cq, ck, col0):
    s = qkv.shape[0]
    tb = min(ATT_BLOCK, s)
    nb = s // tb
    return pl.pallas_call(
        functools.partial(_fox_kernel, scale=HEAD_DIM ** -0.5),
        out_shape=jax.ShapeDtypeStruct((s, FOX_W), BF16),
        grid=(FOX_HEADS, nb),
        in_specs=[
            pl.BlockSpec((tb, HEAD_DIM), lambda h, i: (i, col0 + h)),
            pl.BlockSpec((s, HEAD_DIM), lambda h, i: (0, col0 + FOX_HEADS + h)),
            pl.BlockSpec((s, HEAD_DIM), lambda h, i: (0, col0 + 2 * FOX_HEADS + h)),
            pl.BlockSpec((1, tb, 1), lambda h, i: (h, i, 0)),
            pl.BlockSpec((1, nb, 1, tb), lambda h, i: (h, 0, 0, 0)),
        ],
        out_specs=pl.BlockSpec((tb, HEAD_DIM), lambda h, i: (i, h)),
        compiler_params=_params("parallel", "arbitrary"),
        name="fox_attn",
    )(qkv, qkv, qkv, cq, ck)


def _sb_kernel(q_ref, k_ref, v_ref, o_ref, *, scale):
    tb = q_ref.shape[0]
    qi = pl.program_id(1)
    q = q_ref[...]
    row = lax.broadcasted_iota(jnp.int32, (tb, tb), 0)
    col = lax.broadcasted_iota(jnp.int32, (tb, tb), 1)
    upper = jnp.where(row > col, 1.0, 0.0).astype(BF16)

    def block(j, carry, acc, masked):
        start = pl.multiple_of(j * tb, tb)
        kj = k_ref[pl.ds(start, tb), :]
        vj = v_ref[pl.ds(start, tb), :]
        z = _dot_nt(q, kj) * scale
        sp = _softplus(z)
        log_1mb = -sp
        if masked:
            valid = col < row
            log_1mb = jnp.where(valid, log_1mb, 0.0)
        hi, lo = _split2(log_1mb)
        rest = _dot(hi, upper) + _dot(lo, upper)
        a = jnp.exp(z - sp + rest + carry)
        if masked:
            a = jnp.where(valid, a, 0.0)
        acc = acc + _dot(a.astype(BF16), vj)
        carry = carry + rest[:, :1] + log_1mb[:, :1]
        return carry, acc

    carry0 = jnp.zeros((tb, 1), F32)
    acc0 = jnp.zeros((tb, HEAD_DIM), F32)
    state = block(qi, carry0, acc0, True)

    def body(jj, c):
        return block(qi - 1 - jj, *c, False)

    _, acc = lax.fori_loop(0, qi, body, state)
    o_ref[...] = acc.astype(o_ref.dtype)


def _stick_breaking(qkv, col0):
    s = qkv.shape[0]
    tb = min(ATT_BLOCK, s)
    return pl.pallas_call(
        functools.partial(_sb_kernel, scale=HEAD_DIM ** -0.5),
        out_shape=jax.ShapeDtypeStruct((s, SB_W), BF16),
        grid=(SB_HEADS, s // tb),
        in_specs=[
            pl.BlockSpec((tb, HEAD_DIM), lambda h, i: (i, col0 + h)),
            pl.BlockSpec((s, HEAD_DIM), lambda h, i: (0, col0 + SB_HEADS + h)),
            pl.BlockSpec((s, HEAD_DIM), lambda h, i: (0, col0 + 2 * SB_HEADS + h)),
        ],
        out_specs=pl.BlockSpec((tb, HEAD_DIM), lambda h, i: (i, h)),
        compiler_params=_params("parallel", "arbitrary"),
        name="sb_attn",
    )(qkv, qkv, qkv)


def _merge_kernel(x_ref, og_ref, of_ref, os_ref, wg0_ref, wg1_ref, wg2_ref, wbg_ref, wbf_ref, wbs_ref,
                  wo_ref, g_ref, b_ref, o_ref, xb_ref, acc_ref, *, alpha):
    j = pl.program_id(1)

    @pl.when(j == 0)
    def _():
        xb_ref[...] = x_ref[...].astype(BF16)
        acc_ref[...] = jnp.zeros_like(acc_ref)

    xb = xb_ref[...]
    m = _sigmoid(_dot(xb, wg0_ref[...])) * _dot(og_ref[...], wbg_ref[...])
    m += _sigmoid(_dot(xb, wg1_ref[...])) * _dot(of_ref[...], wbf_ref[...])
    m += _sigmoid(_dot(xb, wg2_ref[...])) * _dot(os_ref[...], wbs_ref[...])
    acc_ref[...] += _dot(m.astype(BF16), wo_ref[...])

    @pl.when(j == pl.num_programs(1) - 1)
    def _():
        y = alpha * x_ref[...] + acc_ref[...]
        o_ref[...] = _layernorm(y, g_ref[...], b_ref[...])


def _merge_ln(x, o_gla, o_fox, o_sb, w_gates, w_bg, w_bf, w_bs, w_out, g, b, alpha):
    s, d = x.shape
    tm, tn = min(MERGE_TM, s), min(MERGE_TN, d)
    nj = d // tn
    row_spec = lambda w: pl.BlockSpec((tm, w), lambda i, j: (i, 0))
    gate_spec = lambda br: pl.BlockSpec((d, tn), lambda i, j: (0, br * nj + j))
    col_spec = lambda w: pl.BlockSpec((w, tn), lambda i, j: (0, j))
    return pl.pallas_call(
        functools.partial(_merge_kernel, alpha=alpha),
        out_shape=jax.ShapeDtypeStruct((s, d), F32),
        grid=(s // tm, nj),
        in_specs=[
            row_spec(d), row_spec(GLA_V), row_spec(FOX_W), row_spec(SB_W),
            gate_spec(0), gate_spec(1), gate_spec(2),
            col_spec(GLA_V), col_spec(FOX_W), col_spec(SB_W),
            pl.BlockSpec((tn, d), lambda i, j: (j, 0)),
            pl.BlockSpec((1, d), lambda i, j: (0, 0)),
            pl.BlockSpec((1, d), lambda i, j: (0, 0)),
        ],
        out_specs=pl.BlockSpec((tm, d), lambda i, j: (i, 0)),
        scratch_shapes=[pltpu.VMEM((tm, d), BF16), pltpu.VMEM((tm, d), F32)],
        compiler_params=_params("parallel", "arbitrary"),
        name="merge_ln",
    )(x, o_gla, o_fox, o_sb, w_gates, w_gates, w_gates, w_bg, w_bf, w_bs, w_out,
      g.reshape(1, d), b.reshape(1, d))


def _mixer_branches(x, w_in, gla_w_a2, gla_b_a, gla_norm_g, fox_b_f):
    s, d = x.shape
    sizes = (GLA_QK, GLA_QK, GLA_V, GLA_V, GLA_RANK, FOX_W, FOX_W, FOX_W, FOX_HEADS,
             SB_W, SB_W, SB_W, N_BRANCH * d)
    offs = [0]
    for sz in sizes:
        offs.append(offs[-1] + sz)
    w = lambda a, b: w_in[:, offs[a]:offs[b]]
    w_qkv = jnp.concatenate([w(0, 3), w(5, 8), w(9, 12)], axis=1).astype(BF16)
    pad = LANES - GLA_RANK - FOX_HEADS
    w_small = jnp.concatenate([w(3, 5), w(8, 9), jnp.zeros((d, pad), F32)], axis=1).astype(BF16)
    w_gates = w(12, 13).astype(BF16)

    qkv = _proj(x, w_qkv, BF16, PROJ_TN)
    small = _proj(x, w_small, F32, GLA_V + LANES)

    wa_pad = jnp.zeros((LANES, GLA_QK), F32).at[:GLA_RANK].set(gla_w_a2)
    o_gla = _gla(qkv, small, wa_pad, gla_b_a, gla_norm_g)

    bf_pad = jnp.zeros((1, LANES), F32).at[0, GLA_RANK:GLA_RANK + FOX_HEADS].set(fox_b_f)
    c = _fox_cumsum(small, bf_pad)
    ct = c[:, GLA_RANK:GLA_RANK + FOX_HEADS].T
    tb = min(ATT_BLOCK, s)
    o_fox = _fox(qkv, ct.reshape(FOX_HEADS, s, 1), ct.reshape(FOX_HEADS, s // tb, 1, tb),
                 (2 * GLA_QK + GLA_V) // HEAD_DIM)
    o_sb = _stick_breaking(qkv, (2 * GLA_QK + GLA_V + 3 * FOX_W) // HEAD_DIM)
    return o_gla, o_fox, o_sb, w_gates


def _mixer_ln(x, w_in, gla_w_a2, gla_b_a, gla_norm_g, fox_b_f, w_br_gla, w_br_fox, w_br_sb, w_out,
              ln_g, ln_b, alpha):
    o_gla, o_fox, o_sb, w_gates = _mixer_branches(x, w_in, gla_w_a2, gla_b_a, gla_norm_g, fox_b_f)
    return _merge_ln(x, o_gla, o_fox, o_sb, w_gates, w_br_gla.astype(BF16), w_br_fox.astype(BF16),
                     w_br_sb.astype(BF16), w_out.astype(BF16), ln_g, ln_b, alpha)


def kernel(x, ffn1_w_gate, ffn1_w_up, ffn1_w_down, ln1_g, ln1_b, w_in, gla_w_a2, gla_b_a, gla_norm_g, fox_b_f, w_br_gla, w_br_fox, w_br_sb, w_out, ln2_g, ln2_b, ffn2_w_gate, ffn2_w_up, ffn2_w_down, ln3_g, ln3_b):
    batch, seq, d = x.shape
    depth = w_in.shape[0]
    alpha = (2 * depth) ** 0.25
    outs = []
    for bi in range(batch):
        h = x[bi]
        for l in range(depth):
            h = _ffn_ln(h, ffn1_w_gate[l].astype(BF16), ffn1_w_up[l].astype(BF16),
                        ffn1_w_down[l].astype(BF16), ln1_g[l], ln1_b[l], alpha)
            h = _mixer_ln(h, w_in[l], gla_w_a2[l], gla_b_a[l], gla_norm_g[l], fox_b_f[l], w_br_gla[l],
                          w_br_fox[l], w_br_sb[l], w_out[l], ln2_g[l], ln2_b[l], alpha)
            h = _ffn_ln(h, ffn2_w_gate[l].astype(BF16), ffn2_w_up[l].astype(BF16),
                        ffn2_w_down[l].astype(BF16), ln3_g[l], ln3_b[l], alpha)
        outs.append(h)
    return jnp.stack(outs, axis=0)
```

```python
import functools

import jax
import jax.numpy as jnp
from jax import lax
from jax.experimental import pallas as pl
from jax.experimental.pallas import tpu as pltpu

F32 = jnp.float32
BF16 = jnp.bfloat16

LN_EPS = 1e-5
GLA_HEADS = 4
GLA_DK = 64
GLA_DV = 128
GLA_RANK = 16
GLA_TAU = 16.0
GLA_CHUNK = 64
GLA_SUB = 16
GLA_QK = GLA_HEADS * GLA_DK
GLA_V = GLA_HEADS * GLA_DV
FOX_HEADS = 6
SB_HEADS = 6
HEAD_DIM = 128
FOX_W = FOX_HEADS * HEAD_DIM
SB_W = SB_HEADS * HEAD_DIM
N_BRANCH = 3
LANES = 128

REGROUP_W = 768
MAIN_W = 2 * GLA_QK + 2 * GLA_V + 3 * FOX_W + 3 * SB_W
COL_GV = 2 * GLA_QK
COL_GR = COL_GV + GLA_V
COL_FOX = COL_GR + GLA_V
COL_SB = COL_FOX + 3 * FOX_W
FF_LANE = GLA_RANK

FFN_TM = 512
FFN_TF = 512
PROJ_TM = 512
PROJ_TN = 512
MERGE_TM = 512
MERGE_TN = 256
REGROUP_ROWS = 512
GLA_ROWS = 256
CUM_ROWS = 256
ATT_BLOCK = 256
VMEM_LIMIT = 52 * 1024 * 1024
MASK_VALUE = -1e30
SKIP_LOG = 40.0


def _dot(a, b):
    return jnp.dot(a, b, preferred_element_type=F32)


def _dot_nt(a, b):
    return lax.dot_general(a, b, (((1,), (1,)), ((), ())), preferred_element_type=F32)


def _dot_tn(a, b):
    return lax.dot_general(a, b, (((0,), (0,)), ((), ())), preferred_element_type=F32)


def _split2(x):
    hi = x.astype(BF16)
    lo = (x - hi.astype(F32)).astype(BF16)
    return hi, lo


def _split3(x):
    hi = x.astype(BF16)
    r = x - hi.astype(F32)
    mid = r.astype(BF16)
    lo = (r - mid.astype(F32)).astype(BF16)
    return hi, mid, lo


def _softplus(x):
    return jnp.maximum(x, 0.0) + jnp.log(1.0 + jnp.exp(-jnp.abs(x)))


def _sigmoid(x):
    return 1.0 / (1.0 + jnp.exp(-x))


def _layernorm(y, g, b):
    mu = jnp.mean(y, axis=-1, keepdims=True)
    d = y - mu
    var = jnp.mean(d * d, axis=-1, keepdims=True)
    return d * lax.rsqrt(var + LN_EPS) * g + b


def _params(*sem):
    return pltpu.CompilerParams(dimension_semantics=sem, vmem_limit_bytes=VMEM_LIMIT)


def _ffn_kernel(x_ref, wg_ref, wu_ref, wd_ref, g_ref, b_ref, o_ref, xb_ref, acc_ref, *, alpha):
    f = pl.program_id(1)

    @pl.when(f == 0)
    def _():
        xb_ref[...] = x_ref[...].astype(BF16)
        acc_ref[...] = jnp.zeros_like(acc_ref)

    xb = xb_ref[...]
    gate = _dot(xb, wg_ref[...])
    up = _dot(xb, wu_ref[...])
    h = gate * _sigmoid(gate) * up
    acc_ref[...] += _dot(h.astype(BF16), wd_ref[...])

    @pl.when(f == pl.num_programs(1) - 1)
    def _():
        y = alpha * x_ref[...] + 0.5 * acc_ref[...]
        o_ref[...] = _layernorm(y, g_ref[...], b_ref[...])


def _ffn_ln(x, wg, wu, wd, g, b, alpha):
    s, d = x.shape
    f = wg.shape[1]
    tm, tf = min(FFN_TM, s), min(FFN_TF, f)
    return pl.pallas_call(
        functools.partial(_ffn_kernel, alpha=alpha),
        out_shape=jax.ShapeDtypeStruct((s, d), F32),
        grid=(s // tm, f // tf),
        in_specs=[
            pl.BlockSpec((tm, d), lambda i, j: (i, 0)),
            pl.BlockSpec((d, tf), lambda i, j: (0, j)),
            pl.BlockSpec((d, tf), lambda i, j: (0, j)),
            pl.BlockSpec((tf, d), lambda i, j: (j, 0)),
            pl.BlockSpec((1, d), lambda i, j: (0, 0)),
            pl.BlockSpec((1, d), lambda i, j: (0, 0)),
        ],
        out_specs=pl.BlockSpec((tm, d), lambda i, j: (i, 0)),
        scratch_shapes=[pltpu.VMEM((tm, d), BF16), pltpu.VMEM((tm, d), F32)],
        compiler_params=_params("parallel", "arbitrary"),
        name="ffn_ln",
    )(x, wg, wu, wd, g.reshape(1, d), b.reshape(1, d))


def _regroup_kernel(a_ref, b_ref, o_ref, *, n_plain, n_fox):
    j = pl.program_id(1)

    def shifted(off):
        if off == 0:
            return a_ref[...]
        return jnp.concatenate([a_ref[:, off:], b_ref[:, :off]], axis=1)

    @pl.when(j < n_plain)
    def _():
        o_ref[...] = shifted(0).astype(o_ref.dtype)

    @pl.when((j >= n_plain) & (j < n_plain + n_fox))
    def _():
        o_ref[...] = shifted(GLA_RANK).astype(o_ref.dtype)

    @pl.when(j >= n_plain + n_fox)
    def _():
        o_ref[...] = shifted(GLA_RANK + FOX_HEADS).astype(o_ref.dtype)


def _regroup(w_in, layer):
    _, d, _ = w_in.shape
    n_out = MAIN_W + N_BRANCH * d
    tr = min(REGROUP_ROWS, d)
    sub = REGROUP_W // LANES
    kern = functools.partial(_regroup_kernel, n_plain=COL_FOX // REGROUP_W, n_fox=3 * FOX_W // REGROUP_W)
    return pl.pallas_call(
        kern,
        out_shape=jax.ShapeDtypeStruct((d, n_out), BF16),
        grid=(d // tr, n_out // REGROUP_W),
        in_specs=[
            pl.BlockSpec((None, tr, REGROUP_W), lambda i, j: (layer, i, j)),
            pl.BlockSpec((None, tr, LANES), lambda i, j: (layer, i, (j + 1) * sub)),
        ],
        out_specs=pl.BlockSpec((tr, REGROUP_W), lambda i, j: (i, j)),
        compiler_params=_params("parallel", "parallel"),
        name="regroup_w_in",
    )(w_in, w_in)


def _tail_kernel(a_ref, b_ref, o_ref):
    lane = lax.broadcasted_iota(jnp.int32, a_ref.shape, 1)
    tail = jnp.where(lane < GLA_RANK, a_ref[...], jnp.where(lane < GLA_RANK + FOX_HEADS, b_ref[...], 0.0))
    o_ref[...] = tail.astype(o_ref.dtype)


def _tail_weights(w_in, layer):
    _, d, _ = w_in.shape
    ga_blk = COL_FOX // LANES
    ff_blk = (COL_FOX + GLA_RANK + 3 * FOX_W) // LANES
    return pl.pallas_call(
        _tail_kernel,
        out_shape=jax.ShapeDtypeStruct((d, LANES), BF16),
        grid=(1,),
        in_specs=[
            pl.BlockSpec((None, d, LANES), lambda i: (layer, 0, ga_blk)),
            pl.BlockSpec((None, d, LANES), lambda i: (layer, 0, ff_blk)),
        ],
        out_specs=pl.BlockSpec((d, LANES), lambda i: (0, 0)),
        compiler_params=_params("arbitrary"),
        name="tail_w_in",
    )(w_in, w_in)


def _proj_kernel(x_ref, w_ref, wt_ref, o_ref, t_ref, xb_ref):
    @pl.when(pl.program_id(1) == 0)
    def _():
        xb_ref[...] = x_ref[...].astype(BF16)
        t_ref[...] = _dot(xb_ref[...], wt_ref[...])

    o_ref[...] = _dot(xb_ref[...], w_ref[...]).astype(o_ref.dtype)


def _proj(x, w_all, w_tail):
    s, d = x.shape
    tm, tn = min(PROJ_TM, s), PROJ_TN
    return pl.pallas_call(
        _proj_kernel,
        out_shape=(jax.ShapeDtypeStruct((s, MAIN_W), BF16), jax.ShapeDtypeStruct((s, LANES), F32)),
        grid=(s // tm, MAIN_W // tn),
        in_specs=[
            pl.BlockSpec((tm, d), lambda i, j: (i, 0)),
            pl.BlockSpec((d, tn), lambda i, j: (0, j)),
            pl.BlockSpec((d, LANES), lambda i, j: (0, 0)),
        ],
        out_specs=(pl.BlockSpec((tm, tn), lambda i, j: (i, j)),
                   pl.BlockSpec((tm, LANES), lambda i, j: (i, 0))),
        scratch_shapes=[pltpu.VMEM((tm, d), BF16)],
        compiler_params=_params("parallel", "arbitrary"),
        name="in_proj",
    )(x, w_all, w_tail)


def _gla_kernel(q_ref, k_ref, v_ref, gr_ref, ga_ref, wa_ref, ba_ref, ng_ref, o_ref, state_ref):
    rows = q_ref.shape[0]
    n_chunks = rows // GLA_CHUNK
    n_sub = GLA_CHUNK // GLA_SUB

    @pl.when(pl.program_id(0) == 0)
    def _():
        state_ref[...] = jnp.zeros_like(state_ref)

    ga_hi, ga_lo = _split2(ga_ref[...])
    wa_hi, wa_lo = _split2(wa_ref[...])
    xa = _dot(ga_hi, wa_hi) + _dot(ga_hi, wa_lo) + _dot(ga_lo, wa_hi) + ba_ref[...]
    la = -_softplus(-xa) * (1.0 / GLA_TAU)

    r_i = lax.broadcasted_iota(jnp.int32, (rows, rows), 0)
    c_i = lax.broadcasted_iota(jnp.int32, (rows, rows), 1)
    chunk_shift = GLA_CHUNK.bit_length() - 1
    same_chunk = (c_i >> chunk_shift) == (r_i >> chunk_shift)
    tri = jnp.where((c_i <= r_i) & same_chunk, 1.0, 0.0).astype(BF16)
    la_hi, la_mid, la_lo = _split3(la)
    b_all = _dot(tri, la_hi) + _dot(tri, la_mid) + _dot(tri, la_lo)

    e_r = lax.broadcasted_iota(jnp.int32, (GLA_QK, GLA_V), 0)
    e_c = lax.broadcasted_iota(jnp.int32, (GLA_QK, GLA_V), 1)
    same_head = (e_r >> GLA_DK.bit_length() - 1) == (e_c >> GLA_DV.bit_length() - 1)
    expand = jnp.where(same_head, 1.0, 0.0).astype(BF16)
    lane_head = lax.broadcasted_iota(jnp.int32, (1, GLA_QK), 1) >> GLA_DK.bit_length() - 1
    head_mask = [jnp.where(lane_head == h, 1.0, 0.0) for h in range(GLA_HEADS)]
    ones_col = jnp.ones((GLA_CHUNK, GLA_DV), BF16)
    t_sub = lax.broadcasted_iota(jnp.int32, (GLA_SUB, GLA_QK), 0)
    s_chunk = lax.broadcasted_iota(jnp.int32, (GLA_CHUNK, GLA_QK), 0)

    def head_rows(x, n):
        return jnp.concatenate(
            [x[h * n:(h + 1) * n, h * GLA_DV:(h + 1) * GLA_DV] for h in range(GLA_HEADS)], axis=1)

    def stack_heads(x):
        return jnp.concatenate([x * head_mask[h] for h in range(GLA_HEADS)], axis=0).astype(BF16)

    for c in range(n_chunks):
        sl = slice(c * GLA_CHUNK, (c + 1) * GLA_CHUNK)
        q_c = q_ref[sl, :].astype(F32) * (GLA_DK ** -0.5)
        v_cb = v_ref[sl, :]
        k_c = k_ref[sl, :].astype(F32)
        v_c = v_cb.astype(F32)
        b_c = b_all[sl, :]
        b_last = b_c[GLA_CHUNK - 1:GLA_CHUNK, :]

        state = state_ref[...]
        inter = _dot(stack_heads(q_c * jnp.exp(b_c)), state.astype(BF16))
        o_inter = jnp.concatenate(
            [inter[h * GLA_CHUNK:(h + 1) * GLA_CHUNK, :] for h in range(GLA_HEADS)], axis=1)

        kv = _dot_tn((k_c * jnp.exp(b_last - b_c)).astype(BF16), v_cb)
        kv = jnp.concatenate(
            [kv[h * GLA_DK:(h + 1) * GLA_DK, h * GLA_DV:(h + 1) * GLA_DV] for h in range(GLA_HEADS)], axis=0)
        tot = _dot_tn(la_hi[sl, :], ones_col) + _dot_tn(la_mid[sl, :], ones_col)
        state_ref[...] = state * jnp.exp(tot) + kv

        o_rows = []
        for i in range(n_sub):
            rs = slice(i * GLA_SUB, (i + 1) * GLA_SUB)
            q_i, k_i, b_i, v_i = q_c[rs, :], k_c[rs, :], b_c[rs, :], v_c[rs, :]
            ps = []
            for s in range(GLA_SUB):
                diff = jnp.where(t_sub >= s, b_i - b_i[s:s + 1, :], -jnp.inf)
                ps.append(q_i * k_i[s:s + 1, :] * jnp.exp(diff))
            att = _dot(jnp.concatenate(ps, axis=0).astype(BF16), expand)
            o_i = att[0:GLA_SUB, :] * v_i[0:1, :]
            for s in range(1, GLA_SUB):
                o_i += att[s * GLA_SUB:(s + 1) * GLA_SUB, :] * v_i[s:s + 1, :]
            if i > 0:
                ref_row = b_c[i * GLA_SUB - 1:i * GLA_SUB, :]
                q_dec = stack_heads(q_i * jnp.exp(b_i - ref_row))
                k_dec = k_c * jnp.exp(jnp.where(s_chunk < i * GLA_SUB, ref_row - b_c, -jnp.inf))
                att_prev = _dot_nt(q_dec, k_dec.astype(BF16))
                o_i += head_rows(_dot(att_prev.astype(BF16), v_cb), GLA_SUB)
            o_rows.append(o_i)
        o_c = jnp.concatenate(o_rows, axis=0) + o_inter

        outs = []
        for h in range(GLA_HEADS):
            o_h = o_c[:, h * GLA_DV:(h + 1) * GLA_DV]
            ms = jnp.mean(o_h * o_h, axis=-1, keepdims=True)
            outs.append(o_h * lax.rsqrt(ms + LN_EPS) * ng_ref[...])
        gr = gr_ref[sl, :].astype(F32)
        o_ref[sl, :] = (jnp.concatenate(outs, axis=-1) * (gr * _sigmoid(gr))).astype(o_ref.dtype)


def _gla(u, tail, wa_pad, b_a, norm_g):
    s = u.shape[0]
    rows = min(GLA_ROWS, s)
    return pl.pallas_call(
        _gla_kernel,
        out_shape=jax.ShapeDtypeStruct((s, GLA_V), BF16),
        grid=(s // rows,),
        in_specs=[
            pl.BlockSpec((rows, GLA_QK), lambda i: (i, 0)),
            pl.BlockSpec((rows, GLA_QK), lambda i: (i, 1)),
            pl.BlockSpec((rows, GLA_V), lambda i: (i, COL_GV // GLA_V)),
            pl.BlockSpec((rows, GLA_V), lambda i: (i, COL_GR // GLA_V)),
            pl.BlockSpec((rows, LANES), lambda i: (i, 0)),
            pl.BlockSpec((LANES, GLA_QK), lambda i: (0, 0)),
            pl.BlockSpec((1, GLA_QK), lambda i: (0, 0)),
            pl.BlockSpec((1, GLA_DV), lambda i: (0, 0)),
        ],
        out_specs=pl.BlockSpec((rows, GLA_V), lambda i: (i, 0)),
        scratch_shapes=[pltpu.VMEM((GLA_QK, GLA_DV), F32)],
        compiler_params=_params("arbitrary"),
        name="gla",
    )(u, u, u, u, tail, wa_pad, b_a.reshape(1, GLA_QK), norm_g.reshape(1, GLA_DV))


def _cumsum_kernel(ff_ref, bf_ref, c_ref, carry_ref):
    rows = ff_ref.shape[0]

    @pl.when(pl.program_id(0) == 0)
    def _():
        carry_ref[...] = jnp.zeros_like(carry_ref)

    log_f = -_softplus(-(ff_ref[...] + bf_ref[...]))
    r_i = lax.broadcasted_iota(jnp.int32, (rows, rows), 0)
    c_i = lax.broadcasted_iota(jnp.int32, (rows, rows), 1)
    tri = jnp.where(c_i <= r_i, 1.0, 0.0).astype(BF16)
    hi, mid, lo = _split3(log_f)
    c = _dot(tri, hi) + _dot(tri, mid) + _dot(tri, lo) + carry_ref[...]
    c_ref[...] = c
    carry_ref[...] = c[rows - 1:rows, :]


def _fox_cumsum(tail, bf_pad):
    s = tail.shape[0]
    rows = min(CUM_ROWS, s)
    return pl.pallas_call(
        _cumsum_kernel,
        out_shape=jax.ShapeDtypeStruct((s, LANES), F32),
        grid=(s // rows,),
        in_specs=[
            pl.BlockSpec((rows, LANES), lambda i: (i, 0)),
            pl.BlockSpec((1, LANES), lambda i: (0, 0)),
        ],
        out_specs=pl.BlockSpec((rows, LANES), lambda i: (i, 0)),
        scratch_shapes=[pltpu.VMEM((1, LANES), F32)],
        compiler_params=_params("arbitrary"),
        name="fox_cumsum",
    )(tail, bf_pad)


def _fox_kernel(q_ref, k_ref, v_ref, cq_ref, ck_ref, o_ref, kmax_ref, *, scale):
    tb = q_ref.shape[0]
    qi = pl.program_id(1)

    @pl.when(qi == 0)
    def _():
        kf = k_ref[...].astype(F32)
        kmax_ref[...] = jnp.sqrt(jnp.max(jnp.sum(kf * kf, axis=-1, keepdims=True), axis=0, keepdims=True))

    q = q_ref[...]
    qf = q.astype(F32)
    cq = cq_ref[0]
    qk_bound = jnp.sqrt(jnp.sum(qf * qf, axis=-1, keepdims=True)) * kmax_ref[...] * scale + cq
    row = lax.broadcasted_iota(jnp.int32, (tb, tb), 0)
    col = lax.broadcasted_iota(jnp.int32, (tb, tb), 1)

    def block(j, m, l, acc, masked):
        start = pl.multiple_of(j * tb, tb)
        kj = k_ref[pl.ds(start, tb), :]
        vj = v_ref[pl.ds(start, tb), :]
        s = _dot_nt(q, kj) * scale + cq - ck_ref[0, j]
        if masked:
            s = jnp.where(col <= row, s, MASK_VALUE)
        m_new = jnp.maximum(m, jnp.max(s, axis=-1, keepdims=True))
        p = jnp.exp(s - m_new)
        corr = jnp.exp(m - m_new)
        l = corr * l + jnp.sum(p, axis=-1, keepdims=True)
        acc = corr * acc + _dot(p.astype(BF16), vj)
        return m_new, l, acc

    def worth(j, m):
        c_end = ck_ref[0, jnp.maximum(j, 0)][:, tb - 1:tb]
        return (jnp.max(qk_bound - c_end - m) > -SKIP_LOG).astype(jnp.int32)

    m0 = jnp.full((tb, 1), MASK_VALUE, F32)
    l0 = jnp.zeros((tb, 1), F32)
    acc0 = jnp.zeros((tb, HEAD_DIM), F32)
    m1, l1, acc1 = block(qi, m0, l0, acc0, True)

    def cond(c):
        return (c[0] >= 0) & (c[1] > 0)

    def body(c):
        j, _, m, l, acc = c
        m, l, acc = block(j, m, l, acc, False)
        return j - 1, worth(j - 1, m), m, l, acc

    _, _, _, l, acc = lax.while_loop(cond, body, (qi - 1, worth(qi - 1, m1), m1, l1, acc1))
    o_ref[...] = (acc / l).astype(o_ref.dtype)


def _fox(u, cq, ck):
    s = u.shape[0]
    tb = min(ATT_BLOCK, s)
    nb = s // tb
    col0 = COL_FOX // HEAD_DIM
    return pl.pallas_call(
        functools.partial(_fox_kernel, scale=HEAD_DIM ** -0.5),
        out_shape=jax.ShapeDtypeStruct((s, FOX_W), BF16),
        grid=(FOX_HEADS, nb),
        in_specs=[
            pl.BlockSpec((tb, HEAD_DIM), lambda h, i: (i, col0 + h)),
            pl.BlockSpec((s, HEAD_DIM), lambda h, i: (0, col0 + FOX_HEADS + h)),
            pl.BlockSpec((s, HEAD_DIM), lambda h, i: (0, col0 + 2 * FOX_HEADS + h)),
            pl.BlockSpec((1, tb, 1), lambda h, i: (h, i, 0)),
            pl.BlockSpec((1, nb, 1, tb), lambda h, i: (h, 0, 0, 0)),
        ],
        out_specs=pl.BlockSpec((tb, HEAD_DIM), lambda h, i: (i, h)),
        scratch_shapes=[pltpu.VMEM((1, 1), F32)],
        compiler_params=_params("arbitrary", "arbitrary"),
        name="fox_attn",
    )(u, u, u, cq, ck)


def _sb_kernel(q_ref, k_ref, v_ref, o_ref, *, scale):
    tb = q_ref.shape[0]
    qi = pl.program_id(1)
    q = q_ref[...]
    row = lax.broadcasted_iota(jnp.int32, (tb, tb), 0)
    col = lax.broadcasted_iota(jnp.int32, (tb, tb), 1)
    upper = jnp.where(row > col, 1.0, 0.0).astype(BF16)

    def block(j, carry, acc, masked):
        start = pl.multiple_of(j * tb, tb)
        kj = k_ref[pl.ds(start, tb), :]
        vj = v_ref[pl.ds(start, tb), :]
        z = _dot_nt(q, kj) * scale
        sp = _softplus(z)
        log_1mb = -sp
        if masked:
            valid = col < row
            log_1mb = jnp.where(valid, log_1mb, 0.0)
        hi, lo = _split2(log_1mb)
        rest = _dot(hi, upper) + _dot(lo, upper)
        a = jnp.exp(z - sp + rest + carry)
        if masked:
            a = jnp.where(valid, a, 0.0)
        acc = acc + _dot(a.astype(BF16), vj)
        carry = carry + rest[:, :1] + log_1mb[:, :1]
        return carry, acc

    carry0 = jnp.zeros((tb, 1), F32)
    acc0 = jnp.zeros((tb, HEAD_DIM), F32)
    carry1, acc1 = block(qi, carry0, acc0, True)

    def worth(carry):
        return (jnp.max(carry) > -SKIP_LOG).astype(jnp.int32)

    def cond(c):
        return (c[0] >= 0) & (c[1] > 0)

    def body(c):
        j, _, carry, acc = c
        carry, acc = block(j, carry, acc, False)
        return j - 1, worth(carry), carry, acc

    _, _, _, acc = lax.while_loop(cond, body, (qi - 1, worth(carry1), carry1, acc1))
    o_ref[...] = acc.astype(o_ref.dtype)


def _stick_breaking(u):
    s = u.shape[0]
    tb = min(ATT_BLOCK, s)
    col0 = COL_SB // HEAD_DIM
    return pl.pallas_call(
        functools.partial(_sb_kernel, scale=HEAD_DIM ** -0.5),
        out_shape=jax.ShapeDtypeStruct((s, SB_W), BF16),
        grid=(SB_HEADS, s // tb),
        in_specs=[
            pl.BlockSpec((tb, HEAD_DIM), lambda h, i: (i, col0 + h)),
            pl.BlockSpec((s, HEAD_DIM), lambda h, i: (0, col0 + SB_HEADS + h)),
            pl.BlockSpec((s, HEAD_DIM), lambda h, i: (0, col0 + 2 * SB_HEADS + h)),
        ],
        out_specs=pl.BlockSpec((tb, HEAD_DIM), lambda h, i: (i, h)),
        compiler_params=_params("parallel", "arbitrary"),
        name="sb_attn",
    )(u, u, u)


def _merge_kernel(x_ref, og_ref, of_ref, os_ref, wg0_ref, wg1_ref, wg2_ref, wbg_ref, wbf_ref, wbs_ref,
                  wo_ref, g_ref, b_ref, o_ref, xb_ref, acc_ref, *, alpha):
    j = pl.program_id(1)

    @pl.when(j == 0)
    def _():
        xb_ref[...] = x_ref[...].astype(BF16)
        acc_ref[...] = jnp.zeros_like(acc_ref)

    xb = xb_ref[...]
    m = _sigmoid(_dot(xb, wg0_ref[...])) * _dot(og_ref[...], wbg_ref[...])
    m += _sigmoid(_dot(xb, wg1_ref[...])) * _dot(of_ref[...], wbf_ref[...])
    m += _sigmoid(_dot(xb, wg2_ref[...])) * _dot(os_ref[...], wbs_ref[...])
    acc_ref[...] += _dot(m.astype(BF16), wo_ref[...])

    @pl.when(j == pl.num_programs(1) - 1)
    def _():
        y = alpha * x_ref[...] + acc_ref[...]
        o_ref[...] = _layernorm(y, g_ref[...], b_ref[...])


def _merge_ln(x, o_gla, o_fox, o_sb, w_all, w_bg, w_bf, w_bs, w_out, g, b, alpha):
    s, d = x.shape
    tm, tn = min(MERGE_TM, s), min(MERGE_TN, d)
    nj = d // tn
    gate0 = MAIN_W // tn
    row_spec = lambda w: pl.BlockSpec((tm, w), lambda i, j: (i, 0))
    gate_spec = lambda br: pl.BlockSpec((d, tn), lambda i, j: (0, gate0 + br * nj + j))
    col_spec = lambda w: pl.BlockSpec((w, tn), lambda i, j: (0, j))
    return pl.pallas_call(
        functools.partial(_merge_kernel, alpha=alpha),
        out_shape=jax.ShapeDtypeStruct((s, d), F32),
        grid=(s // tm, nj),
        in_specs=[
            row_spec(d), row_spec(GLA_V), row_spec(FOX_W), row_spec(SB_W),
            gate_spec(0), gate_spec(1), gate_spec(2),
            col_spec(GLA_V), col_spec(FOX_W), col_spec(SB_W),
            pl.BlockSpec((tn, d), lambda i, j: (j, 0)),
            pl.BlockSpec((1, d), lambda i, j: (0, 0)),
            pl.BlockSpec((1, d), lambda i, j: (0, 0)),
        ],
        out_specs=pl.BlockSpec((tm, d), lambda i, j: (i, 0)),
        scratch_shapes=[pltpu.VMEM((tm, d), BF16), pltpu.VMEM((tm, d), F32)],
        compiler_params=_params("parallel", "arbitrary"),
        name="merge_ln",
    )(x, o_gla, o_fox, o_sb, w_all, w_all, w_all, w_bg, w_bf, w_bs, w_out,
      g.reshape(1, d), b.reshape(1, d))


def _mixer_branches(x, w_in, layer, gla_w_a2, gla_b_a, gla_norm_g, fox_b_f):
    s, d = x.shape
    assert (N_BRANCH * d) % REGROUP_W == 0 and w_in.shape[2] == MAIN_W + GLA_RANK + FOX_HEADS + N_BRANCH * d
    w_all = _regroup(w_in, layer)
    u, tail = _proj(x, w_all, _tail_weights(w_in, layer))

    wa_pad = jnp.zeros((LANES, GLA_QK), F32).at[:GLA_RANK].set(gla_w_a2)
    o_gla = _gla(u, tail, wa_pad, gla_b_a, gla_norm_g)

    bf_pad = jnp.zeros((1, LANES), F32).at[0, FF_LANE:FF_LANE + FOX_HEADS].set(fox_b_f)
    c = _fox_cumsum(tail, bf_pad)
    ct = c[:, FF_LANE:FF_LANE + FOX_HEADS].T
    tb = min(ATT_BLOCK, s)
    o_fox = _fox(u, ct.reshape(FOX_HEADS, s, 1), ct.reshape(FOX_HEADS, s // tb, 1, tb))
    o_sb = _stick_breaking(u)
    return o_gla, o_fox, o_sb, w_all


def _mixer_ln(x, w_in, layer, gla_w_a2, gla_b_a, gla_norm_g, fox_b_f, w_br_gla, w_br_fox, w_br_sb, w_out,
              ln_g, ln_b, alpha):
    o_gla, o_fox, o_sb, w_all = _mixer_branches(x, w_in, layer, gla_w_a2, gla_b_a, gla_norm_g, fox_b_f)
    return _merge_ln(x, o_gla, o_fox, o_sb, w_all, w_br_gla.astype(BF16), w_br_fox.astype(BF16),
                     w_br_sb.astype(BF16), w_out.astype(BF16), ln_g, ln_b, alpha)


def kernel(x, ffn1_w_gate, ffn1_w_up, ffn1_w_down, ln1_g, ln1_b, w_in, gla_w_a2, gla_b_a, gla_norm_g, fox_b_f, w_br_gla, w_br_fox, w_br_sb, w_out, ln2_g, ln2_b, ffn2_w_gate, ffn2_w_up, ffn2_w_down, ln3_g, ln3_b):
    batch, seq, d = x.shape
    depth = w_in.shape[0]
    alpha = (2 * depth) ** 0.25
    outs = []
    for bi in range(batch):
        h = x[bi]
        for l in range(depth):
            h = _ffn_ln(h, ffn1_w_gate[l].astype(BF16), ffn1_w_up[l].astype(BF16),
                        ffn1_w_down[l].astype(BF16), ln1_g[l], ln1_b[l], alpha)
            h = _mixer_ln(h, w_in, l, gla_w_a2[l], gla_b_a[l], gla_norm_g[l], fox_b_f[l], w_br_gla[l],
                          w_br_fox[l], w_br_sb[l], w_out[l], ln2_g[l], ln2_b[l], alpha)
            h = _ffn_ln(h, ffn2_w_gate[l].astype(BF16), ffn2_w_up[l].astype(BF16),
                        ffn2_w_down[l].astype(BF16), ln3_g[l], ln3_b[l], alpha)
        outs.append(h)
    return jnp.stack(outs, axis=0)
```

```python
import functools

import jax
import jax.numpy as jnp
from jax import lax
from jax.experimental import pallas as pl
from jax.experimental.pallas import tpu as pltpu

F32 = jnp.float32
BF16 = jnp.bfloat16

LN_EPS = 1e-5
GLA_HEADS = 4
GLA_DK = 64
GLA_DV = 128
GLA_RANK = 16
GLA_TAU = 16.0
GLA_CHUNK = 64
GLA_SUB = 16
GLA_QK = GLA_HEADS * GLA_DK
GLA_V = GLA_HEADS * GLA_DV
FOX_HEADS = 6
SB_HEADS = 6
HEAD_DIM = 128
FOX_W = FOX_HEADS * HEAD_DIM
SB_W = SB_HEADS * HEAD_DIM
N_BRANCH = 3
LANES = 128

REGROUP_W = 768
MAIN_W = 2 * GLA_QK + 2 * GLA_V + 3 * FOX_W + 3 * SB_W
COL_GV = 2 * GLA_QK
COL_GR = COL_GV + GLA_V
COL_FOX = COL_GR + GLA_V
COL_SB = COL_FOX + 3 * FOX_W
FF_LANE = GLA_RANK

FFN_TM = 512
FFN_TF = 512
PROJ_TM = 1024
PROJ_TN = 512
MERGE_TM = 512
MERGE_TN = 256
REGROUP_ROWS = 512
GLA_ROWS = 256
CUM_ROWS = 256
ATT_BLOCK = 256
VMEM_LIMIT = 52 * 1024 * 1024
MASK_VALUE = -1e30
SKIP_LOG = 40.0


def _dot(a, b):
    return jnp.dot(a, b, preferred_element_type=F32)


def _dot_nt(a, b):
    return lax.dot_general(a, b, (((1,), (1,)), ((), ())), preferred_element_type=F32)


def _dot_tn(a, b):
    return lax.dot_general(a, b, (((0,), (0,)), ((), ())), preferred_element_type=F32)


def _split2(x):
    hi = x.astype(BF16)
    lo = (x - hi.astype(F32)).astype(BF16)
    return hi, lo


def _split3(x):
    hi = x.astype(BF16)
    r = x - hi.astype(F32)
    mid = r.astype(BF16)
    lo = (r - mid.astype(F32)).astype(BF16)
    return hi, mid, lo


def _softplus(x):
    return jnp.maximum(x, 0.0) + jnp.log(1.0 + jnp.exp(-jnp.abs(x)))


def _sigmoid(x):
    return 1.0 / (1.0 + jnp.exp(-x))


def _layernorm(y, g, b):
    mu = jnp.mean(y, axis=-1, keepdims=True)
    d = y - mu
    var = jnp.mean(d * d, axis=-1, keepdims=True)
    return d * lax.rsqrt(var + LN_EPS) * g + b


def _params(*sem):
    return pltpu.CompilerParams(dimension_semantics=sem, vmem_limit_bytes=VMEM_LIMIT)


def _ffn_kernel(x_ref, wg_ref, wu_ref, wd_ref, g_ref, b_ref, o_ref, xb_ref, acc_ref, *, alpha):
    f = pl.program_id(1)

    @pl.when(f == 0)
    def _():
        xb_ref[...] = x_ref[...].astype(BF16)
        acc_ref[...] = jnp.zeros_like(acc_ref)

    xb = xb_ref[...]
    gate = _dot(xb, wg_ref[...])
    up = _dot(xb, wu_ref[...])
    h = gate * _sigmoid(gate) * up
    acc_ref[...] += _dot(h.astype(BF16), wd_ref[...])

    @pl.when(f == pl.num_programs(1) - 1)
    def _():
        y = alpha * x_ref[...] + 0.5 * acc_ref[...]
        o_ref[...] = _layernorm(y, g_ref[...], b_ref[...])


def _ffn_ln(x, wg, wu, wd, g, b, layer, alpha):
    s, d = x.shape
    f = wg.shape[2]
    tm, tf = min(FFN_TM, s), min(FFN_TF, f)
    vec_spec = pl.BlockSpec((None, 1, d), lambda i, j: (layer, 0, 0))
    return pl.pallas_call(
        functools.partial(_ffn_kernel, alpha=alpha),
        out_shape=jax.ShapeDtypeStruct((s, d), F32),
        grid=(s // tm, f // tf),
        in_specs=[
            pl.BlockSpec((tm, d), lambda i, j: (i, 0)),
            pl.BlockSpec((None, d, tf), lambda i, j: (layer, 0, j)),
            pl.BlockSpec((None, d, tf), lambda i, j: (layer, 0, j)),
            pl.BlockSpec((None, tf, d), lambda i, j: (layer, j, 0)),
            vec_spec, vec_spec,
        ],
        out_specs=pl.BlockSpec((tm, d), lambda i, j: (i, 0)),
        scratch_shapes=[pltpu.VMEM((tm, d), BF16), pltpu.VMEM((tm, d), F32)],
        compiler_params=_params("parallel", "arbitrary"),
        name="ffn_ln",
    )(x, wg, wu, wd, g[:, None, :], b[:, None, :])


def _regroup_kernel(a_ref, b_ref, o_ref, *, n_plain, n_fox):
    j = pl.program_id(1)

    def shifted(off):
        if off == 0:
            return a_ref[...]
        return jnp.concatenate([a_ref[:, off:], b_ref[:, :off]], axis=1)

    @pl.when(j < n_plain)
    def _():
        o_ref[...] = shifted(0).astype(o_ref.dtype)

    @pl.when((j >= n_plain) & (j < n_plain + n_fox))
    def _():
        o_ref[...] = shifted(GLA_RANK).astype(o_ref.dtype)

    @pl.when(j >= n_plain + n_fox)
    def _():
        o_ref[...] = shifted(GLA_RANK + FOX_HEADS).astype(o_ref.dtype)


def _regroup(w_in, layer):
    _, d, _ = w_in.shape
    n_out = MAIN_W + N_BRANCH * d
    tr = min(REGROUP_ROWS, d)
    sub = REGROUP_W // LANES
    kern = functools.partial(_regroup_kernel, n_plain=COL_FOX // REGROUP_W, n_fox=3 * FOX_W // REGROUP_W)
    return pl.pallas_call(
        kern,
        out_shape=jax.ShapeDtypeStruct((d, n_out), BF16),
        grid=(d // tr, n_out // REGROUP_W),
        in_specs=[
            pl.BlockSpec((None, tr, REGROUP_W), lambda i, j: (layer, i, j)),
            pl.BlockSpec((None, tr, LANES), lambda i, j: (layer, i, (j + 1) * sub)),
        ],
        out_specs=pl.BlockSpec((tr, REGROUP_W), lambda i, j: (i, j)),
        compiler_params=_params("parallel", "parallel"),
        name="regroup_w_in",
    )(w_in, w_in)


def _tail_kernel(a_ref, b_ref, o_ref):
    lane = lax.broadcasted_iota(jnp.int32, a_ref.shape, 1)
    tail = jnp.where(lane < GLA_RANK, a_ref[...], jnp.where(lane < GLA_RANK + FOX_HEADS, b_ref[...], 0.0))
    o_ref[...] = tail.astype(o_ref.dtype)


def _tail_weights(w_in, layer):
    _, d, _ = w_in.shape
    ga_blk = COL_FOX // LANES
    ff_blk = (COL_FOX + GLA_RANK + 3 * FOX_W) // LANES
    return pl.pallas_call(
        _tail_kernel,
        out_shape=jax.ShapeDtypeStruct((d, LANES), BF16),
        grid=(1,),
        in_specs=[
            pl.BlockSpec((None, d, LANES), lambda i: (layer, 0, ga_blk)),
            pl.BlockSpec((None, d, LANES), lambda i: (layer, 0, ff_blk)),
        ],
        out_specs=pl.BlockSpec((d, LANES), lambda i: (0, 0)),
        compiler_params=_params("arbitrary"),
        name="tail_w_in",
    )(w_in, w_in)


def _proj_kernel(x_ref, w_ref, wt_ref, o_ref, t_ref, xb_ref):
    @pl.when(pl.program_id(1) == 0)
    def _():
        xb_ref[...] = x_ref[...].astype(BF16)
        t_ref[...] = _dot(xb_ref[...], wt_ref[...])

    o_ref[...] = _dot(xb_ref[...], w_ref[...]).astype(o_ref.dtype)


def _proj(x, w_all, w_tail):
    s, d = x.shape
    tm, tn = min(PROJ_TM, s), PROJ_TN
    return pl.pallas_call(
        _proj_kernel,
        out_shape=(jax.ShapeDtypeStruct((s, MAIN_W), BF16), jax.ShapeDtypeStruct((s, LANES), F32)),
        grid=(s // tm, MAIN_W // tn),
        in_specs=[
            pl.BlockSpec((tm, d), lambda i, j: (i, 0)),
            pl.BlockSpec((d, tn), lambda i, j: (0, j)),
            pl.BlockSpec((d, LANES), lambda i, j: (0, 0)),
        ],
        out_specs=(pl.BlockSpec((tm, tn), lambda i, j: (i, j)),
                   pl.BlockSpec((tm, LANES), lambda i, j: (i, 0))),
        scratch_shapes=[pltpu.VMEM((tm, d), BF16)],
        compiler_params=_params("parallel", "arbitrary"),
        name="in_proj",
    )(x, w_all, w_tail)


def _gla_kernel(q_ref, k_ref, v_ref, gr_ref, ga_ref, wa_ref, ba_ref, ng_ref, o_ref, state_ref):
    rows = q_ref.shape[0]
    n_chunks = rows // GLA_CHUNK
    n_sub = GLA_CHUNK // GLA_SUB

    @pl.when(pl.program_id(0) == 0)
    def _():
        state_ref[...] = jnp.zeros_like(state_ref)

    ga_hi, ga_lo = _split2(ga_ref[...])
    wa_hi, wa_lo = _split2(wa_ref[...])
    xa = _dot(ga_hi, wa_hi) + _dot(ga_hi, wa_lo) + _dot(ga_lo, wa_hi) + ba_ref[...]
    la = -_softplus(-xa) * (1.0 / GLA_TAU)

    r_i = lax.broadcasted_iota(jnp.int32, (rows, rows), 0)
    c_i = lax.broadcasted_iota(jnp.int32, (rows, rows), 1)
    chunk_shift = GLA_CHUNK.bit_length() - 1
    same_chunk = (c_i >> chunk_shift) == (r_i >> chunk_shift)
    tri = jnp.where((c_i <= r_i) & same_chunk, 1.0, 0.0).astype(BF16)
    la_hi, la_mid, la_lo = _split3(la)
    b_all = _dot(tri, la_hi) + _dot(tri, la_mid) + _dot(tri, la_lo)

    e_r = lax.broadcasted_iota(jnp.int32, (GLA_QK, GLA_V), 0)
    e_c = lax.broadcasted_iota(jnp.int32, (GLA_QK, GLA_V), 1)
    same_head = (e_r >> GLA_DK.bit_length() - 1) == (e_c >> GLA_DV.bit_length() - 1)
    expand = jnp.where(same_head, 1.0, 0.0).astype(BF16)
    lane_head = lax.broadcasted_iota(jnp.int32, (1, GLA_QK), 1) >> GLA_DK.bit_length() - 1
    head_mask = [jnp.where(lane_head == h, 1.0, 0.0) for h in range(GLA_HEADS)]
    ones_col = jnp.ones((GLA_CHUNK, GLA_DV), BF16)
    t_sub = lax.broadcasted_iota(jnp.int32, (GLA_SUB, GLA_QK), 0)
    s_chunk = lax.broadcasted_iota(jnp.int32, (GLA_CHUNK, GLA_QK), 0)

    def head_rows(x, n):
        return jnp.concatenate(
            [x[h * n:(h + 1) * n, h * GLA_DV:(h + 1) * GLA_DV] for h in range(GLA_HEADS)], axis=1)

    def stack_heads(x):
        return jnp.concatenate([x * head_mask[h] for h in range(GLA_HEADS)], axis=0).astype(BF16)

    for c in range(n_chunks):
        sl = slice(c * GLA_CHUNK, (c + 1) * GLA_CHUNK)
        q_c = q_ref[sl, :].astype(F32) * (GLA_DK ** -0.5)
        v_cb = v_ref[sl, :]
        k_c = k_ref[sl, :].astype(F32)
        v_c = v_cb.astype(F32)
        b_c = b_all[sl, :]
        b_last = b_c[GLA_CHUNK - 1:GLA_CHUNK, :]

        state = state_ref[...]
        inter = _dot(stack_heads(q_c * jnp.exp(b_c)), state.astype(BF16))
        o_inter = jnp.concatenate(
            [inter[h * GLA_CHUNK:(h + 1) * GLA_CHUNK, :] for h in range(GLA_HEADS)], axis=1)

        kv = _dot_tn((k_c * jnp.exp(b_last - b_c)).astype(BF16), v_cb)
        kv = jnp.concatenate(
            [kv[h * GLA_DK:(h + 1) * GLA_DK, h * GLA_DV:(h + 1) * GLA_DV] for h in range(GLA_HEADS)], axis=0)
        tot = _dot_tn(la_hi[sl, :], ones_col) + _dot_tn(la_mid[sl, :], ones_col)
        state_ref[...] = state * jnp.exp(tot) + kv

        o_rows = []
        for i in range(n_sub):
            rs = slice(i * GLA_SUB, (i + 1) * GLA_SUB)
            q_i, k_i, b_i, v_i = q_c[rs, :], k_c[rs, :], b_c[rs, :], v_c[rs, :]
            ps = []
            for s in range(GLA_SUB):
                diff = jnp.where(t_sub >= s, b_i - b_i[s:s + 1, :], -jnp.inf)
                ps.append(q_i * k_i[s:s + 1, :] * jnp.exp(diff))
            att = _dot(jnp.concatenate(ps, axis=0).astype(BF16), expand)
            o_i = att[0:GLA_SUB, :] * v_i[0:1, :]
            for s in range(1, GLA_SUB):
                o_i += att[s * GLA_SUB:(s + 1) * GLA_SUB, :] * v_i[s:s + 1, :]
            if i > 0:
                ref_row = b_c[i * GLA_SUB - 1:i * GLA_SUB, :]
                q_dec = stack_heads(q_i * jnp.exp(b_i - ref_row))
                k_dec = k_c * jnp.exp(jnp.where(s_chunk < i * GLA_SUB, ref_row - b_c, -jnp.inf))
                att_prev = _dot_nt(q_dec, k_dec.astype(BF16))
                o_i += head_rows(_dot(att_prev.astype(BF16), v_cb), GLA_SUB)
            o_rows.append(o_i)
        o_c = jnp.concatenate(o_rows, axis=0) + o_inter

        outs = []
        for h in range(GLA_HEADS):
            o_h = o_c[:, h * GLA_DV:(h + 1) * GLA_DV]
            ms = jnp.mean(o_h * o_h, axis=-1, keepdims=True)
            outs.append(o_h * lax.rsqrt(ms + LN_EPS) * ng_ref[...])
        gr = gr_ref[sl, :].astype(F32)
        o_ref[sl, :] = (jnp.concatenate(outs, axis=-1) * (gr * _sigmoid(gr))).astype(o_ref.dtype)


def _gla(u, tail, wa_pad, b_a, norm_g):
    s = u.shape[0]
    rows = min(GLA_ROWS, s)
    return pl.pallas_call(
        _gla_kernel,
        out_shape=jax.ShapeDtypeStruct((s, GLA_V), BF16),
        grid=(s // rows,),
        in_specs=[
            pl.BlockSpec((rows, GLA_QK), lambda i: (i, 0)),
            pl.BlockSpec((rows, GLA_QK), lambda i: (i, 1)),
            pl.BlockSpec((rows, GLA_V), lambda i: (i, COL_GV // GLA_V)),
            pl.BlockSpec((rows, GLA_V), lambda i: (i, COL_GR // GLA_V)),
            pl.BlockSpec((rows, LANES), lambda i: (i, 0)),
            pl.BlockSpec((LANES, GLA_QK), lambda i: (0, 0)),
            pl.BlockSpec((1, GLA_QK), lambda i: (0, 0)),
            pl.BlockSpec((1, GLA_DV), lambda i: (0, 0)),
        ],
        out_specs=pl.BlockSpec((rows, GLA_V), lambda i: (i, 0)),
        scratch_shapes=[pltpu.VMEM((GLA_QK, GLA_DV), F32)],
        compiler_params=_params("arbitrary"),
        name="gla",
    )(u, u, u, u, tail, wa_pad, b_a.reshape(1, GLA_QK), norm_g.reshape(1, GLA_DV))


def _cumsum_kernel(ff_ref, bf_ref, c_ref, carry_ref):
    rows = ff_ref.shape[0]

    @pl.when(pl.program_id(0) == 0)
    def _():
        carry_ref[...] = jnp.zeros_like(carry_ref)

    log_f = -_softplus(-(ff_ref[...] + bf_ref[...]))
    r_i = lax.broadcasted_iota(jnp.int32, (rows, rows), 0)
    c_i = lax.broadcasted_iota(jnp.int32, (rows, rows), 1)
    tri = jnp.where(c_i <= r_i, 1.0, 0.0).astype(BF16)
    hi, mid, lo = _split3(log_f)
    c = _dot(tri, hi) + _dot(tri, mid) + _dot(tri, lo) + carry_ref[...]
    c_ref[...] = c
    carry_ref[...] = c[rows - 1:rows, :]


def _fox_cumsum(tail, bf_pad):
    s = tail.shape[0]
    rows = min(CUM_ROWS, s)
    return pl.pallas_call(
        _cumsum_kernel,
        out_shape=jax.ShapeDtypeStruct((s, LANES), F32),
        grid=(s // rows,),
        in_specs=[
            pl.BlockSpec((rows, LANES), lambda i: (i, 0)),
            pl.BlockSpec((1, LANES), lambda i: (0, 0)),
        ],
        out_specs=pl.BlockSpec((rows, LANES), lambda i: (i, 0)),
        scratch_shapes=[pltpu.VMEM((1, LANES), F32)],
        compiler_params=_params("arbitrary"),
        name="fox_cumsum",
    )(tail, bf_pad)


def _fox_kernel(q_ref, k_ref, v_ref, c_ref, ck_ref, o_ref, kmax_ref, *, scale):
    tb = q_ref.shape[0]
    heads = q_ref.shape[1] // HEAD_DIM
    qi = pl.program_id(0)
    lanes = [slice(h * HEAD_DIM, (h + 1) * HEAD_DIM) for h in range(heads)]

    @pl.when(qi == 0)
    def _():
        for h in range(heads):
            kf = k_ref[:, lanes[h]].astype(F32)
            kmax_ref[h] = jnp.sqrt(jnp.max(jnp.sum(kf * kf, axis=-1, keepdims=True), axis=0, keepdims=True))

    qs = [q_ref[:, lanes[h]] for h in range(heads)]
    cqs = [c_ref[:, FF_LANE + h:FF_LANE + h + 1] for h in range(heads)]
    bounds = []
    for h in range(heads):
        qf = qs[h].astype(F32)
        bounds.append(jnp.sqrt(jnp.sum(qf * qf, axis=-1, keepdims=True)) * kmax_ref[h] * scale + cqs[h])
    row = lax.broadcasted_iota(jnp.int32, (tb, tb), 0)
    col = lax.broadcasted_iota(jnp.int32, (tb, tb), 1)

    def block(j, state, masked):
        start = pl.multiple_of(j * tb, tb)
        out = []
        for h in range(heads):
            m, l, acc = state[h]
            kj = k_ref[pl.ds(start, tb), lanes[h]]
            vj = v_ref[pl.ds(start, tb), lanes[h]]
            s = _dot_nt(qs[h], kj) * scale + cqs[h] - ck_ref[h, j]
            if masked:
                s = jnp.where(col <= row, s, MASK_VALUE)
            m_new = jnp.maximum(m, jnp.max(s, axis=-1, keepdims=True))
            p = jnp.exp(s - m_new)
            corr = jnp.exp(m - m_new)
            l = corr * l + jnp.sum(p, axis=-1, keepdims=True)
            acc = corr * acc + _dot(p.astype(BF16), vj)
            out.append((m_new, l, acc))
        return tuple(out)

    def worth(j, state):
        jc = jnp.maximum(j, 0)
        gap = bounds[0] - ck_ref[0, jc][:, tb - 1:tb] - state[0][0]
        for h in range(1, heads):
            gap = jnp.maximum(gap, bounds[h] - ck_ref[h, jc][:, tb - 1:tb] - state[h][0])
        return (jnp.max(gap) > -SKIP_LOG).astype(jnp.int32)

    init = tuple((jnp.full((tb, 1), MASK_VALUE, F32), jnp.zeros((tb, 1), F32), jnp.zeros((tb, HEAD_DIM), F32))
                 for _ in range(heads))
    first = block(qi, init, True)

    def cond(c):
        return (c[0] >= 0) & (c[1] > 0)

    def body(c):
        j, _, state = c
        state = block(j, state, False)
        return j - 1, worth(j - 1, state), state

    _, _, final = lax.while_loop(cond, body, (qi - 1, worth(qi - 1, first), first))
    for h in range(heads):
        _, l, acc = final[h]
        o_ref[:, lanes[h]] = (acc / l).astype(o_ref.dtype)


def _fox(u, c, ck):
    s = u.shape[0]
    tb = min(ATT_BLOCK, s)
    nb = s // tb
    col0 = COL_FOX // FOX_W
    resident = lambda blk: pl.BlockSpec((s, FOX_W), lambda i: (0, blk), pipeline_mode=pl.Buffered(1))
    return pl.pallas_call(
        functools.partial(_fox_kernel, scale=HEAD_DIM ** -0.5),
        out_shape=jax.ShapeDtypeStruct((s, FOX_W), BF16),
        grid=(nb,),
        in_specs=[
            pl.BlockSpec((tb, FOX_W), lambda i: (i, col0)),
            resident(col0 + 1),
            resident(col0 + 2),
            pl.BlockSpec((tb, LANES), lambda i: (i, 0)),
            pl.BlockSpec((FOX_HEADS, nb, 1, tb), lambda i: (0, 0, 0, 0)),
        ],
        out_specs=pl.BlockSpec((tb, FOX_W), lambda i: (i, 0)),
        scratch_shapes=[pltpu.VMEM((FOX_HEADS, 1, 1), F32)],
        compiler_params=_params("arbitrary"),
        name="fox_attn",
    )(u, u, u, c, ck)


def _sb_kernel(q_ref, k_ref, v_ref, o_ref, *, scale):
    tb = q_ref.shape[0]
    heads = q_ref.shape[1] // HEAD_DIM
    qi = pl.program_id(0)
    lanes = [slice(h * HEAD_DIM, (h + 1) * HEAD_DIM) for h in range(heads)]
    qs = [q_ref[:, lanes[h]] for h in range(heads)]
    row = lax.broadcasted_iota(jnp.int32, (tb, tb), 0)
    col = lax.broadcasted_iota(jnp.int32, (tb, tb), 1)
    upper = jnp.where(row > col, 1.0, 0.0).astype(BF16)

    def block(j, state, masked):
        start = pl.multiple_of(j * tb, tb)
        out = []
        for h in range(heads):
            carry, acc = state[h]
            kj = k_ref[pl.ds(start, tb), lanes[h]]
            vj = v_ref[pl.ds(start, tb), lanes[h]]
            z = _dot_nt(qs[h], kj) * scale
            sp = _softplus(z)
            log_1mb = -sp
            if masked:
                valid = col < row
                log_1mb = jnp.where(valid, log_1mb, 0.0)
            hi, lo = _split2(log_1mb)
            rest = _dot(hi, upper) + _dot(lo, upper)
            a = jnp.exp(z - sp + rest + carry)
            if masked:
                a = jnp.where(valid, a, 0.0)
            acc = acc + _dot(a.astype(BF16), vj)
            carry = carry + rest[:, :1] + log_1mb[:, :1]
            out.append((carry, acc))
        return tuple(out)

    def worth(state):
        top = state[0][0]
        for h in range(1, heads):
            top = jnp.maximum(top, state[h][0])
        return (jnp.max(top) > -SKIP_LOG).astype(jnp.int32)

    init = tuple((jnp.zeros((tb, 1), F32), jnp.zeros((tb, HEAD_DIM), F32)) for _ in range(heads))
    first = block(qi, init, True)

    def cond(c):
        return (c[0] >= 0) & (c[1] > 0)

    def body(c):
        j, _, state = c
        state = block(j, state, False)
        return j - 1, worth(state), state

    _, _, final = lax.while_loop(cond, body, (qi - 1, worth(first), first))
    for h in range(heads):
        o_ref[:, lanes[h]] = final[h][1].astype(o_ref.dtype)


def _stick_breaking(u):
    s = u.shape[0]
    tb = min(ATT_BLOCK, s)
    col0 = COL_SB // SB_W
    resident = lambda blk: pl.BlockSpec((s, SB_W), lambda i: (0, blk), pipeline_mode=pl.Buffered(1))
    return pl.pallas_call(
        functools.partial(_sb_kernel, scale=HEAD_DIM ** -0.5),
        out_shape=jax.ShapeDtypeStruct((s, SB_W), BF16),
        grid=(s // tb,),
        in_specs=[pl.BlockSpec((tb, SB_W), lambda i: (i, col0)), resident(col0 + 1), resident(col0 + 2)],
        out_specs=pl.BlockSpec((tb, SB_W), lambda i: (i, 0)),
        compiler_params=_params("parallel"),
        name="sb_attn",
    )(u, u, u)


def _merge_kernel(x_ref, og_ref, of_ref, os_ref, wg0_ref, wg1_ref, wg2_ref, wbg_ref, wbf_ref, wbs_ref,
                  wo_ref, g_ref, b_ref, o_ref, xb_ref, acc_ref, *, alpha):
    j = pl.program_id(1)

    @pl.when(j == 0)
    def _():
        xb_ref[...] = x_ref[...].astype(BF16)
        acc_ref[...] = jnp.zeros_like(acc_ref)

    xb = xb_ref[...]
    m = _sigmoid(_dot(xb, wg0_ref[...])) * _dot(og_ref[...], wbg_ref[...])
    m += _sigmoid(_dot(xb, wg1_ref[...])) * _dot(of_ref[...], wbf_ref[...])
    m += _sigmoid(_dot(xb, wg2_ref[...])) * _dot(os_ref[...], wbs_ref[...])
    acc_ref[...] += _dot(m.astype(BF16), wo_ref[...])

    @pl.when(j == pl.num_programs(1) - 1)
    def _():
        y = alpha * x_ref[...] + acc_ref[...]
        o_ref[...] = _layernorm(y, g_ref[...], b_ref[...])


def _merge_ln(x, o_gla, o_fox, o_sb, w_all, w_bg, w_bf, w_bs, w_out, g, b, layer, alpha):
    s, d = x.shape
    tm, tn = min(MERGE_TM, s), min(MERGE_TN, d)
    nj = d // tn
    gate0 = MAIN_W // tn
    row_spec = lambda w: pl.BlockSpec((tm, w), lambda i, j: (i, 0))
    gate_spec = lambda br: pl.BlockSpec((d, tn), lambda i, j: (0, gate0 + br * nj + j))
    col_spec = lambda w: pl.BlockSpec((None, w, tn), lambda i, j: (layer, 0, j))
    vec_spec = pl.BlockSpec((None, 1, d), lambda i, j: (layer, 0, 0))
    return pl.pallas_call(
        functools.partial(_merge_kernel, alpha=alpha),
        out_shape=jax.ShapeDtypeStruct((s, d), F32),
        grid=(s // tm, nj),
        in_specs=[
            row_spec(d), row_spec(GLA_V), row_spec(FOX_W), row_spec(SB_W),
            gate_spec(0), gate_spec(1), gate_spec(2),
            col_spec(GLA_V), col_spec(FOX_W), col_spec(SB_W),
            pl.BlockSpec((None, tn, d), lambda i, j: (layer, j, 0)),
            vec_spec, vec_spec,
        ],
        out_specs=pl.BlockSpec((tm, d), lambda i, j: (i, 0)),
        scratch_shapes=[pltpu.VMEM((tm, d), BF16), pltpu.VMEM((tm, d), F32)],
        compiler_params=_params("parallel", "arbitrary"),
        name="merge_ln",
    )(x, o_gla, o_fox, o_sb, w_all, w_all, w_all, w_bg, w_bf, w_bs, w_out, g[:, None, :], b[:, None, :])


def _mixer_branches(x, w_in, layer, gla_w_a2, gla_b_a, gla_norm_g, fox_b_f):
    s, d = x.shape
    assert (N_BRANCH * d) % REGROUP_W == 0 and w_in.shape[2] == MAIN_W + GLA_RANK + FOX_HEADS + N_BRANCH * d
    w_all = _regroup(w_in, layer)
    u, tail = _proj(x, w_all, _tail_weights(w_in, layer))

    wa_pad = jnp.zeros((LANES, GLA_QK), F32).at[:GLA_RANK].set(gla_w_a2)
    o_gla = _gla(u, tail, wa_pad, gla_b_a, gla_norm_g)

    bf_pad = jnp.zeros((1, LANES), F32).at[0, FF_LANE:FF_LANE + FOX_HEADS].set(fox_b_f)
    c = _fox_cumsum(tail, bf_pad)
    ct = c[:, FF_LANE:FF_LANE + FOX_HEADS].T
    tb = min(ATT_BLOCK, s)
    o_fox = _fox(u, c, ct.reshape(FOX_HEADS, s // tb, 1, tb))
    o_sb = _stick_breaking(u)
    return o_gla, o_fox, o_sb, w_all


def _mixer_ln(x, w_in, layer, gla_w_a2, gla_b_a, gla_norm_g, fox_b_f, w_bg, w_bf, w_bs, w_out, ln_g, ln_b, alpha):
    o_gla, o_fox, o_sb, w_all = _mixer_branches(x, w_in, layer, gla_w_a2, gla_b_a, gla_norm_g, fox_b_f)
    return _merge_ln(x, o_gla, o_fox, o_sb, w_all, w_bg, w_bf, w_bs, w_out, ln_g, ln_b, layer, alpha)


def kernel(x, ffn1_w_gate, ffn1_w_up, ffn1_w_down, ln1_g, ln1_b, w_in, gla_w_a2, gla_b_a, gla_norm_g, fox_b_f, w_br_gla, w_br_fox, w_br_sb, w_out, ln2_g, ln2_b, ffn2_w_gate, ffn2_w_up, ffn2_w_down, ln3_g, ln3_b):
    batch, seq, d = x.shape
    depth = w_in.shape[0]
    alpha = (2 * depth) ** 0.25
    cast = lambda w: w.astype(BF16)
    ffn1 = (cast(ffn1_w_gate), cast(ffn1_w_up), cast(ffn1_w_down))
    ffn2 = (cast(ffn2_w_gate), cast(ffn2_w_up), cast(ffn2_w_down))
    branch = (cast(w_br_gla), cast(w_br_fox), cast(w_br_sb), cast(w_out))
    outs = []
    for bi in range(batch):
        h = x[bi]
        for l in range(depth):
            h = _ffn_ln(h, *ffn1, ln1_g, ln1_b, l, alpha)
            h = _mixer_ln(h, w_in, l, gla_w_a2[l], gla_b_a[l], gla_norm_g[l], fox_b_f[l], *branch,
                          ln2_g, ln2_b, alpha)
            h = _ffn_ln(h, *ffn2, ln3_g, ln3_b, l, alpha)
        outs.append(h)
    return jnp.stack(outs, axis=0)
```

```python
import functools

import jax
import jax.numpy as jnp
from jax import lax
from jax.experimental import pallas as pl
from jax.experimental.pallas import tpu as pltpu

F32 = jnp.float32
BF16 = jnp.bfloat16

LN_EPS = 1e-5
GLA_HEADS = 4
GLA_DK = 64
GLA_DV = 128
GLA_RANK = 16
GLA_TAU = 16.0
GLA_CHUNK = 64
GLA_SUB = 16
GLA_QK = GLA_HEADS * GLA_DK
GLA_V = GLA_HEADS * GLA_DV
FOX_HEADS = 6
SB_HEADS = 6
HEAD_DIM = 128
FOX_W = FOX_HEADS * HEAD_DIM
SB_W = SB_HEADS * HEAD_DIM
N_BRANCH = 3
LANES = 128

REGROUP_W = 768
MAIN_W = 2 * GLA_QK + 2 * GLA_V + 3 * FOX_W + 3 * SB_W
COL_GV = 2 * GLA_QK
COL_GR = COL_GV + GLA_V
COL_FOX = COL_GR + GLA_V
COL_SB = COL_FOX + 3 * FOX_W
FF_LANE = GLA_RANK

FFN_TM = 512
FFN_TF = 512
PROJ_TM = 1024
PROJ_TN = 512
MERGE_TM = 512
MERGE_TN = 256
REGROUP_ROWS = 512
GLA_ROWS = 256
CUM_ROWS = 256
FOX_BLOCK = 256
SB_BLOCK = 128
VMEM_LIMIT = 52 * 1024 * 1024
MASK_VALUE = -1e30
SKIP_LOG = 30.0


def _dot(a, b):
    return jnp.dot(a, b, preferred_element_type=F32)


def _dot_nt(a, b):
    return lax.dot_general(a, b, (((1,), (1,)), ((), ())), preferred_element_type=F32)


def _dot_tn(a, b):
    return lax.dot_general(a, b, (((0,), (0,)), ((), ())), preferred_element_type=F32)


def _split2(x):
    hi = x.astype(BF16)
    lo = (x - hi.astype(F32)).astype(BF16)
    return hi, lo


def _split3(x):
    hi = x.astype(BF16)
    r = x - hi.astype(F32)
    mid = r.astype(BF16)
    lo = (r - mid.astype(F32)).astype(BF16)
    return hi, mid, lo


def _softplus(x):
    return jnp.maximum(x, 0.0) + jnp.log(1.0 + jnp.exp(-jnp.abs(x)))


def _sigmoid(x):
    return 1.0 / (1.0 + jnp.exp(-x))


def _layernorm(y, g, b):
    mu = jnp.mean(y, axis=-1, keepdims=True)
    d = y - mu
    var = jnp.mean(d * d, axis=-1, keepdims=True)
    return d * lax.rsqrt(var + LN_EPS) * g + b


def _params(*sem):
    return pltpu.CompilerParams(dimension_semantics=sem, vmem_limit_bytes=VMEM_LIMIT)


def _ffn_kernel(x_ref, wg_ref, wu_ref, wd_ref, g_ref, b_ref, o_ref, xb_ref, acc_ref, *, alpha):
    f = pl.program_id(1)

    @pl.when(f == 0)
    def _():
        xb_ref[...] = x_ref[...].astype(BF16)
        acc_ref[...] = jnp.zeros_like(acc_ref)

    xb = xb_ref[...]
    gate = _dot(xb, wg_ref[...])
    up = _dot(xb, wu_ref[...])
    h = gate * _sigmoid(gate) * up
    acc_ref[...] += _dot(h.astype(BF16), wd_ref[...])

    @pl.when(f == pl.num_programs(1) - 1)
    def _():
        y = alpha * x_ref[...] + 0.5 * acc_ref[...]
        o_ref[...] = _layernorm(y, g_ref[...], b_ref[...])


def _ffn_ln(x, wg, wu, wd, g, b, layer, alpha):
    s, d = x.shape
    f = wg.shape[2]
    tm, tf = min(FFN_TM, s), min(FFN_TF, f)
    vec_spec = pl.BlockSpec((None, 1, d), lambda i, j: (layer, 0, 0))
    return pl.pallas_call(
        functools.partial(_ffn_kernel, alpha=alpha),
        out_shape=jax.ShapeDtypeStruct((s, d), F32),
        grid=(s // tm, f // tf),
        in_specs=[
            pl.BlockSpec((tm, d), lambda i, j: (i, 0)),
            pl.BlockSpec((None, d, tf), lambda i, j: (layer, 0, j)),
            pl.BlockSpec((None, d, tf), lambda i, j: (layer, 0, j)),
            pl.BlockSpec((None, tf, d), lambda i, j: (layer, j, 0)),
            vec_spec, vec_spec,
        ],
        out_specs=pl.BlockSpec((tm, d), lambda i, j: (i, 0)),
        scratch_shapes=[pltpu.VMEM((tm, d), BF16), pltpu.VMEM((tm, d), F32)],
        compiler_params=_params("parallel", "arbitrary"),
        name="ffn_ln",
    )(x, wg, wu, wd, g[:, None, :], b[:, None, :])


def _regroup_kernel(a_ref, b_ref, o_ref, *, n_plain, n_fox):
    j = pl.program_id(1)

    def shifted(off):
        if off == 0:
            return a_ref[...]
        return jnp.concatenate([a_ref[:, off:], b_ref[:, :off]], axis=1)

    @pl.when(j < n_plain)
    def _():
        o_ref[...] = shifted(0).astype(o_ref.dtype)

    @pl.when((j >= n_plain) & (j < n_plain + n_fox))
    def _():
        o_ref[...] = shifted(GLA_RANK).astype(o_ref.dtype)

    @pl.when(j >= n_plain + n_fox)
    def _():
        o_ref[...] = shifted(GLA_RANK + FOX_HEADS).astype(o_ref.dtype)


def _regroup(w_in, layer):
    _, d, _ = w_in.shape
    n_out = MAIN_W + N_BRANCH * d
    tr = min(REGROUP_ROWS, d)
    sub = REGROUP_W // LANES
    kern = functools.partial(_regroup_kernel, n_plain=COL_FOX // REGROUP_W, n_fox=3 * FOX_W // REGROUP_W)
    return pl.pallas_call(
        kern,
        out_shape=jax.ShapeDtypeStruct((d, n_out), BF16),
        grid=(d // tr, n_out // REGROUP_W),
        in_specs=[
            pl.BlockSpec((None, tr, REGROUP_W), lambda i, j: (layer, i, j)),
            pl.BlockSpec((None, tr, LANES), lambda i, j: (layer, i, (j + 1) * sub)),
        ],
        out_specs=pl.BlockSpec((tr, REGROUP_W), lambda i, j: (i, j)),
        compiler_params=_params("parallel", "parallel"),
        name="regroup_w_in",
    )(w_in, w_in)


def _tail_kernel(a_ref, b_ref, o_ref):
    lane = lax.broadcasted_iota(jnp.int32, a_ref.shape, 1)
    tail = jnp.where(lane < GLA_RANK, a_ref[...], jnp.where(lane < GLA_RANK + FOX_HEADS, b_ref[...], 0.0))
    o_ref[...] = tail.astype(o_ref.dtype)


def _tail_weights(w_in, layer):
    _, d, _ = w_in.shape
    ga_blk = COL_FOX // LANES
    ff_blk = (COL_FOX + GLA_RANK + 3 * FOX_W) // LANES
    return pl.pallas_call(
        _tail_kernel,
        out_shape=jax.ShapeDtypeStruct((d, LANES), BF16),
        grid=(1,),
        in_specs=[
            pl.BlockSpec((None, d, LANES), lambda i: (layer, 0, ga_blk)),
            pl.BlockSpec((None, d, LANES), lambda i: (layer, 0, ff_blk)),
        ],
        out_specs=pl.BlockSpec((d, LANES), lambda i: (0, 0)),
        compiler_params=_params("arbitrary"),
        name="tail_w_in",
    )(w_in, w_in)


def _proj_kernel(x_ref, w_ref, wt_ref, o_ref, t_ref, xb_ref):
    @pl.when(pl.program_id(1) == 0)
    def _():
        xb_ref[...] = x_ref[...].astype(BF16)
        t_ref[...] = _dot(xb_ref[...], wt_ref[...])

    o_ref[...] = _dot(xb_ref[...], w_ref[...]).astype(o_ref.dtype)


def _proj(x, w_all, w_tail):
    s, d = x.shape
    tm, tn = min(PROJ_TM, s), PROJ_TN
    return pl.pallas_call(
        _proj_kernel,
        out_shape=(jax.ShapeDtypeStruct((s, MAIN_W), BF16), jax.ShapeDtypeStruct((s, LANES), F32)),
        grid=(s // tm, MAIN_W // tn),
        in_specs=[
            pl.BlockSpec((tm, d), lambda i, j: (i, 0)),
            pl.BlockSpec((d, tn), lambda i, j: (0, j)),
            pl.BlockSpec((d, LANES), lambda i, j: (0, 0)),
        ],
        out_specs=(pl.BlockSpec((tm, tn), lambda i, j: (i, j)),
                   pl.BlockSpec((tm, LANES), lambda i, j: (i, 0))),
        scratch_shapes=[pltpu.VMEM((tm, d), BF16)],
        compiler_params=_params("parallel", "arbitrary"),
        name="in_proj",
    )(x, w_all, w_tail)


def _gla_kernel(q_ref, k_ref, v_ref, gr_ref, ga_ref, wa_ref, ba_ref, ng_ref, o_ref, state_ref):
    rows = q_ref.shape[0]
    n_chunks = rows // GLA_CHUNK
    n_sub = GLA_CHUNK // GLA_SUB

    @pl.when(pl.program_id(0) == 0)
    def _():
        state_ref[...] = jnp.zeros_like(state_ref)

    ga_hi, ga_lo = _split2(ga_ref[...])
    wa_hi, wa_lo = _split2(wa_ref[...])
    xa = _dot(ga_hi, wa_hi) + _dot(ga_hi, wa_lo) + _dot(ga_lo, wa_hi) + ba_ref[...]
    la = -_softplus(-xa) * (1.0 / GLA_TAU)

    r_i = lax.broadcasted_iota(jnp.int32, (rows, rows), 0)
    c_i = lax.broadcasted_iota(jnp.int32, (rows, rows), 1)
    chunk_shift = GLA_CHUNK.bit_length() - 1
    same_chunk = (c_i >> chunk_shift) == (r_i >> chunk_shift)
    tri = jnp.where((c_i <= r_i) & same_chunk, 1.0, 0.0).astype(BF16)
    la_hi, la_mid, la_lo = _split3(la)
    b_all = _dot(tri, la_hi) + _dot(tri, la_mid) + _dot(tri, la_lo)

    e_r = lax.broadcasted_iota(jnp.int32, (GLA_QK, GLA_V), 0)
    e_c = lax.broadcasted_iota(jnp.int32, (GLA_QK, GLA_V), 1)
    same_head = (e_r >> GLA_DK.bit_length() - 1) == (e_c >> GLA_DV.bit_length() - 1)
    expand = jnp.where(same_head, 1.0, 0.0).astype(BF16)
    lane_head = lax.broadcasted_iota(jnp.int32, (1, GLA_QK), 1) >> GLA_DK.bit_length() - 1
    head_mask = [jnp.where(lane_head == h, 1.0, 0.0) for h in range(GLA_HEADS)]
    ones_col = jnp.ones((GLA_CHUNK, GLA_DV), BF16)
    t_sub = lax.broadcasted_iota(jnp.int32, (GLA_SUB, GLA_QK), 0)
    s_chunk = lax.broadcasted_iota(jnp.int32, (GLA_CHUNK, GLA_QK), 0)

    def head_rows(x, n):
        return jnp.concatenate(
            [x[h * n:(h + 1) * n, h * GLA_DV:(h + 1) * GLA_DV] for h in range(GLA_HEADS)], axis=1)

    def stack_heads(x):
        return jnp.concatenate([x * head_mask[h] for h in range(GLA_HEADS)], axis=0).astype(BF16)

    for c in range(n_chunks):
        sl = slice(c * GLA_CHUNK, (c + 1) * GLA_CHUNK)
        q_c = q_ref[sl, :].astype(F32) * (GLA_DK ** -0.5)
        v_cb = v_ref[sl, :]
        k_c = k_ref[sl, :].astype(F32)
        v_c = v_cb.astype(F32)
        b_c = b_all[sl, :]
        b_last = b_c[GLA_CHUNK - 1:GLA_CHUNK, :]

        state = state_ref[...]
        inter = _dot(stack_heads(q_c * jnp.exp(b_c)), state.astype(BF16))
        o_inter = jnp.concatenate(
            [inter[h * GLA_CHUNK:(h + 1) * GLA_CHUNK, :] for h in range(GLA_HEADS)], axis=1)

        kv = _dot_tn((k_c * jnp.exp(b_last - b_c)).astype(BF16), v_cb)
        kv = jnp.concatenate(
            [kv[h * GLA_DK:(h + 1) * GLA_DK, h * GLA_DV:(h + 1) * GLA_DV] for h in range(GLA_HEADS)], axis=0)
        tot = _dot_tn(la_hi[sl, :], ones_col) + _dot_tn(la_mid[sl, :], ones_col)
        state_ref[...] = state * jnp.exp(tot) + kv

        o_rows = []
        for i in range(n_sub):
            rs = slice(i * GLA_SUB, (i + 1) * GLA_SUB)
            q_i, k_i, b_i, v_i = q_c[rs, :], k_c[rs, :], b_c[rs, :], v_c[rs, :]
            ps = []
            for s in range(GLA_SUB):
                diff = jnp.where(t_sub >= s, b_i - b_i[s:s + 1, :], -jnp.inf)
                ps.append(q_i * k_i[s:s + 1, :] * jnp.exp(diff))
            att = _dot(jnp.concatenate(ps, axis=0).astype(BF16), expand)
            o_i = att[0:GLA_SUB, :] * v_i[0:1, :]
            for s in range(1, GLA_SUB):
                o_i += att[s * GLA_SUB:(s + 1) * GLA_SUB, :] * v_i[s:s + 1, :]
            if i > 0:
                ref_row = b_c[i * GLA_SUB - 1:i * GLA_SUB, :]
                q_dec = stack_heads(q_i * jnp.exp(b_i - ref_row))
                k_dec = k_c * jnp.exp(jnp.where(s_chunk < i * GLA_SUB, ref_row - b_c, -jnp.inf))
                att_prev = _dot_nt(q_dec, k_dec.astype(BF16))
                o_i += head_rows(_dot(att_prev.astype(BF16), v_cb), GLA_SUB)
            o_rows.append(o_i)
        o_c = jnp.concatenate(o_rows, axis=0) + o_inter

        outs = []
        for h in range(GLA_HEADS):
            o_h = o_c[:, h * GLA_DV:(h + 1) * GLA_DV]
            ms = jnp.mean(o_h * o_h, axis=-1, keepdims=True)
            outs.append(o_h * lax.rsqrt(ms + LN_EPS) * ng_ref[...])
        gr = gr_ref[sl, :].astype(F32)
        o_ref[sl, :] = (jnp.concatenate(outs, axis=-1) * (gr * _sigmoid(gr))).astype(o_ref.dtype)


def _gla(u, tail, wa_pad, b_a, norm_g):
    s = u.shape[0]
    rows = min(GLA_ROWS, s)
    return pl.pallas_call(
        _gla_kernel,
        out_shape=jax.ShapeDtypeStruct((s, GLA_V), BF16),
        grid=(s // rows,),
        in_specs=[
            pl.BlockSpec((rows, GLA_QK), lambda i: (i, 0)),
            pl.BlockSpec((rows, GLA_QK), lambda i: (i, 1)),
            pl.BlockSpec((rows, GLA_V), lambda i: (i, COL_GV // GLA_V)),
            pl.BlockSpec((rows, GLA_V), lambda i: (i, COL_GR // GLA_V)),
            pl.BlockSpec((rows, LANES), lambda i: (i, 0)),
            pl.BlockSpec((LANES, GLA_QK), lambda i: (0, 0)),
            pl.BlockSpec((1, GLA_QK), lambda i: (0, 0)),
            pl.BlockSpec((1, GLA_DV), lambda i: (0, 0)),
        ],
        out_specs=pl.BlockSpec((rows, GLA_V), lambda i: (i, 0)),
        scratch_shapes=[pltpu.VMEM((GLA_QK, GLA_DV), F32)],
        compiler_params=_params("arbitrary"),
        name="gla",
    )(u, u, u, u, tail, wa_pad, b_a.reshape(1, GLA_QK), norm_g.reshape(1, GLA_DV))


def _cumsum_kernel(ff_ref, bf_ref, c_ref, carry_ref):
    rows = ff_ref.shape[0]

    @pl.when(pl.program_id(0) == 0)
    def _():
        carry_ref[...] = jnp.zeros_like(carry_ref)

    log_f = -_softplus(-(ff_ref[...] + bf_ref[...]))
    r_i = lax.broadcasted_iota(jnp.int32, (rows, rows), 0)
    c_i = lax.broadcasted_iota(jnp.int32, (rows, rows), 1)
    tri = jnp.where(c_i <= r_i, 1.0, 0.0).astype(BF16)
    hi, mid, lo = _split3(log_f)
    c = _dot(tri, hi) + _dot(tri, mid) + _dot(tri, lo) + carry_ref[...]
    c_ref[...] = c
    carry_ref[...] = c[rows - 1:rows, :]


def _fox_cumsum(tail, bf_pad):
    s = tail.shape[0]
    rows = min(CUM_ROWS, s)
    return pl.pallas_call(
        _cumsum_kernel,
        out_shape=jax.ShapeDtypeStruct((s, LANES), F32),
        grid=(s // rows,),
        in_specs=[
            pl.BlockSpec((rows, LANES), lambda i: (i, 0)),
            pl.BlockSpec((1, LANES), lambda i: (0, 0)),
        ],
        out_specs=pl.BlockSpec((rows, LANES), lambda i: (i, 0)),
        scratch_shapes=[pltpu.VMEM((1, LANES), F32)],
        compiler_params=_params("arbitrary"),
        name="fox_cumsum",
    )(tail, bf_pad)


def _fox_kernel(q_ref, k_ref, v_ref, c_ref, ck_ref, o_ref, kmax_ref, *, scale):
    tb = q_ref.shape[0]
    heads = q_ref.shape[1] // HEAD_DIM
    qi = pl.program_id(0)
    lanes = [slice(h * HEAD_DIM, (h + 1) * HEAD_DIM) for h in range(heads)]

    @pl.when(qi == 0)
    def _():
        for h in range(heads):
            kf = k_ref[:, lanes[h]].astype(F32)
            kmax_ref[h] = jnp.sqrt(jnp.max(jnp.sum(kf * kf, axis=-1, keepdims=True), axis=0, keepdims=True))

    qs = [q_ref[:, lanes[h]] for h in range(heads)]
    cqs = [c_ref[:, FF_LANE + h:FF_LANE + h + 1] for h in range(heads)]
    bounds = []
    for h in range(heads):
        qf = qs[h].astype(F32)
        bounds.append(jnp.sqrt(jnp.sum(qf * qf, axis=-1, keepdims=True)) * kmax_ref[h] * scale + cqs[h])
    row = lax.broadcasted_iota(jnp.int32, (tb, tb), 0)
    col = lax.broadcasted_iota(jnp.int32, (tb, tb), 1)

    def block(j, state, masked):
        start = pl.multiple_of(j * tb, tb)
        hs = range(heads)
        ss = [_dot_nt(qs[h], k_ref[pl.ds(start, tb), lanes[h]]) for h in hs]
        ss = [ss[h] * scale + cqs[h] - ck_ref[h, j] for h in hs]
        if masked:
            ss = [jnp.where(col <= row, s, MASK_VALUE) for s in ss]
        m_new = [jnp.maximum(state[h][0], jnp.max(ss[h], axis=-1, keepdims=True)) for h in hs]
        ps = [jnp.exp(ss[h] - m_new[h]) for h in hs]
        corr = [jnp.exp(state[h][0] - m_new[h]) for h in hs]
        ls = [corr[h] * state[h][1] + jnp.sum(ps[h], axis=-1, keepdims=True) for h in hs]
        pv = [_dot(ps[h].astype(BF16), v_ref[pl.ds(start, tb), lanes[h]]) for h in hs]
        return tuple((m_new[h], ls[h], corr[h] * state[h][2] + pv[h]) for h in hs)

    def worth(j, state):
        jc = jnp.maximum(j, 0)
        gap = bounds[0] - ck_ref[0, jc][:, tb - 1:tb] - state[0][0]
        for h in range(1, heads):
            gap = jnp.maximum(gap, bounds[h] - ck_ref[h, jc][:, tb - 1:tb] - state[h][0])
        return (jnp.max(gap) > -SKIP_LOG).astype(jnp.int32)

    init = tuple((jnp.full((tb, 1), MASK_VALUE, F32), jnp.zeros((tb, 1), F32), jnp.zeros((tb, HEAD_DIM), F32))
                 for _ in range(heads))
    first = block(qi, init, True)

    def cond(c):
        return (c[0] >= 0) & (c[1] > 0)

    def body(c):
        j, _, state = c
        state = block(j, state, False)
        return j - 1, worth(j - 1, state), state

    _, _, final = lax.while_loop(cond, body, (qi - 1, worth(qi - 1, first), first))
    for h in range(heads):
        _, l, acc = final[h]
        o_ref[:, lanes[h]] = (acc / l).astype(o_ref.dtype)


def _fox(u, c, ck):
    s = u.shape[0]
    tb = min(FOX_BLOCK, s)
    nb = s // tb
    col0 = COL_FOX // FOX_W
    resident = lambda blk: pl.BlockSpec((s, FOX_W), lambda i: (0, blk), pipeline_mode=pl.Buffered(1))
    return pl.pallas_call(
        functools.partial(_fox_kernel, scale=HEAD_DIM ** -0.5),
        out_shape=jax.ShapeDtypeStruct((s, FOX_W), BF16),
        grid=(nb,),
        in_specs=[
            pl.BlockSpec((tb, FOX_W), lambda i: (i, col0)),
            resident(col0 + 1),
            resident(col0 + 2),
            pl.BlockSpec((tb, LANES), lambda i: (i, 0)),
            pl.BlockSpec((FOX_HEADS, nb, 1, tb), lambda i: (0, 0, 0, 0)),
        ],
        out_specs=pl.BlockSpec((tb, FOX_W), lambda i: (i, 0)),
        scratch_shapes=[pltpu.VMEM((FOX_HEADS, 1, 1), F32)],
        compiler_params=_params("arbitrary"),
        name="fox_attn",
    )(u, u, u, c, ck)


def _sb_kernel(q_ref, k_ref, v_ref, o_ref, *, scale):
    tb = q_ref.shape[0]
    heads = q_ref.shape[1] // HEAD_DIM
    qi = pl.program_id(0)
    lanes = [slice(h * HEAD_DIM, (h + 1) * HEAD_DIM) for h in range(heads)]
    qs = [q_ref[:, lanes[h]] for h in range(heads)]
    row = lax.broadcasted_iota(jnp.int32, (tb, tb), 0)
    col = lax.broadcasted_iota(jnp.int32, (tb, tb), 1)
    upper = jnp.where(row > col, 1.0, 0.0).astype(BF16)

    def block(j, state, masked):
        start = pl.multiple_of(j * tb, tb)
        hs = range(heads)
        zs = [_dot_nt(qs[h], k_ref[pl.ds(start, tb), lanes[h]]) * scale for h in hs]
        sps = [_softplus(z) for z in zs]
        if masked:
            valid = col < row
            log_1mb = [jnp.where(valid, -sp, 0.0) for sp in sps]
        else:
            log_1mb = [-sp for sp in sps]
        parts = [_split2(x) for x in log_1mb]
        rest = [_dot(hi, upper) + _dot(lo, upper) for hi, lo in parts]
        a = [jnp.exp(zs[h] - sps[h] + rest[h] + state[h][0]) for h in hs]
        if masked:
            a = [jnp.where(valid, x, 0.0) for x in a]
        pv = [_dot(a[h].astype(BF16), v_ref[pl.ds(start, tb), lanes[h]]) for h in hs]
        return tuple((state[h][0] + rest[h][:, :1] + log_1mb[h][:, :1], state[h][1] + pv[h]) for h in hs)

    def worth(state):
        top = state[0][0]
        for h in range(1, heads):
            top = jnp.maximum(top, state[h][0])
        return (jnp.max(top) > -SKIP_LOG).astype(jnp.int32)

    init = tuple((jnp.zeros((tb, 1), F32), jnp.zeros((tb, HEAD_DIM), F32)) for _ in range(heads))
    first = block(qi, init, True)

    def cond(c):
        return (c[0] >= 0) & (c[1] > 0)

    def body(c):
        j, _, state = c
        state = block(j, state, False)
        return j - 1, worth(state), state

    _, _, final = lax.while_loop(cond, body, (qi - 1, worth(first), first))
    for h in range(heads):
        o_ref[:, lanes[h]] = final[h][1].astype(o_ref.dtype)


def _stick_breaking(u):
    s = u.shape[0]
    tb = min(SB_BLOCK, s)
    col0 = COL_SB // SB_W
    resident = lambda blk: pl.BlockSpec((s, SB_W), lambda i: (0, blk), pipeline_mode=pl.Buffered(1))
    return pl.pallas_call(
        functools.partial(_sb_kernel, scale=HEAD_DIM ** -0.5),
        out_shape=jax.ShapeDtypeStruct((s, SB_W), BF16),
        grid=(s // tb,),
        in_specs=[pl.BlockSpec((tb, SB_W), lambda i: (i, col0)), resident(col0 + 1), resident(col0 + 2)],
        out_specs=pl.BlockSpec((tb, SB_W), lambda i: (i, 0)),
        compiler_params=_params("parallel"),
        name="sb_attn",
    )(u, u, u)


def _merge_kernel(x_ref, og_ref, of_ref, os_ref, wg0_ref, wg1_ref, wg2_ref, wbg_ref, wbf_ref, wbs_ref,
                  wo_ref, g_ref, b_ref, o_ref, xb_ref, acc_ref, *, alpha):
    j = pl.program_id(1)

    @pl.when(j == 0)
    def _():
        xb_ref[...] = x_ref[...].astype(BF16)
        acc_ref[...] = jnp.zeros_like(acc_ref)

    xb = xb_ref[...]
    m = _sigmoid(_dot(xb, wg0_ref[...])) * _dot(og_ref[...], wbg_ref[...])
    m += _sigmoid(_dot(xb, wg1_ref[...])) * _dot(of_ref[...], wbf_ref[...])
    m += _sigmoid(_dot(xb, wg2_ref[...])) * _dot(os_ref[...], wbs_ref[...])
    acc_ref[...] += _dot(m.astype(BF16), wo_ref[...])

    @pl.when(j == pl.num_programs(1) - 1)
    def _():
        y = alpha * x_ref[...] + acc_ref[...]
        o_ref[...] = _layernorm(y, g_ref[...], b_ref[...])


def _merge_ln(x, o_gla, o_fox, o_sb, w_all, w_bg, w_bf, w_bs, w_out, g, b, layer, alpha):
    s, d = x.shape
    tm, tn = min(MERGE_TM, s), min(MERGE_TN, d)
    nj = d // tn
    gate0 = MAIN_W // tn
    row_spec = lambda w: pl.BlockSpec((tm, w), lambda i, j: (i, 0))
    gate_spec = lambda br: pl.BlockSpec((d, tn), lambda i, j: (0, gate0 + br * nj + j))
    col_spec = lambda w: pl.BlockSpec((None, w, tn), lambda i, j: (layer, 0, j))
    vec_spec = pl.BlockSpec((None, 1, d), lambda i, j: (layer, 0, 0))
    return pl.pallas_call(
        functools.partial(_merge_kernel, alpha=alpha),
        out_shape=jax.ShapeDtypeStruct((s, d), F32),
        grid=(s // tm, nj),
        in_specs=[
            row_spec(d), row_spec(GLA_V), row_spec(FOX_W), row_spec(SB_W),
            gate_spec(0), gate_spec(1), gate_spec(2),
            col_spec(GLA_V), col_spec(FOX_W), col_spec(SB_W),
            pl.BlockSpec((None, tn, d), lambda i, j: (layer, j, 0)),
            vec_spec, vec_spec,
        ],
        out_specs=pl.BlockSpec((tm, d), lambda i, j: (i, 0)),
        scratch_shapes=[pltpu.VMEM((tm, d), BF16), pltpu.VMEM((tm, d), F32)],
        compiler_params=_params("parallel", "arbitrary"),
        name="merge_ln",
    )(x, o_gla, o_fox, o_sb, w_all, w_all, w_all, w_bg, w_bf, w_bs, w_out, g[:, None, :], b[:, None, :])


def _mixer_branches(x, w_in, layer, gla_w_a2, gla_b_a, gla_norm_g, fox_b_f):
    s, d = x.shape
    assert (N_BRANCH * d) % REGROUP_W == 0 and w_in.shape[2] == MAIN_W + GLA_RANK + FOX_HEADS + N_BRANCH * d
    w_all = _regroup(w_in, layer)
    u, tail = _proj(x, w_all, _tail_weights(w_in, layer))

    wa_pad = jnp.zeros((LANES, GLA_QK), F32).at[:GLA_RANK].set(gla_w_a2)
    o_gla = _gla(u, tail, wa_pad, gla_b_a, gla_norm_g)

    bf_pad = jnp.zeros((1, LANES), F32).at[0, FF_LANE:FF_LANE + FOX_HEADS].set(fox_b_f)
    c = _fox_cumsum(tail, bf_pad)
    ct = c[:, FF_LANE:FF_LANE + FOX_HEADS].T
    tb = min(FOX_BLOCK, s)
    o_fox = _fox(u, c, ct.reshape(FOX_HEADS, s // tb, 1, tb))
    o_sb = _stick_breaking(u)
    return o_gla, o_fox, o_sb, w_all


def _mixer_ln(x, w_in, layer, gla_w_a2, gla_b_a, gla_norm_g, fox_b_f, w_bg, w_bf, w_bs, w_out, ln_g, ln_b, alpha):
    o_gla, o_fox, o_sb, w_all = _mixer_branches(x, w_in, layer, gla_w_a2, gla_b_a, gla_norm_g, fox_b_f)
    return _merge_ln(x, o_gla, o_fox, o_sb, w_all, w_bg, w_bf, w_bs, w_out, ln_g, ln_b, layer, alpha)


def kernel(x, ffn1_w_gate, ffn1_w_up, ffn1_w_down, ln1_g, ln1_b, w_in, gla_w_a2, gla_b_a, gla_norm_g, fox_b_f, w_br_gla, w_br_fox, w_br_sb, w_out, ln2_g, ln2_b, ffn2_w_gate, ffn2_w_up, ffn2_w_down, ln3_g, ln3_b):
    batch, seq, d = x.shape
    depth = w_in.shape[0]
    alpha = (2 * depth) ** 0.25
    cast = lambda w: w.astype(BF16)
    ffn1 = (cast(ffn1_w_gate), cast(ffn1_w_up), cast(ffn1_w_down))
    ffn2 = (cast(ffn2_w_gate), cast(ffn2_w_up), cast(ffn2_w_down))
    branch = (cast(w_br_gla), cast(w_br_fox), cast(w_br_sb), cast(w_out))
    outs = []
    for bi in range(batch):
        h = x[bi]
        for l in range(depth):
            h = _ffn_ln(h, *ffn1, ln1_g, ln1_b, l, alpha)
            h = _mixer_ln(h, w_in, l, gla_w_a2[l], gla_b_a[l], gla_norm_g[l], fox_b_f[l], *branch,
                          ln2_g, ln2_b, alpha)
            h = _ffn_ln(h, *ffn2, ln3_g, ln3_b, l, alpha)
        outs.append(h)
    return jnp.stack(outs, axis=0)
```

```python
import functools

import jax
import jax.numpy as jnp
from jax import lax
from jax.experimental import pallas as pl
from jax.experimental.pallas import tpu as pltpu

F32 = jnp.float32
BF16 = jnp.bfloat16

LN_EPS = 1e-5
GLA_HEADS = 4
GLA_DK = 64
GLA_DV = 128
GLA_RANK = 16
GLA_TAU = 16.0
GLA_CHUNK = 64
GLA_SUB = 16
GLA_QK = GLA_HEADS * GLA_DK
GLA_V = GLA_HEADS * GLA_DV
FOX_HEADS = 6
SB_HEADS = 6
HEAD_DIM = 128
FOX_W = FOX_HEADS * HEAD_DIM
SB_W = SB_HEADS * HEAD_DIM
N_BRANCH = 3
LANES = 128

REGROUP_W = 768
MAIN_W = 2 * GLA_QK + 2 * GLA_V + 3 * FOX_W + 3 * SB_W
COL_GV = 2 * GLA_QK
COL_GR = COL_GV + GLA_V
COL_FOX = COL_GR + GLA_V
COL_SB = COL_FOX + 3 * FOX_W
FF_LANE = GLA_RANK

FFN_TM = 1024
FFN_TF = 256
PROJ_TM = 1024
PROJ_TN = 512
MERGE_TM = 512
MERGE_TN = 256
REGROUP_ROWS = 512
GLA_ROWS = 256
CUM_ROWS = 256
FOX_BLOCK = 256
SB_BLOCK = 128
VMEM_LIMIT = 52 * 1024 * 1024
MASK_VALUE = -1e30
SKIP_LOG = 30.0


def _dot(a, b):
    return jnp.dot(a, b, preferred_element_type=F32)


def _dot_nt(a, b):
    return lax.dot_general(a, b, (((1,), (1,)), ((), ())), preferred_element_type=F32)


def _dot_tn(a, b):
    return lax.dot_general(a, b, (((0,), (0,)), ((), ())), preferred_element_type=F32)


def _split2(x):
    hi = x.astype(BF16)
    lo = (x - hi.astype(F32)).astype(BF16)
    return hi, lo


def _split3(x):
    hi = x.astype(BF16)
    r = x - hi.astype(F32)
    mid = r.astype(BF16)
    lo = (r - mid.astype(F32)).astype(BF16)
    return hi, mid, lo


def _softplus(x):
    return jnp.maximum(x, 0.0) + jnp.log(1.0 + jnp.exp(-jnp.abs(x)))


def _sigmoid(x):
    return 1.0 / (1.0 + jnp.exp(-x))


def _layernorm(y, g, b):
    mu = jnp.mean(y, axis=-1, keepdims=True)
    d = y - mu
    var = jnp.mean(d * d, axis=-1, keepdims=True)
    return d * lax.rsqrt(var + LN_EPS) * g + b


def _params(*sem):
    return pltpu.CompilerParams(dimension_semantics=sem, vmem_limit_bytes=VMEM_LIMIT)


def _ffn_kernel(x_ref, wg_ref, wu_ref, wd_ref, g_ref, b_ref, o_ref, xb_ref, *, alpha):
    f = pl.program_id(1)

    @pl.when(f == 0)
    def _():
        x = x_ref[...]
        xb_ref[...] = x.astype(BF16)
        o_ref[...] = (2.0 * alpha) * x

    xb = xb_ref[...]
    gate = _dot(xb, wg_ref[...].astype(BF16))
    up = _dot(xb, wu_ref[...].astype(BF16))
    h = gate * _sigmoid(gate) * up
    o_ref[...] += _dot(h.astype(BF16), wd_ref[...].astype(BF16))

    @pl.when(f == pl.num_programs(1) - 1)
    def _():
        o_ref[...] = _layernorm(0.5 * o_ref[...], g_ref[...], b_ref[...])


def _ffn_ln(x, wg, wu, wd, g, b, layer, alpha):
    s, d = x.shape
    f = wg.shape[2]
    tm, tf = min(FFN_TM, s), min(FFN_TF, f)
    vec_spec = pl.BlockSpec((None, 1, d), lambda i, j: (layer, 0, 0))
    return pl.pallas_call(
        functools.partial(_ffn_kernel, alpha=alpha),
        out_shape=jax.ShapeDtypeStruct((s, d), F32),
        grid=(s // tm, f // tf),
        in_specs=[
            pl.BlockSpec((tm, d), lambda i, j: (i, 0), pipeline_mode=pl.Buffered(1)),
            pl.BlockSpec((None, d, tf), lambda i, j: (layer, 0, j)),
            pl.BlockSpec((None, d, tf), lambda i, j: (layer, 0, j)),
            pl.BlockSpec((None, tf, d), lambda i, j: (layer, j, 0)),
            vec_spec, vec_spec,
        ],
        out_specs=pl.BlockSpec((tm, d), lambda i, j: (i, 0)),
        scratch_shapes=[pltpu.VMEM((tm, d), BF16)],
        compiler_params=_params("parallel", "arbitrary"),
        name="ffn_ln",
    )(x, wg, wu, wd, g[:, None, :], b[:, None, :])


def _regroup_kernel(a_ref, b_ref, o_ref, *, n_plain, n_fox):
    j = pl.program_id(1)

    def shifted(off):
        if off == 0:
            return a_ref[...]
        return jnp.concatenate([a_ref[:, off:], b_ref[:, :off]], axis=1)

    @pl.when(j < n_plain)
    def _():
        o_ref[...] = shifted(0).astype(o_ref.dtype)

    @pl.when((j >= n_plain) & (j < n_plain + n_fox))
    def _():
        o_ref[...] = shifted(GLA_RANK).astype(o_ref.dtype)

    @pl.when(j >= n_plain + n_fox)
    def _():
        o_ref[...] = shifted(GLA_RANK + FOX_HEADS).astype(o_ref.dtype)


def _regroup(w_in, layer):
    _, d, _ = w_in.shape
    n_out = MAIN_W + N_BRANCH * d
    tr = min(REGROUP_ROWS, d)
    sub = REGROUP_W // LANES
    kern = functools.partial(_regroup_kernel, n_plain=COL_FOX // REGROUP_W, n_fox=3 * FOX_W // REGROUP_W)
    return pl.pallas_call(
        kern,
        out_shape=jax.ShapeDtypeStruct((d, n_out), BF16),
        grid=(d // tr, n_out // REGROUP_W),
        in_specs=[
            pl.BlockSpec((None, tr, REGROUP_W), lambda i, j: (layer, i, j)),
            pl.BlockSpec((None, tr, LANES), lambda i, j: (layer, i, (j + 1) * sub)),
        ],
        out_specs=pl.BlockSpec((tr, REGROUP_W), lambda i, j: (i, j)),
        compiler_params=_params("parallel", "parallel"),
        name="regroup_w_in",
    )(w_in, w_in)


def _tail_kernel(a_ref, b_ref, o_ref):
    lane = lax.broadcasted_iota(jnp.int32, a_ref.shape, 1)
    tail = jnp.where(lane < GLA_RANK, a_ref[...], jnp.where(lane < GLA_RANK + FOX_HEADS, b_ref[...], 0.0))
    o_ref[...] = tail.astype(o_ref.dtype)


def _tail_weights(w_in, layer):
    _, d, _ = w_in.shape
    ga_blk = COL_FOX // LANES
    ff_blk = (COL_FOX + GLA_RANK + 3 * FOX_W) // LANES
    return pl.pallas_call(
        _tail_kernel,
        out_shape=jax.ShapeDtypeStruct((d, LANES), BF16),
        grid=(1,),
        in_specs=[
            pl.BlockSpec((None, d, LANES), lambda i: (layer, 0, ga_blk)),
            pl.BlockSpec((None, d, LANES), lambda i: (layer, 0, ff_blk)),
        ],
        out_specs=pl.BlockSpec((d, LANES), lambda i: (0, 0)),
        compiler_params=_params("arbitrary"),
        name="tail_w_in",
    )(w_in, w_in)


def _proj_kernel(x_ref, w_ref, wt_ref, o_ref, t_ref, xb_ref):
    @pl.when(pl.program_id(1) == 0)
    def _():
        xb_ref[...] = x_ref[...].astype(BF16)
        t_ref[...] = _dot(xb_ref[...], wt_ref[...])

    o_ref[...] = _dot(xb_ref[...], w_ref[...]).astype(o_ref.dtype)


def _proj(x, w_all, w_tail):
    s, d = x.shape
    tm, tn = min(PROJ_TM, s), PROJ_TN
    return pl.pallas_call(
        _proj_kernel,
        out_shape=(jax.ShapeDtypeStruct((s, MAIN_W), BF16), jax.ShapeDtypeStruct((s, LANES), F32)),
        grid=(s // tm, MAIN_W // tn),
        in_specs=[
            pl.BlockSpec((tm, d), lambda i, j: (i, 0)),
            pl.BlockSpec((d, tn), lambda i, j: (0, j)),
            pl.BlockSpec((d, LANES), lambda i, j: (0, 0)),
        ],
        out_specs=(pl.BlockSpec((tm, tn), lambda i, j: (i, j)),
                   pl.BlockSpec((tm, LANES), lambda i, j: (i, 0))),
        scratch_shapes=[pltpu.VMEM((tm, d), BF16)],
        compiler_params=_params("parallel", "arbitrary"),
        name="in_proj",
    )(x, w_all, w_tail)


def _gla_kernel(q_ref, k_ref, v_ref, gr_ref, ga_ref, wa_ref, ba_ref, ng_ref, o_ref, state_ref):
    rows = q_ref.shape[0]
    n_chunks = rows // GLA_CHUNK
    n_sub = GLA_CHUNK // GLA_SUB

    @pl.when(pl.program_id(0) == 0)
    def _():
        state_ref[...] = jnp.zeros_like(state_ref)

    ga_hi, ga_lo = _split2(ga_ref[...])
    wa_hi, wa_lo = _split2(wa_ref[...])
    xa = _dot(ga_hi, wa_hi) + _dot(ga_hi, wa_lo) + _dot(ga_lo, wa_hi) + ba_ref[...]
    la = -_softplus(-xa) * (1.0 / GLA_TAU)

    r_i = lax.broadcasted_iota(jnp.int32, (rows, rows), 0)
    c_i = lax.broadcasted_iota(jnp.int32, (rows, rows), 1)
    chunk_shift = GLA_CHUNK.bit_length() - 1
    same_chunk = (c_i >> chunk_shift) == (r_i >> chunk_shift)
    tri = jnp.where((c_i <= r_i) & same_chunk, 1.0, 0.0).astype(BF16)
    la_hi, la_mid, la_lo = _split3(la)
    b_all = _dot(tri, la_hi) + _dot(tri, la_mid) + _dot(tri, la_lo)

    e_r = lax.broadcasted_iota(jnp.int32, (GLA_QK, GLA_V), 0)
    e_c = lax.broadcasted_iota(jnp.int32, (GLA_QK, GLA_V), 1)
    same_head = (e_r >> GLA_DK.bit_length() - 1) == (e_c >> GLA_DV.bit_length() - 1)
    expand = jnp.where(same_head, 1.0, 0.0).astype(BF16)
    lane_head = lax.broadcasted_iota(jnp.int32, (1, GLA_QK), 1) >> GLA_DK.bit_length() - 1
    head_mask = [jnp.where(lane_head == h, 1.0, 0.0) for h in range(GLA_HEADS)]
    ones_col = jnp.ones((GLA_CHUNK, GLA_DV), BF16)
    t_sub = lax.broadcasted_iota(jnp.int32, (GLA_SUB, GLA_QK), 0)
    s_chunk = lax.broadcasted_iota(jnp.int32, (GLA_CHUNK, GLA_QK), 0)

    def head_rows(x, n):
        return jnp.concatenate(
            [x[h * n:(h + 1) * n, h * GLA_DV:(h + 1) * GLA_DV] for h in range(GLA_HEADS)], axis=1)

    def stack_heads(x):
        return jnp.concatenate([x * head_mask[h] for h in range(GLA_HEADS)], axis=0).astype(BF16)

    for c in range(n_chunks):
        sl = slice(c * GLA_CHUNK, (c + 1) * GLA_CHUNK)
        q_c = q_ref[sl, :].astype(F32) * (GLA_DK ** -0.5)
        v_cb = v_ref[sl, :]
        k_c = k_ref[sl, :].astype(F32)
        v_c = v_cb.astype(F32)
        b_c = b_all[sl, :]
        b_last = b_c[GLA_CHUNK - 1:GLA_CHUNK, :]

        state = state_ref[...]
        inter = _dot(stack_heads(q_c * jnp.exp(b_c)), state.astype(BF16))
        o_inter = jnp.concatenate(
            [inter[h * GLA_CHUNK:(h + 1) * GLA_CHUNK, :] for h in range(GLA_HEADS)], axis=1)

        kv = _dot_tn((k_c * jnp.exp(b_last - b_c)).astype(BF16), v_cb)
        kv = jnp.concatenate(
            [kv[h * GLA_DK:(h + 1) * GLA_DK, h * GLA_DV:(h + 1) * GLA_DV] for h in range(GLA_HEADS)], axis=0)
        tot = _dot_tn(la_hi[sl, :], ones_col) + _dot_tn(la_mid[sl, :], ones_col)
        state_ref[...] = state * jnp.exp(tot) + kv

        o_rows = []
        for i in range(n_sub):
            rs = slice(i * GLA_SUB, (i + 1) * GLA_SUB)
            q_i, k_i, b_i, v_i = q_c[rs, :], k_c[rs, :], b_c[rs, :], v_c[rs, :]
            ps = []
            for s in range(GLA_SUB):
                diff = jnp.where(t_sub >= s, b_i - b_i[s:s + 1, :], -jnp.inf)
                ps.append(q_i * k_i[s:s + 1, :] * jnp.exp(diff))
            att = _dot(jnp.concatenate(ps, axis=0).astype(BF16), expand)
            o_i = att[0:GLA_SUB, :] * v_i[0:1, :]
            for s in range(1, GLA_SUB):
                o_i += att[s * GLA_SUB:(s + 1) * GLA_SUB, :] * v_i[s:s + 1, :]
            if i > 0:
                ref_row = b_c[i * GLA_SUB - 1:i * GLA_SUB, :]
                q_dec = stack_heads(q_i * jnp.exp(b_i - ref_row))
                k_dec = k_c * jnp.exp(jnp.where(s_chunk < i * GLA_SUB, ref_row - b_c, -jnp.inf))
                att_prev = _dot_nt(q_dec, k_dec.astype(BF16))
                o_i += head_rows(_dot(att_prev.astype(BF16), v_cb), GLA_SUB)
            o_rows.append(o_i)
        o_c = jnp.concatenate(o_rows, axis=0) + o_inter

        outs = []
        for h in range(GLA_HEADS):
            o_h = o_c[:, h * GLA_DV:(h + 1) * GLA_DV]
            ms = jnp.mean(o_h * o_h, axis=-1, keepdims=True)
            outs.append(o_h * lax.rsqrt(ms + LN_EPS) * ng_ref[...])
        gr = gr_ref[sl, :].astype(F32)
        o_ref[sl, :] = (jnp.concatenate(outs, axis=-1) * (gr * _sigmoid(gr))).astype(o_ref.dtype)


def _gla(u, tail, wa_pad, b_a, norm_g):
    s = u.shape[0]
    rows = min(GLA_ROWS, s)
    return pl.pallas_call(
        _gla_kernel,
        out_shape=jax.ShapeDtypeStruct((s, GLA_V), BF16),
        grid=(s // rows,),
        in_specs=[
            pl.BlockSpec((rows, GLA_QK), lambda i: (i, 0)),
            pl.BlockSpec((rows, GLA_QK), lambda i: (i, 1)),
            pl.BlockSpec((rows, GLA_V), lambda i: (i, COL_GV // GLA_V)),
            pl.BlockSpec((rows, GLA_V), lambda i: (i, COL_GR // GLA_V)),
            pl.BlockSpec((rows, LANES), lambda i: (i, 0)),
            pl.BlockSpec((LANES, GLA_QK), lambda i: (0, 0)),
            pl.BlockSpec((1, GLA_QK), lambda i: (0, 0)),
            pl.BlockSpec((1, GLA_DV), lambda i: (0, 0)),
        ],
        out_specs=pl.BlockSpec((rows, GLA_V), lambda i: (i, 0)),
        scratch_shapes=[pltpu.VMEM((GLA_QK, GLA_DV), F32)],
        compiler_params=_params("arbitrary"),
        name="gla",
    )(u, u, u, u, tail, wa_pad, b_a.reshape(1, GLA_QK), norm_g.reshape(1, GLA_DV))


def _cumsum_kernel(ff_ref, bf_ref, c_ref, carry_ref):
    rows = ff_ref.shape[0]

    @pl.when(pl.program_id(0) == 0)
    def _():
        carry_ref[...] = jnp.zeros_like(carry_ref)

    log_f = -_softplus(-(ff_ref[...] + bf_ref[...]))
    r_i = lax.broadcasted_iota(jnp.int32, (rows, rows), 0)
    c_i = lax.broadcasted_iota(jnp.int32, (rows, rows), 1)
    tri = jnp.where(c_i <= r_i, 1.0, 0.0).astype(BF16)
    hi, mid, lo = _split3(log_f)
    c = _dot(tri, hi) + _dot(tri, mid) + _dot(tri, lo) + carry_ref[...]
    c_ref[...] = c
    carry_ref[...] = c[rows - 1:rows, :]


def _fox_cumsum(tail, bf_pad):
    s = tail.shape[0]
    rows = min(CUM_ROWS, s)
    return pl.pallas_call(
        _cumsum_kernel,
        out_shape=jax.ShapeDtypeStruct((s, LANES), F32),
        grid=(s // rows,),
        in_specs=[
            pl.BlockSpec((rows, LANES), lambda i: (i, 0)),
            pl.BlockSpec((1, LANES), lambda i: (0, 0)),
        ],
        out_specs=pl.BlockSpec((rows, LANES), lambda i: (i, 0)),
        scratch_shapes=[pltpu.VMEM((1, LANES), F32)],
        compiler_params=_params("arbitrary"),
        name="fox_cumsum",
    )(tail, bf_pad)


def _fox_kernel(q_ref, k_ref, v_ref, c_ref, ck_ref, o_ref, kmax_ref, *, scale):
    tb = q_ref.shape[0]
    heads = q_ref.shape[1] // HEAD_DIM
    qi = pl.program_id(0)
    lanes = [slice(h * HEAD_DIM, (h + 1) * HEAD_DIM) for h in range(heads)]

    @pl.when(qi == 0)
    def _():
        for h in range(heads):
            kf = k_ref[:, lanes[h]].astype(F32)
            kmax_ref[h] = jnp.sqrt(jnp.max(jnp.sum(kf * kf, axis=-1, keepdims=True), axis=0, keepdims=True))

    qs = [q_ref[:, lanes[h]] for h in range(heads)]
    cqs = [c_ref[:, FF_LANE + h:FF_LANE + h + 1] for h in range(heads)]
    bounds = []
    for h in range(heads):
        qf = qs[h].astype(F32)
        bounds.append(jnp.sqrt(jnp.sum(qf * qf, axis=-1, keepdims=True)) * kmax_ref[h] * scale + cqs[h])
    row = lax.broadcasted_iota(jnp.int32, (tb, tb), 0)
    col = lax.broadcasted_iota(jnp.int32, (tb, tb), 1)

    def block(j, state, masked):
        start = pl.multiple_of(j * tb, tb)
        hs = range(heads)
        ss = [_dot_nt(qs[h], k_ref[pl.ds(start, tb), lanes[h]]) for h in hs]
        ss = [ss[h] * scale + cqs[h] - ck_ref[h, j] for h in hs]
        if masked:
            ss = [jnp.where(col <= row, s, MASK_VALUE) for s in ss]
        m_new = [jnp.maximum(state[h][0], jnp.max(ss[h], axis=-1, keepdims=True)) for h in hs]
        ps = [jnp.exp(ss[h] - m_new[h]) for h in hs]
        corr = [jnp.exp(state[h][0] - m_new[h]) for h in hs]
        ls = [corr[h] * state[h][1] + jnp.sum(ps[h], axis=-1, keepdims=True) for h in hs]
        pv = [_dot(ps[h].astype(BF16), v_ref[pl.ds(start, tb), lanes[h]]) for h in hs]
        return tuple((m_new[h], ls[h], corr[h] * state[h][2] + pv[h]) for h in hs)

    def worth(j, state):
        jc = jnp.maximum(j, 0)
        gap = bounds[0] - ck_ref[0, jc][:, tb - 1:tb] - state[0][0]
        for h in range(1, heads):
            gap = jnp.maximum(gap, bounds[h] - ck_ref[h, jc][:, tb - 1:tb] - state[h][0])
        return (jnp.max(gap) > -SKIP_LOG).astype(jnp.int32)

    init = tuple((jnp.full((tb, 1), MASK_VALUE, F32), jnp.zeros((tb, 1), F32), jnp.zeros((tb, HEAD_DIM), F32))
                 for _ in range(heads))
    first = block(qi, init, True)

    def cond(c):
        return (c[0] >= 0) & (c[1] > 0)

    def body(c):
        j, _, state = c
        state = block(j, state, False)
        return j - 1, worth(j - 1, state), state

    _, _, final = lax.while_loop(cond, body, (qi - 1, worth(qi - 1, first), first))
    for h in range(heads):
        _, l, acc = final[h]
        o_ref[:, lanes[h]] = (acc / l).astype(o_ref.dtype)


def _fox(u, c, ck):
    s = u.shape[0]
    tb = min(FOX_BLOCK, s)
    nb = s // tb
    col0 = COL_FOX // FOX_W
    resident = lambda blk: pl.BlockSpec((s, FOX_W), lambda i: (0, blk), pipeline_mode=pl.Buffered(1))
    return pl.pallas_call(
        functools.partial(_fox_kernel, scale=HEAD_DIM ** -0.5),
        out_shape=jax.ShapeDtypeStruct((s, FOX_W), BF16),
        grid=(nb,),
        in_specs=[
            pl.BlockSpec((tb, FOX_W), lambda i: (i, col0)),
            resident(col0 + 1),
            resident(col0 + 2),
            pl.BlockSpec((tb, LANES), lambda i: (i, 0)),
            pl.BlockSpec((FOX_HEADS, nb, 1, tb), lambda i: (0, 0, 0, 0)),
        ],
        out_specs=pl.BlockSpec((tb, FOX_W), lambda i: (i, 0)),
        scratch_shapes=[pltpu.VMEM((FOX_HEADS, 1, 1), F32)],
        compiler_params=_params("arbitrary"),
        name="fox_attn",
    )(u, u, u, c, ck)


def _sb_kernel(q_ref, k_ref, v_ref, o_ref, *, scale):
    tb = q_ref.shape[0]
    heads = q_ref.shape[1] // HEAD_DIM
    qi = pl.program_id(0)
    lanes = [slice(h * HEAD_DIM, (h + 1) * HEAD_DIM) for h in range(heads)]
    qs = [q_ref[:, lanes[h]] for h in range(heads)]
    row = lax.broadcasted_iota(jnp.int32, (tb, tb), 0)
    col = lax.broadcasted_iota(jnp.int32, (tb, tb), 1)
    upper = jnp.where(row > col, 1.0, 0.0).astype(BF16)

    def block(j, state, masked):
        start = pl.multiple_of(j * tb, tb)
        hs = range(heads)
        zs = [_dot_nt(qs[h], k_ref[pl.ds(start, tb), lanes[h]]) * scale for h in hs]
        sps = [_softplus(z) for z in zs]
        if masked:
            valid = col < row
            log_1mb = [jnp.where(valid, -sp, 0.0) for sp in sps]
        else:
            log_1mb = [-sp for sp in sps]
        parts = [_split2(x) for x in log_1mb]
        rest = [_dot(hi, upper) + _dot(lo, upper) for hi, lo in parts]
        a = [jnp.exp(zs[h] - sps[h] + rest[h] + state[h][0]) for h in hs]
        if masked:
            a = [jnp.where(valid, x, 0.0) for x in a]
        pv = [_dot(a[h].astype(BF16), v_ref[pl.ds(start, tb), lanes[h]]) for h in hs]
        return tuple((state[h][0] + rest[h][:, :1] + log_1mb[h][:, :1], state[h][1] + pv[h]) for h in hs)

    def worth(state):
        top = state[0][0]
        for h in range(1, heads):
            top = jnp.maximum(top, state[h][0])
        return (jnp.max(top) > -SKIP_LOG).astype(jnp.int32)

    init = tuple((jnp.zeros((tb, 1), F32), jnp.zeros((tb, HEAD_DIM), F32)) for _ in range(heads))
    first = block(qi, init, True)

    def cond(c):
        return (c[0] >= 0) & (c[1] > 0)

    def body(c):
        j, _, state = c
        state = block(j, state, False)
        return j - 1, worth(state), state

    _, _, final = lax.while_loop(cond, body, (qi - 1, worth(first), first))
    for h in range(heads):
        o_ref[:, lanes[h]] = final[h][1].astype(o_ref.dtype)


def _stick_breaking(u):
    s = u.shape[0]
    tb = min(SB_BLOCK, s)
    col0 = COL_SB // SB_W
    resident = lambda blk: pl.BlockSpec((s, SB_W), lambda i: (0, blk), pipeline_mode=pl.Buffered(1))
    return pl.pallas_call(
        functools.partial(_sb_kernel, scale=HEAD_DIM ** -0.5),
        out_shape=jax.ShapeDtypeStruct((s, SB_W), BF16),
        grid=(s // tb,),
        in_specs=[pl.BlockSpec((tb, SB_W), lambda i: (i, col0)), resident(col0 + 1), resident(col0 + 2)],
        out_specs=pl.BlockSpec((tb, SB_W), lambda i: (i, 0)),
        compiler_params=_params("parallel"),
        name="sb_attn",
    )(u, u, u)


def _merge_kernel(x_ref, og_ref, of_ref, os_ref, wg0_ref, wg1_ref, wg2_ref, wbg_ref, wbf_ref, wbs_ref,
                  wo_ref, g_ref, b_ref, o_ref, xb_ref, acc_ref, *, alpha):
    j = pl.program_id(1)

    @pl.when(j == 0)
    def _():
        xb_ref[...] = x_ref[...].astype(BF16)
        acc_ref[...] = jnp.zeros_like(acc_ref)

    xb = xb_ref[...]
    m = _sigmoid(_dot(xb, wg0_ref[...])) * _dot(og_ref[...], wbg_ref[...])
    m += _sigmoid(_dot(xb, wg1_ref[...])) * _dot(of_ref[...], wbf_ref[...])
    m += _sigmoid(_dot(xb, wg2_ref[...])) * _dot(os_ref[...], wbs_ref[...])
    acc_ref[...] += _dot(m.astype(BF16), wo_ref[...])

    @pl.when(j == pl.num_programs(1) - 1)
    def _():
        y = alpha * x_ref[...] + acc_ref[...]
        o_ref[...] = _layernorm(y, g_ref[...], b_ref[...])


def _merge_ln(x, o_gla, o_fox, o_sb, w_all, w_bg, w_bf, w_bs, w_out, g, b, layer, alpha):
    s, d = x.shape
    tm, tn = min(MERGE_TM, s), min(MERGE_TN, d)
    nj = d // tn
    gate0 = MAIN_W // tn
    row_spec = lambda w: pl.BlockSpec((tm, w), lambda i, j: (i, 0))
    gate_spec = lambda br: pl.BlockSpec((d, tn), lambda i, j: (0, gate0 + br * nj + j))
    col_spec = lambda w: pl.BlockSpec((None, w, tn), lambda i, j: (layer, 0, j))
    vec_spec = pl.BlockSpec((None, 1, d), lambda i, j: (layer, 0, 0))
    return pl.pallas_call(
        functools.partial(_merge_kernel, alpha=alpha),
        out_shape=jax.ShapeDtypeStruct((s, d), F32),
        grid=(s // tm, nj),
        in_specs=[
            row_spec(d), row_spec(GLA_V), row_spec(FOX_W), row_spec(SB_W),
            gate_spec(0), gate_spec(1), gate_spec(2),
            col_spec(GLA_V), col_spec(FOX_W), col_spec(SB_W),
            pl.BlockSpec((None, tn, d), lambda i, j: (layer, j, 0)),
            vec_spec, vec_spec,
        ],
        out_specs=pl.BlockSpec((tm, d), lambda i, j: (i, 0)),
        scratch_shapes=[pltpu.VMEM((tm, d), BF16), pltpu.VMEM((tm, d), F32)],
        compiler_params=_params("parallel", "arbitrary"),
        name="merge_ln",
    )(x, o_gla, o_fox, o_sb, w_all, w_all, w_all, w_bg, w_bf, w_bs, w_out, g[:, None, :], b[:, None, :])


def _mixer_branches(x, w_in, layer, gla_w_a2, gla_b_a, gla_norm_g, fox_b_f):
    s, d = x.shape
    assert (N_BRANCH * d) % REGROUP_W == 0 and w_in.shape[2] == MAIN_W + GLA_RANK + FOX_HEADS + N_BRANCH * d
    w_all = _regroup(w_in, layer)
    u, tail = _proj(x, w_all, _tail_weights(w_in, layer))

    wa_pad = jnp.zeros((LANES, GLA_QK), F32).at[:GLA_RANK].set(gla_w_a2)
    o_gla = _gla(u, tail, wa_pad, gla_b_a, gla_norm_g)

    bf_pad = jnp.zeros((1, LANES), F32).at[0, FF_LANE:FF_LANE + FOX_HEADS].set(fox_b_f)
    c = _fox_cumsum(tail, bf_pad)
    ct = c[:, FF_LANE:FF_LANE + FOX_HEADS].T
    tb = min(FOX_BLOCK, s)
    o_fox = _fox(u, c, ct.reshape(FOX_HEADS, s // tb, 1, tb))
    o_sb = _stick_breaking(u)
    return o_gla, o_fox, o_sb, w_all


def _mixer_ln(x, w_in, layer, gla_w_a2, gla_b_a, gla_norm_g, fox_b_f, w_bg, w_bf, w_bs, w_out, ln_g, ln_b, alpha):
    o_gla, o_fox, o_sb, w_all = _mixer_branches(x, w_in, layer, gla_w_a2, gla_b_a, gla_norm_g, fox_b_f)
    return _merge_ln(x, o_gla, o_fox, o_sb, w_all, w_bg, w_bf, w_bs, w_out, ln_g, ln_b, layer, alpha)


def kernel(x, ffn1_w_gate, ffn1_w_up, ffn1_w_down, ln1_g, ln1_b, w_in, gla_w_a2, gla_b_a, gla_norm_g, fox_b_f, w_br_gla, w_br_fox, w_br_sb, w_out, ln2_g, ln2_b, ffn2_w_gate, ffn2_w_up, ffn2_w_down, ln3_g, ln3_b):
    batch, seq, d = x.shape
    depth = w_in.shape[0]
    alpha = (2 * depth) ** 0.25
    cast = lambda w: w.astype(BF16)
    ffn1 = (ffn1_w_gate, ffn1_w_up, ffn1_w_down)
    ffn2 = (ffn2_w_gate, ffn2_w_up, ffn2_w_down)
    branch = (cast(w_br_gla), cast(w_br_fox), cast(w_br_sb), cast(w_out))
    outs = []
    for bi in range(batch):
        h = x[bi]
        for l in range(depth):
            h = _ffn_ln(h, *ffn1, ln1_g, ln1_b, l, alpha)
            h = _mixer_ln(h, w_in, l, gla_w_a2[l], gla_b_a[l], gla_norm_g[l], fox_b_f[l], *branch,
                          ln2_g, ln2_b, alpha)
            h = _ffn_ln(h, *ffn2, ln3_g, ln3_b, l, alpha)
        outs.append(h)
    return jnp.stack(outs, axis=0)
```

```python
import functools

import jax
import jax.numpy as jnp
from jax import lax
from jax.experimental import pallas as pl
from jax.experimental.pallas import tpu as pltpu

F32 = jnp.float32
BF16 = jnp.bfloat16

LN_EPS = 1e-5
GLA_HEADS = 4
GLA_DK = 64
GLA_DV = 128
GLA_RANK = 16
GLA_TAU = 16.0
GLA_CHUNK = 64
GLA_SUB = 16
GLA_QK = GLA_HEADS * GLA_DK
GLA_V = GLA_HEADS * GLA_DV
FOX_HEADS = 6
SB_HEADS = 6
HEAD_DIM = 128
FOX_W = FOX_HEADS * HEAD_DIM
SB_W = SB_HEADS * HEAD_DIM
N_BRANCH = 3
LANES = 128

REGROUP_W = 256
MAIN_W = 2 * GLA_QK + 2 * GLA_V + 3 * FOX_W + 3 * SB_W
COL_GV = 2 * GLA_QK
COL_GR = COL_GV + GLA_V
COL_FOX = COL_GR + GLA_V
COL_SB = COL_FOX + 3 * FOX_W
FF_LANE = GLA_RANK

FFN_TM = 1024
FFN_TF = 256
PROJ_TM = 1024
PROJ_TN = 512
MERGE_TM = 512
MERGE_TN = 256
GLA_ROWS = 256
CUM_ROWS = 256
FOX_BLOCK = 256
SB_BLOCK = 128
VMEM_LIMIT = 52 * 1024 * 1024
MASK_VALUE = -1e30
SKIP_LOG = 30.0


def _dot(a, b):
    return jnp.dot(a, b, preferred_element_type=F32)


def _dot_nt(a, b):
    return lax.dot_general(a, b, (((1,), (1,)), ((), ())), preferred_element_type=F32)


def _dot_tn(a, b):
    return lax.dot_general(a, b, (((0,), (0,)), ((), ())), preferred_element_type=F32)


def _split2(x):
    hi = x.astype(BF16)
    lo = (x - hi.astype(F32)).astype(BF16)
    return hi, lo


def _split3(x):
    hi = x.astype(BF16)
    r = x - hi.astype(F32)
    mid = r.astype(BF16)
    lo = (r - mid.astype(F32)).astype(BF16)
    return hi, mid, lo


def _softplus(x):
    return jnp.maximum(x, 0.0) + jnp.log(1.0 + jnp.exp(-jnp.abs(x)))


def _sigmoid(x):
    return 1.0 / (1.0 + jnp.exp(-x))


def _layernorm(y, g, b):
    mu = jnp.mean(y, axis=-1, keepdims=True)
    d = y - mu
    var = jnp.mean(d * d, axis=-1, keepdims=True)
    return d * lax.rsqrt(var + LN_EPS) * g + b


def _params(*sem):
    return pltpu.CompilerParams(dimension_semantics=sem, vmem_limit_bytes=VMEM_LIMIT)


def _ffn_kernel(x_ref, wg_ref, wu_ref, wd_ref, g_ref, b_ref, o_ref, xb_ref, *, alpha):
    f = pl.program_id(1)

    @pl.when(f == 0)
    def _():
        x = x_ref[...]
        xb_ref[...] = x.astype(BF16)
        o_ref[...] = (2.0 * alpha) * x

    xb = xb_ref[...]
    gate = _dot(xb, wg_ref[...].astype(BF16))
    up = _dot(xb, wu_ref[...].astype(BF16))
    h = gate * _sigmoid(gate) * up
    o_ref[...] += _dot(h.astype(BF16), wd_ref[...].astype(BF16))

    @pl.when(f == pl.num_programs(1) - 1)
    def _():
        o_ref[...] = _layernorm(0.5 * o_ref[...], g_ref[...], b_ref[...])


def _ffn_ln(x, wg, wu, wd, g, b, layer, alpha):
    s, d = x.shape
    f = wg.shape[2]
    tm, tf = min(FFN_TM, s), min(FFN_TF, f)
    vec_spec = pl.BlockSpec((None, 1, d), lambda i, j: (layer, 0, 0))
    return pl.pallas_call(
        functools.partial(_ffn_kernel, alpha=alpha),
        out_shape=jax.ShapeDtypeStruct((s, d), F32),
        grid=(s // tm, f // tf),
        in_specs=[
            pl.BlockSpec((tm, d), lambda i, j: (i, 0), pipeline_mode=pl.Buffered(1)),
            pl.BlockSpec((None, d, tf), lambda i, j: (layer, 0, j)),
            pl.BlockSpec((None, d, tf), lambda i, j: (layer, 0, j)),
            pl.BlockSpec((None, tf, d), lambda i, j: (layer, j, 0)),
            vec_spec, vec_spec,
        ],
        out_specs=pl.BlockSpec((tm, d), lambda i, j: (i, 0)),
        scratch_shapes=[pltpu.VMEM((tm, d), BF16)],
        compiler_params=_params("parallel", "arbitrary"),
        name="ffn_ln",
    )(x, wg, wu, wd, g[:, None, :], b[:, None, :])


def _column_rows(w_in):
    n_layers, d, cols = w_in.shape
    n_slices = d // LANES
    rows = jnp.transpose(w_in.reshape(n_layers, n_slices, LANES, cols), (3, 1, 0, 2))
    return rows.reshape(cols * n_slices * n_layers, LANES), n_slices


def _read_piece(ref, slice_idx, layer, n_cols, n_slices, n_layers):
    stride = n_slices * n_layers
    return ref[pl.ds(slice_idx * n_layers + layer, n_cols, stride=stride), :].T


def _regroup_kernel(w_ref, o_ref, *, n_slices):
    n_layers, _, tn = o_ref.shape
    for layer in range(n_layers):
        for t in range(n_slices):
            piece = _read_piece(w_ref, t, layer, tn, n_slices, n_layers)
            o_ref[layer, t * LANES:(t + 1) * LANES, :] = piece.astype(o_ref.dtype)


def _regroup(w_in):
    n_layers, d, cols = w_in.shape
    n_out = MAIN_W + N_BRANCH * d
    tn = REGROUP_W
    w_rows, n_slices = _column_rows(w_in)
    per_col = n_slices * n_layers
    n_plain, n_fox = COL_FOX // tn, 3 * FOX_W // tn

    def source(j):
        skipped = jnp.where(j < n_plain, 0, jnp.where(j < n_plain + n_fox, GLA_RANK, GLA_RANK + FOX_HEADS))
        return (j * tn + skipped) * per_col, 0

    return pl.pallas_call(
        functools.partial(_regroup_kernel, n_slices=n_slices),
        out_shape=jax.ShapeDtypeStruct((n_layers, d, n_out), BF16),
        grid=(n_out // tn,),
        in_specs=[pl.BlockSpec((pl.Element(tn * per_col), pl.Element(LANES)), source)],
        out_specs=pl.BlockSpec((n_layers, d, tn), lambda j: (0, 0, j)),
        compiler_params=_params("parallel"),
        name="regroup_w_in",
    )(w_rows)


def _tail_kernel(a_ref, b_ref, o_ref, *, n_slices):
    n_layers = o_ref.shape[0]
    lane = lax.broadcasted_iota(jnp.int32, (LANES, LANES), 1)
    for layer in range(n_layers):
        for t in range(n_slices):
            a = _read_piece(a_ref, t, layer, LANES, n_slices, n_layers)
            b = _read_piece(b_ref, t, layer, LANES, n_slices, n_layers)
            tail = jnp.where(lane < GLA_RANK, a, jnp.where(lane < GLA_RANK + FOX_HEADS, b, 0.0))
            o_ref[layer, t * LANES:(t + 1) * LANES, :] = tail.astype(o_ref.dtype)


def _tail_weights(w_in):
    n_layers, d, _ = w_in.shape
    w_rows, n_slices = _column_rows(w_in)
    blk_rows = LANES * n_slices * n_layers
    ga_blk = COL_FOX // LANES
    ff_blk = (COL_FOX + GLA_RANK + 3 * FOX_W) // LANES
    return pl.pallas_call(
        functools.partial(_tail_kernel, n_slices=n_slices),
        out_shape=jax.ShapeDtypeStruct((n_layers, d, LANES), BF16),
        grid=(1,),
        in_specs=[
            pl.BlockSpec((blk_rows, LANES), lambda i: (ga_blk, 0)),
            pl.BlockSpec((blk_rows, LANES), lambda i: (ff_blk, 0)),
        ],
        out_specs=pl.BlockSpec((n_layers, d, LANES), lambda i: (0, 0, 0)),
        compiler_params=_params("arbitrary"),
        name="tail_w_in",
    )(w_rows, w_rows)


def _proj_kernel(x_ref, w_ref, wt_ref, o_ref, t_ref, xb_ref):
    @pl.when(pl.program_id(1) == 0)
    def _():
        xb_ref[...] = x_ref[...].astype(BF16)
        t_ref[...] = _dot(xb_ref[...], wt_ref[...])

    o_ref[...] = _dot(xb_ref[...], w_ref[...]).astype(o_ref.dtype)


def _proj(x, w_all, w_tail, layer):
    s, d = x.shape
    tm, tn = min(PROJ_TM, s), PROJ_TN
    return pl.pallas_call(
        _proj_kernel,
        out_shape=(jax.ShapeDtypeStruct((s, MAIN_W), BF16), jax.ShapeDtypeStruct((s, LANES), F32)),
        grid=(s // tm, MAIN_W // tn),
        in_specs=[
            pl.BlockSpec((tm, d), lambda i, j: (i, 0)),
            pl.BlockSpec((None, d, tn), lambda i, j: (layer, 0, j)),
            pl.BlockSpec((None, d, LANES), lambda i, j: (layer, 0, 0)),
        ],
        out_specs=(pl.BlockSpec((tm, tn), lambda i, j: (i, j)),
                   pl.BlockSpec((tm, LANES), lambda i, j: (i, 0))),
        scratch_shapes=[pltpu.VMEM((tm, d), BF16)],
        compiler_params=_params("parallel", "arbitrary"),
        name="in_proj",
    )(x, w_all, w_tail)


def _gla_kernel(q_ref, k_ref, v_ref, gr_ref, ga_ref, wa_ref, ba_ref, ng_ref, o_ref, state_ref):
    rows = q_ref.shape[0]
    n_chunks = rows // GLA_CHUNK
    n_sub = GLA_CHUNK // GLA_SUB

    @pl.when(pl.program_id(0) == 0)
    def _():
        state_ref[...] = jnp.zeros_like(state_ref)

    ga_hi, ga_lo = _split2(ga_ref[...])
    wa_hi, wa_lo = _split2(wa_ref[...])
    xa = _dot(ga_hi, wa_hi) + _dot(ga_hi, wa_lo) + _dot(ga_lo, wa_hi) + ba_ref[...]
    la = -_softplus(-xa) * (1.0 / GLA_TAU)

    r_i = lax.broadcasted_iota(jnp.int32, (rows, rows), 0)
    c_i = lax.broadcasted_iota(jnp.int32, (rows, rows), 1)
    chunk_shift = GLA_CHUNK.bit_length() - 1
    same_chunk = (c_i >> chunk_shift) == (r_i >> chunk_shift)
    tri = jnp.where((c_i <= r_i) & same_chunk, 1.0, 0.0).astype(BF16)
    la_hi, la_mid, la_lo = _split3(la)
    b_all = _dot(tri, la_hi) + _dot(tri, la_mid) + _dot(tri, la_lo)

    e_r = lax.broadcasted_iota(jnp.int32, (GLA_QK, GLA_V), 0)
    e_c = lax.broadcasted_iota(jnp.int32, (GLA_QK, GLA_V), 1)
    same_head = (e_r >> GLA_DK.bit_length() - 1) == (e_c >> GLA_DV.bit_length() - 1)
    expand = jnp.where(same_head, 1.0, 0.0).astype(BF16)
    lane_head = lax.broadcasted_iota(jnp.int32, (1, GLA_QK), 1) >> GLA_DK.bit_length() - 1
    head_mask = [jnp.where(lane_head == h, 1.0, 0.0) for h in range(GLA_HEADS)]
    ones_col = jnp.ones((GLA_CHUNK, GLA_DV), BF16)
    t_sub = lax.broadcasted_iota(jnp.int32, (GLA_SUB, GLA_QK), 0)
    s_chunk = lax.broadcasted_iota(jnp.int32, (GLA_CHUNK, GLA_QK), 0)

    def head_rows(x, n):
        return jnp.concatenate(
            [x[h * n:(h + 1) * n, h * GLA_DV:(h + 1) * GLA_DV] for h in range(GLA_HEADS)], axis=1)

    def stack_heads(x):
        return jnp.concatenate([x * head_mask[h] for h in range(GLA_HEADS)], axis=0).astype(BF16)

    for c in range(n_chunks):
        sl = slice(c * GLA_CHUNK, (c + 1) * GLA_CHUNK)
        q_c = q_ref[sl, :].astype(F32) * (GLA_DK ** -0.5)
        v_cb = v_ref[sl, :]
        k_c = k_ref[sl, :].astype(F32)
        v_c = v_cb.astype(F32)
        b_c = b_all[sl, :]
        b_last = b_c[GLA_CHUNK - 1:GLA_CHUNK, :]

        state = state_ref[...]
        inter = _dot(stack_heads(q_c * jnp.exp(b_c)), state.astype(BF16))
        o_inter = jnp.concatenate(
            [inter[h * GLA_CHUNK:(h + 1) * GLA_CHUNK, :] for h in range(GLA_HEADS)], axis=1)

        kv = _dot_tn((k_c * jnp.exp(b_last - b_c)).astype(BF16), v_cb)
        kv = jnp.concatenate(
            [kv[h * GLA_DK:(h + 1) * GLA_DK, h * GLA_DV:(h + 1) * GLA_DV] for h in range(GLA_HEADS)], axis=0)
        tot = _dot_tn(la_hi[sl, :], ones_col) + _dot_tn(la_mid[sl, :], ones_col)
        state_ref[...] = state * jnp.exp(tot) + kv

        o_rows = []
        for i in range(n_sub):
            rs = slice(i * GLA_SUB, (i + 1) * GLA_SUB)
            q_i, k_i, b_i, v_i = q_c[rs, :], k_c[rs, :], b_c[rs, :], v_c[rs, :]
            ps = []
            for s in range(GLA_SUB):
                diff = jnp.where(t_sub >= s, b_i - b_i[s:s + 1, :], -jnp.inf)
                ps.append(q_i * k_i[s:s + 1, :] * jnp.exp(diff))
            att = _dot(jnp.concatenate(ps, axis=0).astype(BF16), expand)
            o_i = att[0:GLA_SUB, :] * v_i[0:1, :]
            for s in range(1, GLA_SUB):
                o_i += att[s * GLA_SUB:(s + 1) * GLA_SUB, :] * v_i[s:s + 1, :]
            if i > 0:
                ref_row = b_c[i * GLA_SUB - 1:i * GLA_SUB, :]
                q_dec = stack_heads(q_i * jnp.exp(b_i - ref_row))
                k_dec = k_c * jnp.exp(jnp.where(s_chunk < i * GLA_SUB, ref_row - b_c, -jnp.inf))
                att_prev = _dot_nt(q_dec, k_dec.astype(BF16))
                o_i += head_rows(_dot(att_prev.astype(BF16), v_cb), GLA_SUB)
            o_rows.append(o_i)
        o_c = jnp.concatenate(o_rows, axis=0) + o_inter

        outs = []
        for h in range(GLA_HEADS):
            o_h = o_c[:, h * GLA_DV:(h + 1) * GLA_DV]
            ms = jnp.mean(o_h * o_h, axis=-1, keepdims=True)
            outs.append(o_h * lax.rsqrt(ms + LN_EPS) * ng_ref[...])
        gr = gr_ref[sl, :].astype(F32)
        o_ref[sl, :] = (jnp.concatenate(outs, axis=-1) * (gr * _sigmoid(gr))).astype(o_ref.dtype)


def _gla(u, tail, wa_pad, b_a, norm_g):
    s = u.shape[0]
    rows = min(GLA_ROWS, s)
    return pl.pallas_call(
        _gla_kernel,
        out_shape=jax.ShapeDtypeStruct((s, GLA_V), BF16),
        grid=(s // rows,),
        in_specs=[
            pl.BlockSpec((rows, GLA_QK), lambda i: (i, 0)),
            pl.BlockSpec((rows, GLA_QK), lambda i: (i, 1)),
            pl.BlockSpec((rows, GLA_V), lambda i: (i, COL_GV // GLA_V)),
            pl.BlockSpec((rows, GLA_V), lambda i: (i, COL_GR // GLA_V)),
            pl.BlockSpec((rows, LANES), lambda i: (i, 0)),
            pl.BlockSpec((LANES, GLA_QK), lambda i: (0, 0)),
            pl.BlockSpec((1, GLA_QK), lambda i: (0, 0)),
            pl.BlockSpec((1, GLA_DV), lambda i: (0, 0)),
        ],
        out_specs=pl.BlockSpec((rows, GLA_V), lambda i: (i, 0)),
        scratch_shapes=[pltpu.VMEM((GLA_QK, GLA_DV), F32)],
        compiler_params=_params("arbitrary"),
        name="gla",
    )(u, u, u, u, tail, wa_pad, b_a.reshape(1, GLA_QK), norm_g.reshape(1, GLA_DV))


def _cumsum_kernel(ff_ref, bf_ref, c_ref, carry_ref):
    rows = ff_ref.shape[0]

    @pl.when(pl.program_id(0) == 0)
    def _():
        carry_ref[...] = jnp.zeros_like(carry_ref)

    log_f = -_softplus(-(ff_ref[...] + bf_ref[...]))
    r_i = lax.broadcasted_iota(jnp.int32, (rows, rows), 0)
    c_i = lax.broadcasted_iota(jnp.int32, (rows, rows), 1)
    tri = jnp.where(c_i <= r_i, 1.0, 0.0).astype(BF16)
    hi, mid, lo = _split3(log_f)
    c = _dot(tri, hi) + _dot(tri, mid) + _dot(tri, lo) + carry_ref[...]
    c_ref[...] = c
    carry_ref[...] = c[rows - 1:rows, :]


def _fox_cumsum(tail, bf_pad):
    s = tail.shape[0]
    rows = min(CUM_ROWS, s)
    return pl.pallas_call(
        _cumsum_kernel,
        out_shape=jax.ShapeDtypeStruct((s, LANES), F32),
        grid=(s // rows,),
        in_specs=[
            pl.BlockSpec((rows, LANES), lambda i: (i, 0)),
            pl.BlockSpec((1, LANES), lambda i: (0, 0)),
        ],
        out_specs=pl.BlockSpec((rows, LANES), lambda i: (i, 0)),
        scratch_shapes=[pltpu.VMEM((1, LANES), F32)],
        compiler_params=_params("arbitrary"),
        name="fox_cumsum",
    )(tail, bf_pad)


def _fox_kernel(q_ref, k_ref, v_ref, c_ref, ck_ref, o_ref, kmax_ref, *, scale):
    tb = q_ref.shape[0]
    heads = q_ref.shape[1] // HEAD_DIM
    qi = pl.program_id(0)
    lanes = [slice(h * HEAD_DIM, (h + 1) * HEAD_DIM) for h in range(heads)]

    @pl.when(qi == 0)
    def _():
        for h in range(heads):
            kf = k_ref[:, lanes[h]].astype(F32)
            kmax_ref[h] = jnp.sqrt(jnp.max(jnp.sum(kf * kf, axis=-1, keepdims=True), axis=0, keepdims=True))

    qs = [q_ref[:, lanes[h]] for h in range(heads)]
    cqs = [c_ref[:, FF_LANE + h:FF_LANE + h + 1] for h in range(heads)]
    bounds = []
    for h in range(heads):
        qf = qs[h].astype(F32)
        bounds.append(jnp.sqrt(jnp.sum(qf * qf, axis=-1, keepdims=True)) * kmax_ref[h] * scale + cqs[h])
    row = lax.broadcasted_iota(jnp.int32, (tb, tb), 0)
    col = lax.broadcasted_iota(jnp.int32, (tb, tb), 1)

    def block(j, state, masked):
        start = pl.multiple_of(j * tb, tb)
        hs = range(heads)
        ss = [_dot_nt(qs[h], k_ref[pl.ds(start, tb), lanes[h]]) for h in hs]
        ss = [ss[h] * scale + cqs[h] - ck_ref[h, j] for h in hs]
        if masked:
            ss = [jnp.where(col <= row, s, MASK_VALUE) for s in ss]
        m_new = [jnp.maximum(state[h][0], jnp.max(ss[h], axis=-1, keepdims=True)) for h in hs]
        ps = [jnp.exp(ss[h] - m_new[h]) for h in hs]
        corr = [jnp.exp(state[h][0] - m_new[h]) for h in hs]
        ls = [corr[h] * state[h][1] + jnp.sum(ps[h], axis=-1, keepdims=True) for h in hs]
        pv = [_dot(ps[h].astype(BF16), v_ref[pl.ds(start, tb), lanes[h]]) for h in hs]
        return tuple((m_new[h], ls[h], corr[h] * state[h][2] + pv[h]) for h in hs)

    def worth(j, state):
        jc = jnp.maximum(j, 0)
        gap = bounds[0] - ck_ref[0, jc][:, tb - 1:tb] - state[0][0]
        for h in range(1, heads):
            gap = jnp.maximum(gap, bounds[h] - ck_ref[h, jc][:, tb - 1:tb] - state[h][0])
        return (jnp.max(gap) > -SKIP_LOG).astype(jnp.int32)

    init = tuple((jnp.full((tb, 1), MASK_VALUE, F32), jnp.zeros((tb, 1), F32), jnp.zeros((tb, HEAD_DIM), F32))
                 for _ in range(heads))
    first = block(qi, init, True)

    def cond(c):
        return (c[0] >= 0) & (c[1] > 0)

    def body(c):
        j, _, state = c
        state = block(j, state, False)
        return j - 1, worth(j - 1, state), state

    _, _, final = lax.while_loop(cond, body, (qi - 1, worth(qi - 1, first), first))
    for h in range(heads):
        _, l, acc = final[h]
        o_ref[:, lanes[h]] = (acc / l).astype(o_ref.dtype)


def _fox(u, c, ck):
    s = u.shape[0]
    tb = min(FOX_BLOCK, s)
    nb = s // tb
    col0 = COL_FOX // FOX_W
    resident = lambda blk: pl.BlockSpec((s, FOX_W), lambda i: (0, blk), pipeline_mode=pl.Buffered(1))
    return pl.pallas_call(
        functools.partial(_fox_kernel, scale=HEAD_DIM ** -0.5),
        out_shape=jax.ShapeDtypeStruct((s, FOX_W), BF16),
        grid=(nb,),
        in_specs=[
            pl.BlockSpec((tb, FOX_W), lambda i: (i, col0)),
            resident(col0 + 1),
            resident(col0 + 2),
            pl.BlockSpec((tb, LANES), lambda i: (i, 0)),
            pl.BlockSpec((FOX_HEADS, nb, 1, tb), lambda i: (0, 0, 0, 0)),
        ],
        out_specs=pl.BlockSpec((tb, FOX_W), lambda i: (i, 0)),
        scratch_shapes=[pltpu.VMEM((FOX_HEADS, 1, 1), F32)],
        compiler_params=_params("arbitrary"),
        name="fox_attn",
    )(u, u, u, c, ck)


def _sb_kernel(q_ref, k_ref, v_ref, o_ref, *, scale):
    tb = q_ref.shape[0]
    heads = q_ref.shape[1] // HEAD_DIM
    qi = pl.program_id(0)
    lanes = [slice(h * HEAD_DIM, (h + 1) * HEAD_DIM) for h in range(heads)]
    qs = [q_ref[:, lanes[h]] for h in range(heads)]
    row = lax.broadcasted_iota(jnp.int32, (tb, tb), 0)
    col = lax.broadcasted_iota(jnp.int32, (tb, tb), 1)
    upper = jnp.where(row > col, 1.0, 0.0).astype(BF16)

    def block(j, state, masked):
        start = pl.multiple_of(j * tb, tb)
        hs = range(heads)
        zs = [_dot_nt(qs[h], k_ref[pl.ds(start, tb), lanes[h]]) * scale for h in hs]
        sps = [_softplus(z) for z in zs]
        if masked:
            valid = col < row
            log_1mb = [jnp.where(valid, -sp, 0.0) for sp in sps]
        else:
            log_1mb = [-sp for sp in sps]
        parts = [_split2(x) for x in log_1mb]
        rest = [_dot(hi, upper) + _dot(lo, upper) for hi, lo in parts]
        a = [jnp.exp(zs[h] - sps[h] + rest[h] + state[h][0]) for h in hs]
        if masked:
            a = [jnp.where(valid, x, 0.0) for x in a]
        pv = [_dot(a[h].astype(BF16), v_ref[pl.ds(start, tb), lanes[h]]) for h in hs]
        return tuple((state[h][0] + rest[h][:, :1] + log_1mb[h][:, :1], state[h][1] + pv[h]) for h in hs)

    def worth(state):
        top = state[0][0]
        for h in range(1, heads):
            top = jnp.maximum(top, state[h][0])
        return (jnp.max(top) > -SKIP_LOG).astype(jnp.int32)

    init = tuple((jnp.zeros((tb, 1), F32), jnp.zeros((tb, HEAD_DIM), F32)) for _ in range(heads))
    first = block(qi, init, True)

    def cond(c):
        return (c[0] >= 0) & (c[1] > 0)

    def body(c):
        j, _, state = c
        state = block(j, state, False)
        return j - 1, worth(state), state

    _, _, final = lax.while_loop(cond, body, (qi - 1, worth(first), first))
    for h in range(heads):
        o_ref[:, lanes[h]] = final[h][1].astype(o_ref.dtype)


def _stick_breaking(u):
    s = u.shape[0]
    tb = min(SB_BLOCK, s)
    col0 = COL_SB // SB_W
    resident = lambda blk: pl.BlockSpec((s, SB_W), lambda i: (0, blk), pipeline_mode=pl.Buffered(1))
    return pl.pallas_call(
        functools.partial(_sb_kernel, scale=HEAD_DIM ** -0.5),
        out_shape=jax.ShapeDtypeStruct((s, SB_W), BF16),
        grid=(s // tb,),
        in_specs=[pl.BlockSpec((tb, SB_W), lambda i: (i, col0)), resident(col0 + 1), resident(col0 + 2)],
        out_specs=pl.BlockSpec((tb, SB_W), lambda i: (i, 0)),
        compiler_params=_params("parallel"),
        name="sb_attn",
    )(u, u, u)


def _merge_kernel(x_ref, og_ref, of_ref, os_ref, wg0_ref, wg1_ref, wg2_ref, wbg_ref, wbf_ref, wbs_ref,
                  wo_ref, g_ref, b_ref, o_ref, xb_ref, acc_ref, *, alpha):
    j = pl.program_id(1)

    @pl.when(j == 0)
    def _():
        xb_ref[...] = x_ref[...].astype(BF16)
        acc_ref[...] = jnp.zeros_like(acc_ref)

    xb = xb_ref[...]
    m = _sigmoid(_dot(xb, wg0_ref[...])) * _dot(og_ref[...], wbg_ref[...])
    m += _sigmoid(_dot(xb, wg1_ref[...])) * _dot(of_ref[...], wbf_ref[...])
    m += _sigmoid(_dot(xb, wg2_ref[...])) * _dot(os_ref[...], wbs_ref[...])
    acc_ref[...] += _dot(m.astype(BF16), wo_ref[...])

    @pl.when(j == pl.num_programs(1) - 1)
    def _():
        y = alpha * x_ref[...] + acc_ref[...]
        o_ref[...] = _layernorm(y, g_ref[...], b_ref[...])


def _merge_ln(x, o_gla, o_fox, o_sb, w_all, w_bg, w_bf, w_bs, w_out, g, b, layer, alpha):
    s, d = x.shape
    tm, tn = min(MERGE_TM, s), min(MERGE_TN, d)
    nj = d // tn
    gate0 = MAIN_W // tn
    row_spec = lambda w: pl.BlockSpec((tm, w), lambda i, j: (i, 0))
    gate_spec = lambda br: pl.BlockSpec((None, d, tn), lambda i, j: (layer, 0, gate0 + br * nj + j))
    col_spec = lambda w: pl.BlockSpec((None, w, tn), lambda i, j: (layer, 0, j))
    vec_spec = pl.BlockSpec((None, 1, d), lambda i, j: (layer, 0, 0))
    return pl.pallas_call(
        functools.partial(_merge_kernel, alpha=alpha),
        out_shape=jax.ShapeDtypeStruct((s, d), F32),
        grid=(s // tm, nj),
        in_specs=[
            row_spec(d), row_spec(GLA_V), row_spec(FOX_W), row_spec(SB_W),
            gate_spec(0), gate_spec(1), gate_spec(2),
            col_spec(GLA_V), col_spec(FOX_W), col_spec(SB_W),
            pl.BlockSpec((None, tn, d), lambda i, j: (layer, j, 0)),
            vec_spec, vec_spec,
        ],
        out_specs=pl.BlockSpec((tm, d), lambda i, j: (i, 0)),
        scratch_shapes=[pltpu.VMEM((tm, d), BF16), pltpu.VMEM((tm, d), F32)],
        compiler_params=_params("parallel", "arbitrary"),
        name="merge_ln",
    )(x, o_gla, o_fox, o_sb, w_all, w_all, w_all, w_bg, w_bf, w_bs, w_out, g[:, None, :], b[:, None, :])


def _mixer_branches(x, w_all, w_tail, layer, gla_w_a2, gla_b_a, gla_norm_g, fox_b_f):
    s, d = x.shape
    u, tail = _proj(x, w_all, w_tail, layer)

    wa_pad = jnp.zeros((LANES, GLA_QK), F32).at[:GLA_RANK].set(gla_w_a2)
    o_gla = _gla(u, tail, wa_pad, gla_b_a, gla_norm_g)

    bf_pad = jnp.zeros((1, LANES), F32).at[0, FF_LANE:FF_LANE + FOX_HEADS].set(fox_b_f)
    c = _fox_cumsum(tail, bf_pad)
    ct = c[:, FF_LANE:FF_LANE + FOX_HEADS].T
    tb = min(FOX_BLOCK, s)
    o_fox = _fox(u, c, ct.reshape(FOX_HEADS, s // tb, 1, tb))
    o_sb = _stick_breaking(u)
    return o_gla, o_fox, o_sb


def _mixer_ln(x, w_all, w_tail, layer, gla_w_a2, gla_b_a, gla_norm_g, fox_b_f, w_bg, w_bf, w_bs, w_out,
              ln_g, ln_b, alpha):
    o_gla, o_fox, o_sb = _mixer_branches(x, w_all, w_tail, layer, gla_w_a2, gla_b_a, gla_norm_g, fox_b_f)
    return _merge_ln(x, o_gla, o_fox, o_sb, w_all, w_bg, w_bf, w_bs, w_out, ln_g, ln_b, layer, alpha)


def kernel(x, ffn1_w_gate, ffn1_w_up, ffn1_w_down, ln1_g, ln1_b, w_in, gla_w_a2, gla_b_a, gla_norm_g, fox_b_f, w_br_gla, w_br_fox, w_br_sb, w_out, ln2_g, ln2_b, ffn2_w_gate, ffn2_w_up, ffn2_w_down, ln3_g, ln3_b):
    batch, seq, d = x.shape
    depth = w_in.shape[0]
    alpha = (2 * depth) ** 0.25
    cast = lambda w: w.astype(BF16)
    ffn1 = (ffn1_w_gate, ffn1_w_up, ffn1_w_down)
    ffn2 = (ffn2_w_gate, ffn2_w_up, ffn2_w_down)
    branch = (cast(w_br_gla), cast(w_br_fox), cast(w_br_sb), cast(w_out))
    assert (N_BRANCH * d) % REGROUP_W == 0 and w_in.shape[2] == MAIN_W + GLA_RANK + FOX_HEADS + N_BRANCH * d
    w_all, w_tail = _regroup(w_in), _tail_weights(w_in)
    outs = []
    for bi in range(batch):
        h = x[bi]
        for l in range(depth):
            h = _ffn_ln(h, *ffn1, ln1_g, ln1_b, l, alpha)
            h = _mixer_ln(h, w_all, w_tail, l, gla_w_a2[l], gla_b_a[l], gla_norm_g[l], fox_b_f[l], *branch,
                          ln2_g, ln2_b, alpha)
            h = _ffn_ln(h, *ffn2, ln3_g, ln3_b, l, alpha)
        outs.append(h)
    return jnp.stack(outs, axis=0)
```

```python
import functools

import jax
import jax.numpy as jnp
from jax import lax
from jax.experimental import pallas as pl
from jax.experimental.pallas import tpu as pltpu

F32 = jnp.float32
BF16 = jnp.bfloat16

LN_EPS = 1e-5
GLA_HEADS = 4
GLA_DK = 64
GLA_DV = 128
GLA_RANK = 16
GLA_TAU = 16.0
GLA_CHUNK = 64
GLA_SUB = 16
GLA_QK = GLA_HEADS * GLA_DK
GLA_V = GLA_HEADS * GLA_DV
FOX_HEADS = 6
SB_HEADS = 6
HEAD_DIM = 128
FOX_W = FOX_HEADS * HEAD_DIM
SB_W = SB_HEADS * HEAD_DIM
N_BRANCH = 3
LANES = 128

REGROUP_W = 256
MAIN_W = 2 * GLA_QK + 2 * GLA_V + 3 * FOX_W + 3 * SB_W
COL_GV = 2 * GLA_QK
COL_GR = COL_GV + GLA_V
COL_FOX = COL_GR + GLA_V
COL_SB = COL_FOX + 3 * FOX_W
FF_LANE = GLA_RANK

FFN_TM = 1024
FFN_TF = 256
PROJ_TM = 1024
PROJ_TN = 512
MERGE_TM = 512
MERGE_TN = 256
GLA_ROWS = 256
CUM_ROWS = 256
FOX_BLOCK = 256
SB_BLOCK = 128
VMEM_LIMIT = 52 * 1024 * 1024
MASK_VALUE = -1e30
SKIP_LOG = 30.0


def _dot(a, b):
    return jnp.dot(a, b, preferred_element_type=F32)


def _dot_nt(a, b):
    return lax.dot_general(a, b, (((1,), (1,)), ((), ())), preferred_element_type=F32)


def _dot_tn(a, b):
    return lax.dot_general(a, b, (((0,), (0,)), ((), ())), preferred_element_type=F32)


def _split2(x):
    hi = x.astype(BF16)
    lo = (x - hi.astype(F32)).astype(BF16)
    return hi, lo


def _split3(x):
    hi = x.astype(BF16)
    r = x - hi.astype(F32)
    mid = r.astype(BF16)
    lo = (r - mid.astype(F32)).astype(BF16)
    return hi, mid, lo


def _softplus(x):
    return jnp.maximum(x, 0.0) + jnp.log(1.0 + jnp.exp(-jnp.abs(x)))


def _sigmoid(x):
    return 1.0 / (1.0 + jnp.exp(-x))


def _layernorm(y, g, b):
    mu = jnp.mean(y, axis=-1, keepdims=True)
    d = y - mu
    var = jnp.mean(d * d, axis=-1, keepdims=True)
    return d * lax.rsqrt(var + LN_EPS) * g + b


def _params(*sem):
    return pltpu.CompilerParams(dimension_semantics=sem, vmem_limit_bytes=VMEM_LIMIT)


def _ffn_kernel(x_ref, wg_ref, wu_ref, wd_ref, g_ref, b_ref, o_ref, xb_ref, *, alpha):
    f = pl.program_id(1)

    @pl.when(f == 0)
    def _():
        x = x_ref[...]
        xb_ref[...] = x.astype(BF16)
        o_ref[...] = (2.0 * alpha) * x

    xb = xb_ref[...]
    gate = _dot(xb, wg_ref[...])
    up = _dot(xb, wu_ref[...])
    h = gate * _sigmoid(gate) * up
    o_ref[...] += _dot(h.astype(BF16), wd_ref[...].astype(BF16))

    @pl.when(f == pl.num_programs(1) - 1)
    def _():
        o_ref[...] = _layernorm(0.5 * o_ref[...], g_ref[...], b_ref[...])


def _ffn_ln(x, wg, wu, wd, g, b, layer, alpha):
    s, d = x.shape
    f = wg.shape[2]
    tm, tf = min(FFN_TM, s), min(FFN_TF, f)
    vec_spec = pl.BlockSpec((None, 1, d), lambda i, j: (layer, 0, 0))
    return pl.pallas_call(
        functools.partial(_ffn_kernel, alpha=alpha),
        out_shape=jax.ShapeDtypeStruct((s, d), F32),
        grid=(s // tm, f // tf),
        in_specs=[
            pl.BlockSpec((tm, d), lambda i, j: (i, 0), pipeline_mode=pl.Buffered(1)),
            pl.BlockSpec((None, d, tf), lambda i, j: (layer, 0, j)),
            pl.BlockSpec((None, d, tf), lambda i, j: (layer, 0, j)),
            pl.BlockSpec((None, tf, d), lambda i, j: (layer, j, 0)),
            vec_spec, vec_spec,
        ],
        out_specs=pl.BlockSpec((tm, d), lambda i, j: (i, 0)),
        scratch_shapes=[pltpu.VMEM((tm, d), BF16)],
        compiler_params=_params("parallel", "arbitrary"),
        name="ffn_ln",
    )(x, wg, wu, wd, g[:, None, :], b[:, None, :])


def _column_rows(w_in):
    n_layers, d, cols = w_in.shape
    n_slices = d // LANES
    rows = jnp.transpose(w_in.reshape(n_layers, n_slices, LANES, cols), (3, 1, 0, 2))
    return rows.reshape(cols * n_slices * n_layers, LANES), n_slices


def _read_piece(ref, slice_idx, layer, n_cols, n_slices, n_layers):
    stride = n_slices * n_layers
    return ref[pl.ds(slice_idx * n_layers + layer, n_cols, stride=stride), :].T


def _regroup_kernel(w_ref, o_ref, *, n_slices):
    n_layers, _, tn = o_ref.shape
    for layer in range(n_layers):
        for t in range(n_slices):
            piece = _read_piece(w_ref, t, layer, tn, n_slices, n_layers)
            o_ref[layer, t * LANES:(t + 1) * LANES, :] = piece.astype(o_ref.dtype)


def _regroup(w_in):
    n_layers, d, cols = w_in.shape
    n_out = MAIN_W + N_BRANCH * d
    tn = REGROUP_W
    w_rows, n_slices = _column_rows(w_in)
    per_col = n_slices * n_layers
    n_plain, n_fox = COL_FOX // tn, 3 * FOX_W // tn

    def source(j):
        skipped = jnp.where(j < n_plain, 0, jnp.where(j < n_plain + n_fox, GLA_RANK, GLA_RANK + FOX_HEADS))
        return (j * tn + skipped) * per_col, 0

    return pl.pallas_call(
        functools.partial(_regroup_kernel, n_slices=n_slices),
        out_shape=jax.ShapeDtypeStruct((n_layers, d, n_out), BF16),
        grid=(n_out // tn,),
        in_specs=[pl.BlockSpec((pl.Element(tn * per_col), pl.Element(LANES)), source)],
        out_specs=pl.BlockSpec((n_layers, d, tn), lambda j: (0, 0, j)),
        compiler_params=_params("parallel"),
        name="regroup_w_in",
    )(w_rows)


def _tail_kernel(a_ref, b_ref, o_ref, *, n_slices):
    n_layers = o_ref.shape[0]
    lane = lax.broadcasted_iota(jnp.int32, (LANES, LANES), 1)
    for layer in range(n_layers):
        for t in range(n_slices):
            a = _read_piece(a_ref, t, layer, LANES, n_slices, n_layers)
            b = _read_piece(b_ref, t, layer, LANES, n_slices, n_layers)
            tail = jnp.where(lane < GLA_RANK, a, jnp.where(lane < GLA_RANK + FOX_HEADS, b, 0.0))
            o_ref[layer, t * LANES:(t + 1) * LANES, :] = tail.astype(o_ref.dtype)


def _tail_weights(w_in):
    n_layers, d, _ = w_in.shape
    w_rows, n_slices = _column_rows(w_in)
    blk_rows = LANES * n_slices * n_layers
    ga_blk = COL_FOX // LANES
    ff_blk = (COL_FOX + GLA_RANK + 3 * FOX_W) // LANES
    return pl.pallas_call(
        functools.partial(_tail_kernel, n_slices=n_slices),
        out_shape=jax.ShapeDtypeStruct((n_layers, d, LANES), BF16),
        grid=(1,),
        in_specs=[
            pl.BlockSpec((blk_rows, LANES), lambda i: (ga_blk, 0)),
            pl.BlockSpec((blk_rows, LANES), lambda i: (ff_blk, 0)),
        ],
        out_specs=pl.BlockSpec((n_layers, d, LANES), lambda i: (0, 0, 0)),
        compiler_params=_params("arbitrary"),
        name="tail_w_in",
    )(w_rows, w_rows)


def _proj_kernel(x_ref, w_ref, wt_ref, o_ref, t_ref, xb_ref):
    @pl.when(pl.program_id(1) == 0)
    def _():
        xb_ref[...] = x_ref[...].astype(BF16)
        t_ref[...] = _dot(xb_ref[...], wt_ref[...])

    o_ref[...] = _dot(xb_ref[...], w_ref[...]).astype(o_ref.dtype)


def _proj(x, w_all, w_tail, layer):
    s, d = x.shape
    tm, tn = min(PROJ_TM, s), PROJ_TN
    return pl.pallas_call(
        _proj_kernel,
        out_shape=(jax.ShapeDtypeStruct((s, MAIN_W), BF16), jax.ShapeDtypeStruct((s, LANES), F32)),
        grid=(s // tm, MAIN_W // tn),
        in_specs=[
            pl.BlockSpec((tm, d), lambda i, j: (i, 0)),
            pl.BlockSpec((None, d, tn), lambda i, j: (layer, 0, j)),
            pl.BlockSpec((None, d, LANES), lambda i, j: (layer, 0, 0)),
        ],
        out_specs=(pl.BlockSpec((tm, tn), lambda i, j: (i, j)),
                   pl.BlockSpec((tm, LANES), lambda i, j: (i, 0))),
        scratch_shapes=[pltpu.VMEM((tm, d), BF16)],
        compiler_params=_params("parallel", "arbitrary"),
        name="in_proj",
    )(x, w_all, w_tail)


def _gla_kernel(q_ref, k_ref, v_ref, gr_ref, ga_ref, wa_ref, ba_ref, ng_ref, o_ref, state_ref):
    rows = q_ref.shape[0]
    n_chunks = rows // GLA_CHUNK
    n_sub = GLA_CHUNK // GLA_SUB

    @pl.when(pl.program_id(0) == 0)
    def _():
        state_ref[...] = jnp.zeros_like(state_ref)

    ga_hi, ga_lo = _split2(ga_ref[...])
    wa_hi, wa_lo = _split2(wa_ref[...])
    xa = _dot(ga_hi, wa_hi) + _dot(ga_hi, wa_lo) + _dot(ga_lo, wa_hi) + ba_ref[...]
    la = -_softplus(-xa) * (1.0 / GLA_TAU)

    r_i = lax.broadcasted_iota(jnp.int32, (rows, rows), 0)
    c_i = lax.broadcasted_iota(jnp.int32, (rows, rows), 1)
    chunk_shift = GLA_CHUNK.bit_length() - 1
    same_chunk = (c_i >> chunk_shift) == (r_i >> chunk_shift)
    tri = jnp.where((c_i <= r_i) & same_chunk, 1.0, 0.0).astype(BF16)
    la_hi, la_mid, la_lo = _split3(la)
    b_all = _dot(tri, la_hi) + _dot(tri, la_mid) + _dot(tri, la_lo)

    e_r = lax.broadcasted_iota(jnp.int32, (GLA_QK, GLA_V), 0)
    e_c = lax.broadcasted_iota(jnp.int32, (GLA_QK, GLA_V), 1)
    same_head = (e_r >> GLA_DK.bit_length() - 1) == (e_c >> GLA_DV.bit_length() - 1)
    expand = jnp.where(same_head, 1.0, 0.0).astype(BF16)
    lane_head = lax.broadcasted_iota(jnp.int32, (1, GLA_QK), 1) >> GLA_DK.bit_length() - 1
    head_mask = [jnp.where(lane_head == h, 1.0, 0.0) for h in range(GLA_HEADS)]
    ones_col = jnp.ones((GLA_CHUNK, GLA_DV), BF16)
    t_sub = lax.broadcasted_iota(jnp.int32, (GLA_SUB, GLA_QK), 0)
    s_chunk = lax.broadcasted_iota(jnp.int32, (GLA_CHUNK, GLA_QK), 0)

    def head_rows(x, n):
        return jnp.concatenate(
            [x[h * n:(h + 1) * n, h * GLA_DV:(h + 1) * GLA_DV] for h in range(GLA_HEADS)], axis=1)

    def stack_heads(x):
        return jnp.concatenate([x * head_mask[h] for h in range(GLA_HEADS)], axis=0).astype(BF16)

    for c in range(n_chunks):
        sl = slice(c * GLA_CHUNK, (c + 1) * GLA_CHUNK)
        q_c = q_ref[sl, :].astype(F32) * (GLA_DK ** -0.5)
        v_cb = v_ref[sl, :]
        k_c = k_ref[sl, :].astype(F32)
        v_c = v_cb.astype(F32)
        b_c = b_all[sl, :]
        b_last = b_c[GLA_CHUNK - 1:GLA_CHUNK, :]

        state = state_ref[...]
        inter = _dot(stack_heads(q_c * jnp.exp(b_c)), state.astype(BF16))
        o_inter = jnp.concatenate(
            [inter[h * GLA_CHUNK:(h + 1) * GLA_CHUNK, :] for h in range(GLA_HEADS)], axis=1)

        kv = _dot_tn((k_c * jnp.exp(b_last - b_c)).astype(BF16), v_cb)
        kv = jnp.concatenate(
            [kv[h * GLA_DK:(h + 1) * GLA_DK, h * GLA_DV:(h + 1) * GLA_DV] for h in range(GLA_HEADS)], axis=0)
        tot = _dot_tn(la_hi[sl, :], ones_col) + _dot_tn(la_mid[sl, :], ones_col)
        state_ref[...] = state * jnp.exp(tot) + kv

        o_rows = []
        for i in range(n_sub):
            rs = slice(i * GLA_SUB, (i + 1) * GLA_SUB)
            q_i, k_i, b_i, v_i = q_c[rs, :], k_c[rs, :], b_c[rs, :], v_c[rs, :]
            ps = []
            for s in range(GLA_SUB):
                diff = jnp.where(t_sub >= s, b_i - b_i[s:s + 1, :], -jnp.inf)
                ps.append(q_i * k_i[s:s + 1, :] * jnp.exp(diff))
            att = _dot(jnp.concatenate(ps, axis=0).astype(BF16), expand)
            o_i = att[0:GLA_SUB, :] * v_i[0:1, :]
            for s in range(1, GLA_SUB):
                o_i += att[s * GLA_SUB:(s + 1) * GLA_SUB, :] * v_i[s:s + 1, :]
            if i > 0:
                ref_row = b_c[i * GLA_SUB - 1:i * GLA_SUB, :]
                q_dec = stack_heads(q_i * jnp.exp(b_i - ref_row))
                k_dec = k_c * jnp.exp(jnp.where(s_chunk < i * GLA_SUB, ref_row - b_c, -jnp.inf))
                att_prev = _dot_nt(q_dec, k_dec.astype(BF16))
                o_i += head_rows(_dot(att_prev.astype(BF16), v_cb), GLA_SUB)
            o_rows.append(o_i)
        o_c = jnp.concatenate(o_rows, axis=0) + o_inter

        outs = []
        for h in range(GLA_HEADS):
            o_h = o_c[:, h * GLA_DV:(h + 1) * GLA_DV]
            ms = jnp.mean(o_h * o_h, axis=-1, keepdims=True)
            outs.append(o_h * lax.rsqrt(ms + LN_EPS) * ng_ref[...])
        gr = gr_ref[sl, :].astype(F32)
        o_ref[sl, :] = (jnp.concatenate(outs, axis=-1) * (gr * _sigmoid(gr))).astype(o_ref.dtype)


def _gla(u, tail, wa_pad, b_a, norm_g):
    s = u.shape[0]
    rows = min(GLA_ROWS, s)
    return pl.pallas_call(
        _gla_kernel,
        out_shape=jax.ShapeDtypeStruct((s, GLA_V), BF16),
        grid=(s // rows,),
        in_specs=[
            pl.BlockSpec((rows, GLA_QK), lambda i: (i, 0)),
            pl.BlockSpec((rows, GLA_QK), lambda i: (i, 1)),
            pl.BlockSpec((rows, GLA_V), lambda i: (i, COL_GV // GLA_V)),
            pl.BlockSpec((rows, GLA_V), lambda i: (i, COL_GR // GLA_V)),
            pl.BlockSpec((rows, LANES), lambda i: (i, 0)),
            pl.BlockSpec((LANES, GLA_QK), lambda i: (0, 0)),
            pl.BlockSpec((1, GLA_QK), lambda i: (0, 0)),
            pl.BlockSpec((1, GLA_DV), lambda i: (0, 0)),
        ],
        out_specs=pl.BlockSpec((rows, GLA_V), lambda i: (i, 0)),
        scratch_shapes=[pltpu.VMEM((GLA_QK, GLA_DV), F32)],
        compiler_params=_params("arbitrary"),
        name="gla",
    )(u, u, u, u, tail, wa_pad, b_a.reshape(1, GLA_QK), norm_g.reshape(1, GLA_DV))


def _cumsum_kernel(ff_ref, bf_ref, c_ref, carry_ref):
    rows = ff_ref.shape[0]

    @pl.when(pl.program_id(0) == 0)
    def _():
        carry_ref[...] = jnp.zeros_like(carry_ref)

    log_f = -_softplus(-(ff_ref[...] + bf_ref[...]))
    r_i = lax.broadcasted_iota(jnp.int32, (rows, rows), 0)
    c_i = lax.broadcasted_iota(jnp.int32, (rows, rows), 1)
    tri = jnp.where(c_i <= r_i, 1.0, 0.0).astype(BF16)
    hi, mid, lo = _split3(log_f)
    c = _dot(tri, hi) + _dot(tri, mid) + _dot(tri, lo) + carry_ref[...]
    c_ref[...] = c
    carry_ref[...] = c[rows - 1:rows, :]


def _fox_cumsum(tail, bf_pad):
    s = tail.shape[0]
    rows = min(CUM_ROWS, s)
    return pl.pallas_call(
        _cumsum_kernel,
        out_shape=jax.ShapeDtypeStruct((s, LANES), F32),
        grid=(s // rows,),
        in_specs=[
            pl.BlockSpec((rows, LANES), lambda i: (i, 0)),
            pl.BlockSpec((1, LANES), lambda i: (0, 0)),
        ],
        out_specs=pl.BlockSpec((rows, LANES), lambda i: (i, 0)),
        scratch_shapes=[pltpu.VMEM((1, LANES), F32)],
        compiler_params=_params("arbitrary"),
        name="fox_cumsum",
    )(tail, bf_pad)


def _fox_kernel(q_ref, k_ref, v_ref, c_ref, ck_ref, o_ref, kmax_ref, *, scale):
    tb = q_ref.shape[0]
    heads = q_ref.shape[1] // HEAD_DIM
    qi = pl.program_id(0)
    lanes = [slice(h * HEAD_DIM, (h + 1) * HEAD_DIM) for h in range(heads)]

    @pl.when(qi == 0)
    def _():
        for h in range(heads):
            kf = k_ref[:, lanes[h]].astype(F32)
            kmax_ref[h] = jnp.sqrt(jnp.max(jnp.sum(kf * kf, axis=-1, keepdims=True), axis=0, keepdims=True))

    qs = [q_ref[:, lanes[h]] for h in range(heads)]
    cqs = [c_ref[:, FF_LANE + h:FF_LANE + h + 1] for h in range(heads)]
    bounds = []
    for h in range(heads):
        qf = qs[h].astype(F32)
        bounds.append(jnp.sqrt(jnp.sum(qf * qf, axis=-1, keepdims=True)) * kmax_ref[h] * scale + cqs[h])
    row = lax.broadcasted_iota(jnp.int32, (tb, tb), 0)
    col = lax.broadcasted_iota(jnp.int32, (tb, tb), 1)

    def block(j, state, masked):
        start = pl.multiple_of(j * tb, tb)
        hs = range(heads)
        ss = [_dot_nt(qs[h], k_ref[pl.ds(start, tb), lanes[h]]) for h in hs]
        ss = [ss[h] * scale + cqs[h] - ck_ref[h, j] for h in hs]
        if masked:
            ss = [jnp.where(col <= row, s, MASK_VALUE) for s in ss]
        m_new = [jnp.maximum(state[h][0], jnp.max(ss[h], axis=-1, keepdims=True)) for h in hs]
        ps = [jnp.exp(ss[h] - m_new[h]) for h in hs]
        corr = [jnp.exp(state[h][0] - m_new[h]) for h in hs]
        ls = [corr[h] * state[h][1] + jnp.sum(ps[h], axis=-1, keepdims=True) for h in hs]
        pv = [_dot(ps[h].astype(BF16), v_ref[pl.ds(start, tb), lanes[h]]) for h in hs]
        return tuple((m_new[h], ls[h], corr[h] * state[h][2] + pv[h]) for h in hs)

    def worth(j, state):
        jc = jnp.maximum(j, 0)
        gap = bounds[0] - ck_ref[0, jc][:, tb - 1:tb] - state[0][0]
        for h in range(1, heads):
            gap = jnp.maximum(gap, bounds[h] - ck_ref[h, jc][:, tb - 1:tb] - state[h][0])
        return (jnp.max(gap) > -SKIP_LOG).astype(jnp.int32)

    init = tuple((jnp.full((tb, 1), MASK_VALUE, F32), jnp.zeros((tb, 1), F32), jnp.zeros((tb, HEAD_DIM), F32))
                 for _ in range(heads))
    first = block(qi, init, True)

    def cond(c):
        return (c[0] >= 0) & (c[1] > 0)

    def body(c):
        j, _, state = c
        state = block(j, state, False)
        return j - 1, worth(j - 1, state), state

    _, _, final = lax.while_loop(cond, body, (qi - 1, worth(qi - 1, first), first))
    for h in range(heads):
        _, l, acc = final[h]
        o_ref[:, lanes[h]] = (acc / l).astype(o_ref.dtype)


def _fox(u, c, ck):
    s = u.shape[0]
    tb = min(FOX_BLOCK, s)
    nb = s // tb
    col0 = COL_FOX // FOX_W
    resident = lambda blk: pl.BlockSpec((s, FOX_W), lambda i: (0, blk), pipeline_mode=pl.Buffered(1))
    return pl.pallas_call(
        functools.partial(_fox_kernel, scale=HEAD_DIM ** -0.5),
        out_shape=jax.ShapeDtypeStruct((s, FOX_W), BF16),
        grid=(nb,),
        in_specs=[
            pl.BlockSpec((tb, FOX_W), lambda i: (i, col0)),
            resident(col0 + 1),
            resident(col0 + 2),
            pl.BlockSpec((tb, LANES), lambda i: (i, 0)),
            pl.BlockSpec((FOX_HEADS, nb, 1, tb), lambda i: (0, 0, 0, 0)),
        ],
        out_specs=pl.BlockSpec((tb, FOX_W), lambda i: (i, 0)),
        scratch_shapes=[pltpu.VMEM((FOX_HEADS, 1, 1), F32)],
        compiler_params=_params("arbitrary"),
        name="fox_attn",
    )(u, u, u, c, ck)


def _sb_kernel(q_ref, k_ref, v_ref, o_ref, *, scale):
    tb = q_ref.shape[0]
    heads = q_ref.shape[1] // HEAD_DIM
    qi = pl.program_id(0)
    lanes = [slice(h * HEAD_DIM, (h + 1) * HEAD_DIM) for h in range(heads)]
    qs = [q_ref[:, lanes[h]] for h in range(heads)]
    row = lax.broadcasted_iota(jnp.int32, (tb, tb), 0)
    col = lax.broadcasted_iota(jnp.int32, (tb, tb), 1)
    upper = jnp.where(row > col, 1.0, 0.0).astype(BF16)

    def block(j, state, masked):
        start = pl.multiple_of(j * tb, tb)
        hs = range(heads)
        zs = [_dot_nt(qs[h], k_ref[pl.ds(start, tb), lanes[h]]) * scale for h in hs]
        sps = [_softplus(z) for z in zs]
        if masked:
            valid = col < row
            log_1mb = [jnp.where(valid, -sp, 0.0) for sp in sps]
        else:
            log_1mb = [-sp for sp in sps]
        parts = [_split2(x) for x in log_1mb]
        rest = [_dot(hi, upper) + _dot(lo, upper) for hi, lo in parts]
        a = [jnp.exp(zs[h] - sps[h] + rest[h] + state[h][0]) for h in hs]
        if masked:
            a = [jnp.where(valid, x, 0.0) for x in a]
        pv = [_dot(a[h].astype(BF16), v_ref[pl.ds(start, tb), lanes[h]]) for h in hs]
        return tuple((state[h][0] + rest[h][:, :1] + log_1mb[h][:, :1], state[h][1] + pv[h]) for h in hs)

    def worth(state):
        top = state[0][0]
        for h in range(1, heads):
            top = jnp.maximum(top, state[h][0])
        return (jnp.max(top) > -SKIP_LOG).astype(jnp.int32)

    init = tuple((jnp.zeros((tb, 1), F32), jnp.zeros((tb, HEAD_DIM), F32)) for _ in range(heads))
    first = block(qi, init, True)

    def cond(c):
        return (c[0] >= 0) & (c[1] > 0)

    def body(c):
        j, _, state = c
        state = block(j, state, False)
        return j - 1, worth(state), state

    _, _, final = lax.while_loop(cond, body, (qi - 1, worth(first), first))
    for h in range(heads):
        o_ref[:, lanes[h]] = final[h][1].astype(o_ref.dtype)


def _stick_breaking(u):
    s = u.shape[0]
    tb = min(SB_BLOCK, s)
    col0 = COL_SB // SB_W
    resident = lambda blk: pl.BlockSpec((s, SB_W), lambda i: (0, blk), pipeline_mode=pl.Buffered(1))
    return pl.pallas_call(
        functools.partial(_sb_kernel, scale=HEAD_DIM ** -0.5),
        out_shape=jax.ShapeDtypeStruct((s, SB_W), BF16),
        grid=(s // tb,),
        in_specs=[pl.BlockSpec((tb, SB_W), lambda i: (i, col0)), resident(col0 + 1), resident(col0 + 2)],
        out_specs=pl.BlockSpec((tb, SB_W), lambda i: (i, 0)),
        compiler_params=_params("parallel"),
        name="sb_attn",
    )(u, u, u)


def _merge_kernel(x_ref, og_ref, of_ref, os_ref, wg0_ref, wg1_ref, wg2_ref, wbg_ref, wbf_ref, wbs_ref,
                  wo_ref, g_ref, b_ref, o_ref, xb_ref, acc_ref, *, alpha):
    j = pl.program_id(1)

    @pl.when(j == 0)
    def _():
        xb_ref[...] = x_ref[...].astype(BF16)
        acc_ref[...] = jnp.zeros_like(acc_ref)

    xb = xb_ref[...]
    m = _sigmoid(_dot(xb, wg0_ref[...])) * _dot(og_ref[...], wbg_ref[...])
    m += _sigmoid(_dot(xb, wg1_ref[...])) * _dot(of_ref[...], wbf_ref[...])
    m += _sigmoid(_dot(xb, wg2_ref[...])) * _dot(os_ref[...], wbs_ref[...])
    acc_ref[...] += _dot(m.astype(BF16), wo_ref[...])

    @pl.when(j == pl.num_programs(1) - 1)
    def _():
        y = alpha * x_ref[...] + acc_ref[...]
        o_ref[...] = _layernorm(y, g_ref[...], b_ref[...])


def _merge_ln(x, o_gla, o_fox, o_sb, w_all, w_bg, w_bf, w_bs, w_out, g, b, layer, alpha):
    s, d = x.shape
    tm, tn = min(MERGE_TM, s), min(MERGE_TN, d)
    nj = d // tn
    gate0 = MAIN_W // tn
    row_spec = lambda w: pl.BlockSpec((tm, w), lambda i, j: (i, 0))
    gate_spec = lambda br: pl.BlockSpec((None, d, tn), lambda i, j: (layer, 0, gate0 + br * nj + j))
    col_spec = lambda w: pl.BlockSpec((None, w, tn), lambda i, j: (layer, 0, j))
    vec_spec = pl.BlockSpec((None, 1, d), lambda i, j: (layer, 0, 0))
    return pl.pallas_call(
        functools.partial(_merge_kernel, alpha=alpha),
        out_shape=jax.ShapeDtypeStruct((s, d), F32),
        grid=(s // tm, nj),
        in_specs=[
            row_spec(d), row_spec(GLA_V), row_spec(FOX_W), row_spec(SB_W),
            gate_spec(0), gate_spec(1), gate_spec(2),
            col_spec(GLA_V), col_spec(FOX_W), col_spec(SB_W),
            pl.BlockSpec((None, tn, d), lambda i, j: (layer, j, 0)),
            vec_spec, vec_spec,
        ],
        out_specs=pl.BlockSpec((tm, d), lambda i, j: (i, 0)),
        scratch_shapes=[pltpu.VMEM((tm, d), BF16), pltpu.VMEM((tm, d), F32)],
        compiler_params=_params("parallel", "arbitrary"),
        name="merge_ln",
    )(x, o_gla, o_fox, o_sb, w_all, w_all, w_all, w_bg, w_bf, w_bs, w_out, g[:, None, :], b[:, None, :])


def _mixer_branches(x, w_all, w_tail, layer, gla_w_a2, gla_b_a, gla_norm_g, fox_b_f):
    s, d = x.shape
    u, tail = _proj(x, w_all, w_tail, layer)

    wa_pad = jnp.zeros((LANES, GLA_QK), F32).at[:GLA_RANK].set(gla_w_a2)
    o_gla = _gla(u, tail, wa_pad, gla_b_a, gla_norm_g)

    bf_pad = jnp.zeros((1, LANES), F32).at[0, FF_LANE:FF_LANE + FOX_HEADS].set(fox_b_f)
    c = _fox_cumsum(tail, bf_pad)
    ct = c[:, FF_LANE:FF_LANE + FOX_HEADS].T
    tb = min(FOX_BLOCK, s)
    o_fox = _fox(u, c, ct.reshape(FOX_HEADS, s // tb, 1, tb))
    o_sb = _stick_breaking(u)
    return o_gla, o_fox, o_sb


def _mixer_ln(x, w_all, w_tail, layer, gla_w_a2, gla_b_a, gla_norm_g, fox_b_f, w_bg, w_bf, w_bs, w_out,
              ln_g, ln_b, alpha):
    o_gla, o_fox, o_sb = _mixer_branches(x, w_all, w_tail, layer, gla_w_a2, gla_b_a, gla_norm_g, fox_b_f)
    return _merge_ln(x, o_gla, o_fox, o_sb, w_all, w_bg, w_bf, w_bs, w_out, ln_g, ln_b, layer, alpha)


def kernel(x, ffn1_w_gate, ffn1_w_up, ffn1_w_down, ln1_g, ln1_b, w_in, gla_w_a2, gla_b_a, gla_norm_g, fox_b_f, w_br_gla, w_br_fox, w_br_sb, w_out, ln2_g, ln2_b, ffn2_w_gate, ffn2_w_up, ffn2_w_down, ln3_g, ln3_b):
    batch, seq, d = x.shape
    depth = w_in.shape[0]
    alpha = (2 * depth) ** 0.25
    cast = lambda w: w.astype(BF16)
    ffn1 = (cast(ffn1_w_gate), cast(ffn1_w_up), ffn1_w_down)
    ffn2 = (cast(ffn2_w_gate), cast(ffn2_w_up), ffn2_w_down)
    branch = (cast(w_br_gla), cast(w_br_fox), cast(w_br_sb), cast(w_out))
    assert (N_BRANCH * d) % REGROUP_W == 0 and w_in.shape[2] == MAIN_W + GLA_RANK + FOX_HEADS + N_BRANCH * d
    w_all, w_tail = _regroup(w_in), _tail_weights(w_in)
    outs = []
    for bi in range(batch):
        h = x[bi]
        for l in range(depth):
            h = _ffn_ln(h, *ffn1, ln1_g, ln1_b, l, alpha)
            h = _mixer_ln(h, w_all, w_tail, l, gla_w_a2[l], gla_b_a[l], gla_norm_g[l], fox_b_f[l], *branch,
                          ln2_g, ln2_b, alpha)
            h = _ffn_ln(h, *ffn2, ln3_g, ln3_b, l, alpha)
        outs.append(h)
    return jnp.stack(outs, axis=0)
```

```python
import functools

import jax
import jax.numpy as jnp
from jax import lax
from jax.experimental import pallas as pl
from jax.experimental.pallas import tpu as pltpu

F32 = jnp.float32
BF16 = jnp.bfloat16

LN_EPS = 1e-5
GLA_HEADS = 4
GLA_DK = 64
GLA_DV = 128
GLA_RANK = 16
GLA_TAU = 16.0
GLA_CHUNK = 64
GLA_SUB = 16
GLA_QK = GLA_HEADS * GLA_DK
GLA_V = GLA_HEADS * GLA_DV
FOX_HEADS = 6
SB_HEADS = 6
HEAD_DIM = 128
FOX_W = FOX_HEADS * HEAD_DIM
SB_W = SB_HEADS * HEAD_DIM
N_BRANCH = 3
LANES = 128

REGROUP_W = 256
MAIN_W = 2 * GLA_QK + 2 * GLA_V + 3 * FOX_W + 3 * SB_W
COL_GV = 2 * GLA_QK
COL_GR = COL_GV + GLA_V
COL_FOX = COL_GR + GLA_V
COL_SB = COL_FOX + 3 * FOX_W
FF_LANE = GLA_RANK

FFN_TM = 1024
FFN_TF = 256
PROJ_TM = 2048
PROJ_TN = 512
MERGE_TM = 512
MERGE_TN = 512
GLA_ROWS = 256
CUM_ROWS = 256
FOX_BLOCK = 256
SB_BLOCK = 128
VMEM_LIMIT = 52 * 1024 * 1024
MASK_VALUE = -1e30
SKIP_LOG = 30.0


def _dot(a, b):
    return jnp.dot(a, b, preferred_element_type=F32)


def _dot_nt(a, b):
    return lax.dot_general(a, b, (((1,), (1,)), ((), ())), preferred_element_type=F32)


def _dot_tn(a, b):
    return lax.dot_general(a, b, (((0,), (0,)), ((), ())), preferred_element_type=F32)


def _split2(x):
    hi = x.astype(BF16)
    lo = (x - hi.astype(F32)).astype(BF16)
    return hi, lo


def _split3(x):
    hi = x.astype(BF16)
    r = x - hi.astype(F32)
    mid = r.astype(BF16)
    lo = (r - mid.astype(F32)).astype(BF16)
    return hi, mid, lo


def _softplus(x):
    return jnp.maximum(x, 0.0) + jnp.log(1.0 + jnp.exp(-jnp.abs(x)))


def _sigmoid(x):
    return 1.0 / (1.0 + jnp.exp(-x))


def _layernorm(y, g, b):
    mu = jnp.mean(y, axis=-1, keepdims=True)
    d = y - mu
    var = jnp.mean(d * d, axis=-1, keepdims=True)
    return d * lax.rsqrt(var + LN_EPS) * g + b


def _params(*sem):
    return pltpu.CompilerParams(dimension_semantics=sem, vmem_limit_bytes=VMEM_LIMIT)


def _ffn_kernel(x_ref, wg_ref, wu_ref, wd_ref, g_ref, b_ref, o_ref, xb_ref, *, alpha):
    f = pl.program_id(1)

    @pl.when(f == 0)
    def _():
        x = x_ref[...]
        xb_ref[...] = x.astype(BF16)
        o_ref[...] = (2.0 * alpha) * x

    xb = xb_ref[...]
    gate = _dot(xb, wg_ref[...].astype(BF16))
    up = _dot(xb, wu_ref[...].astype(BF16))
    h = gate * _sigmoid(gate) * up
    o_ref[...] += _dot(h.astype(BF16), wd_ref[...].astype(BF16))

    @pl.when(f == pl.num_programs(1) - 1)
    def _():
        o_ref[...] = _layernorm(0.5 * o_ref[...], g_ref[...], b_ref[...])


def _ffn_ln(x, wg, wu, wd, g, b, layer, alpha):
    s, d = x.shape
    f = wg.shape[2]
    tm, tf = min(FFN_TM, s), min(FFN_TF, f)
    vec_spec = pl.BlockSpec((None, 1, d), lambda i, j: (layer, 0, 0))
    return pl.pallas_call(
        functools.partial(_ffn_kernel, alpha=alpha),
        out_shape=jax.ShapeDtypeStruct((s, d), F32),
        grid=(s // tm, f // tf),
        in_specs=[
            pl.BlockSpec((tm, d), lambda i, j: (i, 0), pipeline_mode=pl.Buffered(1)),
            pl.BlockSpec((None, d, tf), lambda i, j: (layer, 0, j)),
            pl.BlockSpec((None, d, tf), lambda i, j: (layer, 0, j)),
            pl.BlockSpec((None, tf, d), lambda i, j: (layer, j, 0)),
            vec_spec, vec_spec,
        ],
        out_specs=pl.BlockSpec((tm, d), lambda i, j: (i, 0)),
        scratch_shapes=[pltpu.VMEM((tm, d), BF16)],
        compiler_params=_params("parallel", "arbitrary"),
        name="ffn_ln",
    )(x, wg, wu, wd, g[:, None, :], b[:, None, :])


def _column_rows(w_in):
    n_layers, d, cols = w_in.shape
    n_slices = d // LANES
    rows = jnp.transpose(w_in.reshape(n_layers, n_slices, LANES, cols), (3, 1, 0, 2))
    return rows.reshape(cols * n_slices * n_layers, LANES), n_slices


def _read_piece(ref, slice_idx, layer, n_cols, n_slices, n_layers):
    stride = n_slices * n_layers
    return ref[pl.ds(slice_idx * n_layers + layer, n_cols, stride=stride), :].T


def _regroup_kernel(w_ref, o_ref, *, n_slices):
    n_layers, _, tn = o_ref.shape
    for layer in range(n_layers):
        for t in range(n_slices):
            piece = _read_piece(w_ref, t, layer, tn, n_slices, n_layers)
            o_ref[layer, t * LANES:(t + 1) * LANES, :] = piece.astype(o_ref.dtype)


def _regroup(w_in):
    n_layers, d, cols = w_in.shape
    n_out = MAIN_W + N_BRANCH * d
    tn = REGROUP_W
    w_rows, n_slices = _column_rows(w_in)
    per_col = n_slices * n_layers
    n_plain, n_fox = COL_FOX // tn, 3 * FOX_W // tn

    def source(j):
        skipped = jnp.where(j < n_plain, 0, jnp.where(j < n_plain + n_fox, GLA_RANK, GLA_RANK + FOX_HEADS))
        return (j * tn + skipped) * per_col, 0

    return pl.pallas_call(
        functools.partial(_regroup_kernel, n_slices=n_slices),
        out_shape=jax.ShapeDtypeStruct((n_layers, d, n_out), BF16),
        grid=(n_out // tn,),
        in_specs=[pl.BlockSpec((pl.Element(tn * per_col), pl.Element(LANES)), source)],
        out_specs=pl.BlockSpec((n_layers, d, tn), lambda j: (0, 0, j)),
        compiler_params=_params("parallel"),
        name="regroup_w_in",
    )(w_rows)


def _tail_kernel(a_ref, b_ref, o_ref, *, n_slices):
    n_layers = o_ref.shape[0]
    lane = lax.broadcasted_iota(jnp.int32, (LANES, LANES), 1)
    for layer in range(n_layers):
        for t in range(n_slices):
            a = _read_piece(a_ref, t, layer, LANES, n_slices, n_layers)
            b = _read_piece(b_ref, t, layer, LANES, n_slices, n_layers)
            tail = jnp.where(lane < GLA_RANK, a, jnp.where(lane < GLA_RANK + FOX_HEADS, b, 0.0))
            o_ref[layer, t * LANES:(t + 1) * LANES, :] = tail.astype(o_ref.dtype)


def _tail_weights(w_in):
    n_layers, d, _ = w_in.shape
    w_rows, n_slices = _column_rows(w_in)
    blk_rows = LANES * n_slices * n_layers
    ga_blk = COL_FOX // LANES
    ff_blk = (COL_FOX + GLA_RANK + 3 * FOX_W) // LANES
    return pl.pallas_call(
        functools.partial(_tail_kernel, n_slices=n_slices),
        out_shape=jax.ShapeDtypeStruct((n_layers, d, LANES), BF16),
        grid=(1,),
        in_specs=[
            pl.BlockSpec((blk_rows, LANES), lambda i: (ga_blk, 0)),
            pl.BlockSpec((blk_rows, LANES), lambda i: (ff_blk, 0)),
        ],
        out_specs=pl.BlockSpec((n_layers, d, LANES), lambda i: (0, 0, 0)),
        compiler_params=_params("arbitrary"),
        name="tail_w_in",
    )(w_rows, w_rows)


def _proj_kernel(x_ref, w_ref, wt_ref, o_ref, t_ref, xb_ref):
    @pl.when(pl.program_id(1) == 0)
    def _():
        xb_ref[...] = x_ref[...].astype(BF16)
        t_ref[...] = _dot(xb_ref[...], wt_ref[...])

    o_ref[...] = _dot(xb_ref[...], w_ref[...]).astype(o_ref.dtype)


def _proj(x, w_all, w_tail, layer):
    s, d = x.shape
    tm, tn = min(PROJ_TM, s), PROJ_TN
    return pl.pallas_call(
        _proj_kernel,
        out_shape=(jax.ShapeDtypeStruct((s, MAIN_W), BF16), jax.ShapeDtypeStruct((s, LANES), F32)),
        grid=(s // tm, MAIN_W // tn),
        in_specs=[
            pl.BlockSpec((tm, d), lambda i, j: (i, 0), pipeline_mode=pl.Buffered(1)),
            pl.BlockSpec((None, d, tn), lambda i, j: (layer, 0, j)),
            pl.BlockSpec((None, d, LANES), lambda i, j: (layer, 0, 0)),
        ],
        out_specs=(pl.BlockSpec((tm, tn), lambda i, j: (i, j)),
                   pl.BlockSpec((tm, LANES), lambda i, j: (i, 0))),
        scratch_shapes=[pltpu.VMEM((tm, d), BF16)],
        compiler_params=_params("parallel", "arbitrary"),
        name="in_proj",
    )(x, w_all, w_tail)


def _gla_kernel(q_ref, k_ref, v_ref, gr_ref, ga_ref, wa_ref, ba_ref, ng_ref, o_ref, state_ref):
    rows = q_ref.shape[0]
    n_chunks = rows // GLA_CHUNK
    n_sub = GLA_CHUNK // GLA_SUB

    @pl.when(pl.program_id(0) == 0)
    def _():
        state_ref[...] = jnp.zeros_like(state_ref)

    ga_hi, ga_lo = _split2(ga_ref[...])
    wa_hi, wa_lo = _split2(wa_ref[...])
    xa = _dot(ga_hi, wa_hi) + _dot(ga_hi, wa_lo) + _dot(ga_lo, wa_hi) + ba_ref[...]
    la = -_softplus(-xa) * (1.0 / GLA_TAU)

    r_i = lax.broadcasted_iota(jnp.int32, (rows, rows), 0)
    c_i = lax.broadcasted_iota(jnp.int32, (rows, rows), 1)
    chunk_shift = GLA_CHUNK.bit_length() - 1
    same_chunk = (c_i >> chunk_shift) == (r_i >> chunk_shift)
    tri = jnp.where((c_i <= r_i) & same_chunk, 1.0, 0.0).astype(BF16)
    la_hi, la_mid, la_lo = _split3(la)
    b_all = _dot(tri, la_hi) + _dot(tri, la_mid) + _dot(tri, la_lo)

    e_r = lax.broadcasted_iota(jnp.int32, (GLA_QK, GLA_V), 0)
    e_c = lax.broadcasted_iota(jnp.int32, (GLA_QK, GLA_V), 1)
    same_head = (e_r >> GLA_DK.bit_length() - 1) == (e_c >> GLA_DV.bit_length() - 1)
    expand = jnp.where(same_head, 1.0, 0.0).astype(BF16)
    lane_head = lax.broadcasted_iota(jnp.int32, (1, GLA_QK), 1) >> GLA_DK.bit_length() - 1
    head_mask = [jnp.where(lane_head == h, 1.0, 0.0) for h in range(GLA_HEADS)]
    ones_col = jnp.ones((GLA_CHUNK, GLA_DV), BF16)
    t_sub = lax.broadcasted_iota(jnp.int32, (GLA_SUB, GLA_QK), 0)
    s_chunk = lax.broadcasted_iota(jnp.int32, (GLA_CHUNK, GLA_QK), 0)

    def head_rows(x, n):
        return jnp.concatenate(
            [x[h * n:(h + 1) * n, h * GLA_DV:(h + 1) * GLA_DV] for h in range(GLA_HEADS)], axis=1)

    def stack_heads(x):
        return jnp.concatenate([x * head_mask[h] for h in range(GLA_HEADS)], axis=0).astype(BF16)

    for c in range(n_chunks):
        sl = slice(c * GLA_CHUNK, (c + 1) * GLA_CHUNK)
        q_c = q_ref[sl, :].astype(F32) * (GLA_DK ** -0.5)
        v_cb = v_ref[sl, :]
        k_c = k_ref[sl, :].astype(F32)
        v_c = v_cb.astype(F32)
        b_c = b_all[sl, :]
        b_last = b_c[GLA_CHUNK - 1:GLA_CHUNK, :]

        state = state_ref[...]
        inter = _dot(stack_heads(q_c * jnp.exp(b_c)), state.astype(BF16))
        o_inter = jnp.concatenate(
            [inter[h * GLA_CHUNK:(h + 1) * GLA_CHUNK, :] for h in range(GLA_HEADS)], axis=1)

        kv = _dot_tn((k_c * jnp.exp(b_last - b_c)).astype(BF16), v_cb)
        kv = jnp.concatenate(
            [kv[h * GLA_DK:(h + 1) * GLA_DK, h * GLA_DV:(h + 1) * GLA_DV] for h in range(GLA_HEADS)], axis=0)
        tot = _dot_tn(la_hi[sl, :], ones_col) + _dot_tn(la_mid[sl, :], ones_col)
        state_ref[...] = state * jnp.exp(tot) + kv

        o_rows = []
        for i in range(n_sub):
            rs = slice(i * GLA_SUB, (i + 1) * GLA_SUB)
            q_i, k_i, b_i, v_i = q_c[rs, :], k_c[rs, :], b_c[rs, :], v_c[rs, :]
            ps = []
            for s in range(GLA_SUB):
                diff = jnp.where(t_sub >= s, b_i - b_i[s:s + 1, :], -jnp.inf)
                ps.append(q_i * k_i[s:s + 1, :] * jnp.exp(diff))
            att = _dot(jnp.concatenate(ps, axis=0).astype(BF16), expand)
            o_i = att[0:GLA_SUB, :] * v_i[0:1, :]
            for s in range(1, GLA_SUB):
                o_i += att[s * GLA_SUB:(s + 1) * GLA_SUB, :] * v_i[s:s + 1, :]
            if i > 0:
                ref_row = b_c[i * GLA_SUB - 1:i * GLA_SUB, :]
                q_dec = stack_heads(q_i * jnp.exp(b_i - ref_row))
                k_dec = k_c * jnp.exp(jnp.where(s_chunk < i * GLA_SUB, ref_row - b_c, -jnp.inf))
                att_prev = _dot_nt(q_dec, k_dec.astype(BF16))
                o_i += head_rows(_dot(att_prev.astype(BF16), v_cb), GLA_SUB)
            o_rows.append(o_i)
        o_c = jnp.concatenate(o_rows, axis=0) + o_inter

        outs = []
        for h in range(GLA_HEADS):
            o_h = o_c[:, h * GLA_DV:(h + 1) * GLA_DV]
            ms = jnp.mean(o_h * o_h, axis=-1, keepdims=True)
            outs.append(o_h * lax.rsqrt(ms + LN_EPS) * ng_ref[...])
        gr = gr_ref[sl, :].astype(F32)
        o_ref[sl, :] = (jnp.concatenate(outs, axis=-1) * (gr * _sigmoid(gr))).astype(o_ref.dtype)


def _gla(u, tail, wa_pad, b_a, norm_g):
    s = u.shape[0]
    rows = min(GLA_ROWS, s)
    return pl.pallas_call(
        _gla_kernel,
        out_shape=jax.ShapeDtypeStruct((s, GLA_V), BF16),
        grid=(s // rows,),
        in_specs=[
            pl.BlockSpec((rows, GLA_QK), lambda i: (i, 0)),
            pl.BlockSpec((rows, GLA_QK), lambda i: (i, 1)),
            pl.BlockSpec((rows, GLA_V), lambda i: (i, COL_GV // GLA_V)),
            pl.BlockSpec((rows, GLA_V), lambda i: (i, COL_GR // GLA_V)),
            pl.BlockSpec((rows, LANES), lambda i: (i, 0)),
            pl.BlockSpec((LANES, GLA_QK), lambda i: (0, 0)),
            pl.BlockSpec((1, GLA_QK), lambda i: (0, 0)),
            pl.BlockSpec((1, GLA_DV), lambda i: (0, 0)),
        ],
        out_specs=pl.BlockSpec((rows, GLA_V), lambda i: (i, 0)),
        scratch_shapes=[pltpu.VMEM((GLA_QK, GLA_DV), F32)],
        compiler_params=_params("arbitrary"),
        name="gla",
    )(u, u, u, u, tail, wa_pad, b_a.reshape(1, GLA_QK), norm_g.reshape(1, GLA_DV))


def _cumsum_kernel(ff_ref, bf_ref, c_ref, carry_ref):
    rows = ff_ref.shape[0]

    @pl.when(pl.program_id(0) == 0)
    def _():
        carry_ref[...] = jnp.zeros_like(carry_ref)

    log_f = -_softplus(-(ff_ref[...] + bf_ref[...]))
    r_i = lax.broadcasted_iota(jnp.int32, (rows, rows), 0)
    c_i = lax.broadcasted_iota(jnp.int32, (rows, rows), 1)
    tri = jnp.where(c_i <= r_i, 1.0, 0.0).astype(BF16)
    hi, mid, lo = _split3(log_f)
    c = _dot(tri, hi) + _dot(tri, mid) + _dot(tri, lo) + carry_ref[...]
    c_ref[...] = c
    carry_ref[...] = c[rows - 1:rows, :]


def _fox_cumsum(tail, bf_pad):
    s = tail.shape[0]
    rows = min(CUM_ROWS, s)
    return pl.pallas_call(
        _cumsum_kernel,
        out_shape=jax.ShapeDtypeStruct((s, LANES), F32),
        grid=(s // rows,),
        in_specs=[
            pl.BlockSpec((rows, LANES), lambda i: (i, 0)),
            pl.BlockSpec((1, LANES), lambda i: (0, 0)),
        ],
        out_specs=pl.BlockSpec((rows, LANES), lambda i: (i, 0)),
        scratch_shapes=[pltpu.VMEM((1, LANES), F32)],
        compiler_params=_params("arbitrary"),
        name="fox_cumsum",
    )(tail, bf_pad)


def _fox_kernel(q_ref, k_ref, v_ref, c_ref, ck_ref, o_ref, kmax_ref, *, scale):
    tb = q_ref.shape[0]
    heads = q_ref.shape[1] // HEAD_DIM
    qi = pl.program_id(0)
    lanes = [slice(h * HEAD_DIM, (h + 1) * HEAD_DIM) for h in range(heads)]

    @pl.when(qi == 0)
    def _():
        for h in range(heads):
            kf = k_ref[:, lanes[h]].astype(F32)
            kmax_ref[h] = jnp.sqrt(jnp.max(jnp.sum(kf * kf, axis=-1, keepdims=True), axis=0, keepdims=True))

    qs = [q_ref[:, lanes[h]] for h in range(heads)]
    cqs = [c_ref[:, FF_LANE + h:FF_LANE + h + 1] for h in range(heads)]
    bounds = []
    for h in range(heads):
        qf = qs[h].astype(F32)
        bounds.append(jnp.sqrt(jnp.sum(qf * qf, axis=-1, keepdims=True)) * kmax_ref[h] * scale + cqs[h])
    row = lax.broadcasted_iota(jnp.int32, (tb, tb), 0)
    col = lax.broadcasted_iota(jnp.int32, (tb, tb), 1)

    def block(j, state, masked):
        start = pl.multiple_of(j * tb, tb)
        hs = range(heads)
        ss = [_dot_nt(qs[h], k_ref[pl.ds(start, tb), lanes[h]]) for h in hs]
        ss = [ss[h] * scale + cqs[h] - ck_ref[h, j] for h in hs]
        if masked:
            ss = [jnp.where(col <= row, s, MASK_VALUE) for s in ss]
        m_new = [jnp.maximum(state[h][0], jnp.max(ss[h], axis=-1, keepdims=True)) for h in hs]
        ps = [jnp.exp(ss[h] - m_new[h]) for h in hs]
        corr = [jnp.exp(state[h][0] - m_new[h]) for h in hs]
        ls = [corr[h] * state[h][1] + jnp.sum(ps[h], axis=-1, keepdims=True) for h in hs]
        pv = [_dot(ps[h].astype(BF16), v_ref[pl.ds(start, tb), lanes[h]]) for h in hs]
        return tuple((m_new[h], ls[h], corr[h] * state[h][2] + pv[h]) for h in hs)

    def worth(j, state):
        jc = jnp.maximum(j, 0)
        gap = bounds[0] - ck_ref[0, jc][:, tb - 1:tb] - state[0][0]
        for h in range(1, heads):
            gap = jnp.maximum(gap, bounds[h] - ck_ref[h, jc][:, tb - 1:tb] - state[h][0])
        return (jnp.max(gap) > -SKIP_LOG).astype(jnp.int32)

    init = tuple((jnp.full((tb, 1), MASK_VALUE, F32), jnp.zeros((tb, 1), F32), jnp.zeros((tb, HEAD_DIM), F32))
                 for _ in range(heads))
    first = block(qi, init, True)

    def cond(c):
        return (c[0] >= 0) & (c[1] > 0)

    def body(c):
        j, _, state = c
        state = block(j, state, False)
        return j - 1, worth(j - 1, state), state

    _, _, final = lax.while_loop(cond, body, (qi - 1, worth(qi - 1, first), first))
    for h in range(heads):
        _, l, acc = final[h]
        o_ref[:, lanes[h]] = (acc / l).astype(o_ref.dtype)


def _fox(u, c, ck):
    s = u.shape[0]
    tb = min(FOX_BLOCK, s)
    nb = s // tb
    col0 = COL_FOX // FOX_W
    resident = lambda blk: pl.BlockSpec((s, FOX_W), lambda i: (0, blk), pipeline_mode=pl.Buffered(1))
    return pl.pallas_call(
        functools.partial(_fox_kernel, scale=HEAD_DIM ** -0.5),
        out_shape=jax.ShapeDtypeStruct((s, FOX_W), BF16),
        grid=(nb,),
        in_specs=[
            pl.BlockSpec((tb, FOX_W), lambda i: (i, col0)),
            resident(col0 + 1),
            resident(col0 + 2),
            pl.BlockSpec((tb, LANES), lambda i: (i, 0)),
            pl.BlockSpec((FOX_HEADS, nb, 1, tb), lambda i: (0, 0, 0, 0)),
        ],
        out_specs=pl.BlockSpec((tb, FOX_W), lambda i: (i, 0)),
        scratch_shapes=[pltpu.VMEM((FOX_HEADS, 1, 1), F32)],
        compiler_params=_params("arbitrary"),
        name="fox_attn",
    )(u, u, u, c, ck)


def _sb_kernel(q_ref, k_ref, v_ref, o_ref, *, scale):
    tb = q_ref.shape[0]
    heads = q_ref.shape[1] // HEAD_DIM
    qi = pl.program_id(0)
    lanes = [slice(h * HEAD_DIM, (h + 1) * HEAD_DIM) for h in range(heads)]
    qs = [q_ref[:, lanes[h]] for h in range(heads)]
    row = lax.broadcasted_iota(jnp.int32, (tb, tb), 0)
    col = lax.broadcasted_iota(jnp.int32, (tb, tb), 1)
    upper = jnp.where(row > col, 1.0, 0.0).astype(BF16)

    def block(j, state, masked):
        start = pl.multiple_of(j * tb, tb)
        hs = range(heads)
        zs = [_dot_nt(qs[h], k_ref[pl.ds(start, tb), lanes[h]]) * scale for h in hs]
        sps = [_softplus(z) for z in zs]
        if masked:
            valid = col < row
            log_1mb = [jnp.where(valid, -sp, 0.0) for sp in sps]
        else:
            log_1mb = [-sp for sp in sps]
        parts = [_split2(x) for x in log_1mb]
        rest = [_dot(hi, upper) + _dot(lo, upper) for hi, lo in parts]
        a = [jnp.exp(zs[h] - sps[h] + rest[h] + state[h][0]) for h in hs]
        if masked:
            a = [jnp.where(valid, x, 0.0) for x in a]
        pv = [_dot(a[h].astype(BF16), v_ref[pl.ds(start, tb), lanes[h]]) for h in hs]
        return tuple((state[h][0] + rest[h][:, :1] + log_1mb[h][:, :1], state[h][1] + pv[h]) for h in hs)

    def worth(state):
        top = state[0][0]
        for h in range(1, heads):
            top = jnp.maximum(top, state[h][0])
        return (jnp.max(top) > -SKIP_LOG).astype(jnp.int32)

    init = tuple((jnp.zeros((tb, 1), F32), jnp.zeros((tb, HEAD_DIM), F32)) for _ in range(heads))
    first = block(qi, init, True)

    def cond(c):
        return (c[0] >= 0) & (c[1] > 0)

    def body(c):
        j, _, state = c
        state = block(j, state, False)
        return j - 1, worth(state), state

    _, _, final = lax.while_loop(cond, body, (qi - 1, worth(first), first))
    for h in range(heads):
        o_ref[:, lanes[h]] = final[h][1].astype(o_ref.dtype)


def _stick_breaking(u):
    s = u.shape[0]
    tb = min(SB_BLOCK, s)
    col0 = COL_SB // SB_W
    resident = lambda blk: pl.BlockSpec((s, SB_W), lambda i: (0, blk), pipeline_mode=pl.Buffered(1))
    return pl.pallas_call(
        functools.partial(_sb_kernel, scale=HEAD_DIM ** -0.5),
        out_shape=jax.ShapeDtypeStruct((s, SB_W), BF16),
        grid=(s // tb,),
        in_specs=[pl.BlockSpec((tb, SB_W), lambda i: (i, col0)), resident(col0 + 1), resident(col0 + 2)],
        out_specs=pl.BlockSpec((tb, SB_W), lambda i: (i, 0)),
        compiler_params=_params("parallel"),
        name="sb_attn",
    )(u, u, u)


def _merge_kernel(x_ref, og_ref, of_ref, os_ref, wg0_ref, wg1_ref, wg2_ref, wbg_ref, wbf_ref, wbs_ref,
                  wo_ref, g_ref, b_ref, o_ref, xb_ref, *, alpha):
    j = pl.program_id(1)

    @pl.when(j == 0)
    def _():
        x = x_ref[...]
        xb_ref[...] = x.astype(BF16)
        o_ref[...] = alpha * x

    xb = xb_ref[...]
    m = _sigmoid(_dot(xb, wg0_ref[...])) * _dot(og_ref[...], wbg_ref[...])
    m += _sigmoid(_dot(xb, wg1_ref[...])) * _dot(of_ref[...], wbf_ref[...])
    m += _sigmoid(_dot(xb, wg2_ref[...])) * _dot(os_ref[...], wbs_ref[...])
    o_ref[...] += _dot(m.astype(BF16), wo_ref[...])

    @pl.when(j == pl.num_programs(1) - 1)
    def _():
        o_ref[...] = _layernorm(o_ref[...], g_ref[...], b_ref[...])


def _merge_ln(x, o_gla, o_fox, o_sb, w_all, w_bg, w_bf, w_bs, w_out, g, b, layer, alpha):
    s, d = x.shape
    tm, tn = min(MERGE_TM, s), min(MERGE_TN, d)
    nj = d // tn
    gate0 = MAIN_W // tn
    row_spec = lambda w: pl.BlockSpec((tm, w), lambda i, j: (i, 0), pipeline_mode=pl.Buffered(1))
    gate_spec = lambda br: pl.BlockSpec((None, d, tn), lambda i, j: (layer, 0, gate0 + br * nj + j))
    col_spec = lambda w: pl.BlockSpec((None, w, tn), lambda i, j: (layer, 0, j))
    vec_spec = pl.BlockSpec((None, 1, d), lambda i, j: (layer, 0, 0))
    return pl.pallas_call(
        functools.partial(_merge_kernel, alpha=alpha),
        out_shape=jax.ShapeDtypeStruct((s, d), F32),
        grid=(s // tm, nj),
        in_specs=[
            row_spec(d), row_spec(GLA_V), row_spec(FOX_W), row_spec(SB_W),
            gate_spec(0), gate_spec(1), gate_spec(2),
            col_spec(GLA_V), col_spec(FOX_W), col_spec(SB_W),
            pl.BlockSpec((None, tn, d), lambda i, j: (layer, j, 0)),
            vec_spec, vec_spec,
        ],
        out_specs=pl.BlockSpec((tm, d), lambda i, j: (i, 0)),
        scratch_shapes=[pltpu.VMEM((tm, d), BF16)],
        compiler_params=_params("parallel", "arbitrary"),
        name="merge_ln",
    )(x, o_gla, o_fox, o_sb, w_all, w_all, w_all, w_bg, w_bf, w_bs, w_out, g[:, None, :], b[:, None, :])


def _mixer_branches(x, w_all, w_tail, layer, gla_w_a2, gla_b_a, gla_norm_g, fox_b_f):
    s, d = x.shape
    u, tail = _proj(x, w_all, w_tail, layer)

    wa_pad = jnp.zeros((LANES, GLA_QK), F32).at[:GLA_RANK].set(gla_w_a2)
    o_gla = _gla(u, tail, wa_pad, gla_b_a, gla_norm_g)

    bf_pad = jnp.zeros((1, LANES), F32).at[0, FF_LANE:FF_LANE + FOX_HEADS].set(fox_b_f)
    c = _fox_cumsum(tail, bf_pad)
    ct = c[:, FF_LANE:FF_LANE + FOX_HEADS].T
    tb = min(FOX_BLOCK, s)
    o_fox = _fox(u, c, ct.reshape(FOX_HEADS, s // tb, 1, tb))
    o_sb = _stick_breaking(u)
    return o_gla, o_fox, o_sb


def _mixer_ln(x, w_all, w_tail, layer, gla_w_a2, gla_b_a, gla_norm_g, fox_b_f, w_bg, w_bf, w_bs, w_out,
              ln_g, ln_b, alpha):
    o_gla, o_fox, o_sb = _mixer_branches(x, w_all, w_tail, layer, gla_w_a2, gla_b_a, gla_norm_g, fox_b_f)
    return _merge_ln(x, o_gla, o_fox, o_sb, w_all, w_bg, w_bf, w_bs, w_out, ln_g, ln_b, layer, alpha)


def kernel(x, ffn1_w_gate, ffn1_w_up, ffn1_w_down, ln1_g, ln1_b, w_in, gla_w_a2, gla_b_a, gla_norm_g, fox_b_f, w_br_gla, w_br_fox, w_br_sb, w_out, ln2_g, ln2_b, ffn2_w_gate, ffn2_w_up, ffn2_w_down, ln3_g, ln3_b):
    batch, seq, d = x.shape
    depth = w_in.shape[0]
    alpha = (2 * depth) ** 0.25
    cast = lambda w: w.astype(BF16)
    ffn1 = (ffn1_w_gate, ffn1_w_up, ffn1_w_down)
    ffn2 = (ffn2_w_gate, ffn2_w_up, ffn2_w_down)
    branch = (cast(w_br_gla), cast(w_br_fox), cast(w_br_sb), cast(w_out))
    assert (N_BRANCH * d) % REGROUP_W == 0 and w_in.shape[2] == MAIN_W + GLA_RANK + FOX_HEADS + N_BRANCH * d
    w_all, w_tail = _regroup(w_in), _tail_weights(w_in)
    outs = []
    for bi in range(batch):
        h = x[bi]
        for l in range(depth):
            h = _ffn_ln(h, *ffn1, ln1_g, ln1_b, l, alpha)
            h = _mixer_ln(h, w_all, w_tail, l, gla_w_a2[l], gla_b_a[l], gla_norm_g[l], fox_b_f[l], *branch,
                          ln2_g, ln2_b, alpha)
            h = _ffn_ln(h, *ffn2, ln3_g, ln3_b, l, alpha)
        outs.append(h)
    return jnp.stack(outs, axis=0)
```

```python
import functools

import jax
import jax.numpy as jnp
from jax import lax
from jax.experimental import pallas as pl
from jax.experimental.pallas import tpu as pltpu

F32 = jnp.float32
BF16 = jnp.bfloat16

LN_EPS = 1e-5
GLA_HEADS = 4
GLA_DK = 64
GLA_DV = 128
GLA_RANK = 16
GLA_TAU = 16.0
GLA_CHUNK = 64
GLA_SUB = 16
GLA_QK = GLA_HEADS * GLA_DK
GLA_V = GLA_HEADS * GLA_DV
FOX_HEADS = 6
SB_HEADS = 6
HEAD_DIM = 128
FOX_W = FOX_HEADS * HEAD_DIM
SB_W = SB_HEADS * HEAD_DIM
N_BRANCH = 3
LANES = 128

REGROUP_W = 256
MAIN_W = 2 * GLA_QK + 2 * GLA_V + 3 * FOX_W + 3 * SB_W
COL_GV = 2 * GLA_QK
COL_GR = COL_GV + GLA_V
COL_FOX = COL_GR + GLA_V
COL_SB = COL_FOX + 3 * FOX_W
FF_LANE = GLA_RANK

FFN_TM = 1024
FFN_TF = 256
PROJ_TM = 2048
PROJ_TN = 512
MERGE_TM = 1024
MERGE_TN = 256
GLA_ROWS = 256
CUM_ROWS = 256
FOX_BLOCK = 256
SB_BLOCK = 128
VMEM_LIMIT = 52 * 1024 * 1024
MASK_VALUE = -1e30
SKIP_LOG = 30.0


def _dot(a, b):
    return jnp.dot(a, b, preferred_element_type=F32)


def _dot_nt(a, b):
    return lax.dot_general(a, b, (((1,), (1,)), ((), ())), preferred_element_type=F32)


def _dot_tn(a, b):
    return lax.dot_general(a, b, (((0,), (0,)), ((), ())), preferred_element_type=F32)


def _split2(x):
    hi = x.astype(BF16)
    lo = (x - hi.astype(F32)).astype(BF16)
    return hi, lo


def _split3(x):
    hi = x.astype(BF16)
    r = x - hi.astype(F32)
    mid = r.astype(BF16)
    lo = (r - mid.astype(F32)).astype(BF16)
    return hi, mid, lo


def _softplus(x):
    return jnp.maximum(x, 0.0) + jnp.log(1.0 + jnp.exp(-jnp.abs(x)))


def _sigmoid(x):
    return 1.0 / (1.0 + jnp.exp(-x))


def _layernorm(y, g, b):
    mu = jnp.mean(y, axis=-1, keepdims=True)
    d = y - mu
    var = jnp.mean(d * d, axis=-1, keepdims=True)
    return d * lax.rsqrt(var + LN_EPS) * g + b


def _params(*sem):
    return pltpu.CompilerParams(dimension_semantics=sem, vmem_limit_bytes=VMEM_LIMIT)


def _ffn_kernel(x_ref, wg_ref, wu_ref, wd_ref, g_ref, b_ref, o_ref, xb_ref, *, alpha):
    f = pl.program_id(1)

    @pl.when(f == 0)
    def _():
        x = x_ref[...]
        xb_ref[...] = x.astype(BF16)
        o_ref[...] = (2.0 * alpha) * x

    xb = xb_ref[...]
    gate = _dot(xb, wg_ref[...].astype(BF16))
    up = _dot(xb, wu_ref[...].astype(BF16))
    h = gate * _sigmoid(gate) * up
    o_ref[...] += _dot(h.astype(BF16), wd_ref[...].astype(BF16))

    @pl.when(f == pl.num_programs(1) - 1)
    def _():
        o_ref[...] = _layernorm(0.5 * o_ref[...], g_ref[...], b_ref[...])


def _ffn_ln(x, wg, wu, wd, g, b, layer, alpha):
    s, d = x.shape
    f = wg.shape[2]
    tm, tf = min(FFN_TM, s), min(FFN_TF, f)
    vec_spec = pl.BlockSpec((None, 1, d), lambda i, j: (layer, 0, 0))
    return pl.pallas_call(
        functools.partial(_ffn_kernel, alpha=alpha),
        out_shape=jax.ShapeDtypeStruct((s, d), F32),
        grid=(s // tm, f // tf),
        in_specs=[
            pl.BlockSpec((tm, d), lambda i, j: (i, 0), pipeline_mode=pl.Buffered(1)),
            pl.BlockSpec((None, d, tf), lambda i, j: (layer, 0, j)),
            pl.BlockSpec((None, d, tf), lambda i, j: (layer, 0, j)),
            pl.BlockSpec((None, tf, d), lambda i, j: (layer, j, 0)),
            vec_spec, vec_spec,
        ],
        out_specs=pl.BlockSpec((tm, d), lambda i, j: (i, 0)),
        scratch_shapes=[pltpu.VMEM((tm, d), BF16)],
        compiler_params=_params("parallel", "arbitrary"),
        name="ffn_ln",
    )(x, wg, wu, wd, g[:, None, :], b[:, None, :])


def _column_rows(w_in):
    n_layers, d, cols = w_in.shape
    n_slices = d // LANES
    rows = jnp.transpose(w_in.reshape(n_layers, n_slices, LANES, cols), (3, 1, 0, 2))
    return rows.reshape(cols * n_slices * n_layers, LANES), n_slices


def _read_piece(ref, slice_idx, layer, n_cols, n_slices, n_layers):
    stride = n_slices * n_layers
    return ref[pl.ds(slice_idx * n_layers + layer, n_cols, stride=stride), :].T


def _regroup_kernel(w_ref, o_ref, *, n_slices):
    n_layers, _, tn = o_ref.shape
    for layer in range(n_layers):
        for t in range(n_slices):
            piece = _read_piece(w_ref, t, layer, tn, n_slices, n_layers)
            o_ref[layer, t * LANES:(t + 1) * LANES, :] = piece.astype(o_ref.dtype)


def _regroup(w_in):
    n_layers, d, cols = w_in.shape
    n_out = MAIN_W + N_BRANCH * d
    tn = REGROUP_W
    w_rows, n_slices = _column_rows(w_in)
    per_col = n_slices * n_layers
    n_plain, n_fox = COL_FOX // tn, 3 * FOX_W // tn

    def source(j):
        skipped = jnp.where(j < n_plain, 0, jnp.where(j < n_plain + n_fox, GLA_RANK, GLA_RANK + FOX_HEADS))
        return (j * tn + skipped) * per_col, 0

    return pl.pallas_call(
        functools.partial(_regroup_kernel, n_slices=n_slices),
        out_shape=jax.ShapeDtypeStruct((n_layers, d, n_out), BF16),
        grid=(n_out // tn,),
        in_specs=[pl.BlockSpec((pl.Element(tn * per_col), pl.Element(LANES)), source)],
        out_specs=pl.BlockSpec((n_layers, d, tn), lambda j: (0, 0, j)),
        compiler_params=_params("parallel"),
        name="regroup_w_in",
    )(w_rows)


def _tail_kernel(a_ref, b_ref, o_ref, *, n_slices):
    n_layers = o_ref.shape[0]
    lane = lax.broadcasted_iota(jnp.int32, (LANES, LANES), 1)
    for layer in range(n_layers):
        for t in range(n_slices):
            a = _read_piece(a_ref, t, layer, LANES, n_slices, n_layers)
            b = _read_piece(b_ref, t, layer, LANES, n_slices, n_layers)
            tail = jnp.where(lane < GLA_RANK, a, jnp.where(lane < GLA_RANK + FOX_HEADS, b, 0.0))
            o_ref[layer, t * LANES:(t + 1) * LANES, :] = tail.astype(o_ref.dtype)


def _tail_weights(w_in):
    n_layers, d, _ = w_in.shape
    w_rows, n_slices = _column_rows(w_in)
    blk_rows = LANES * n_slices * n_layers
    ga_blk = COL_FOX // LANES
    ff_blk = (COL_FOX + GLA_RANK + 3 * FOX_W) // LANES
    return pl.pallas_call(
        functools.partial(_tail_kernel, n_slices=n_slices),
        out_shape=jax.ShapeDtypeStruct((n_layers, d, LANES), BF16),
        grid=(1,),
        in_specs=[
            pl.BlockSpec((blk_rows, LANES), lambda i: (ga_blk, 0)),
            pl.BlockSpec((blk_rows, LANES), lambda i: (ff_blk, 0)),
        ],
        out_specs=pl.BlockSpec((n_layers, d, LANES), lambda i: (0, 0, 0)),
        compiler_params=_params("arbitrary"),
        name="tail_w_in",
    )(w_rows, w_rows)


def _proj_kernel(x_ref, w_ref, wt_ref, o_ref, t_ref, xb_ref):
    @pl.when(pl.program_id(1) == 0)
    def _():
        xb_ref[...] = x_ref[...].astype(BF16)
        t_ref[...] = _dot(xb_ref[...], wt_ref[...])

    o_ref[...] = _dot(xb_ref[...], w_ref[...]).astype(o_ref.dtype)


def _proj(x, w_all, w_tail, layer):
    s, d = x.shape
    tm, tn = min(PROJ_TM, s), PROJ_TN
    return pl.pallas_call(
        _proj_kernel,
        out_shape=(jax.ShapeDtypeStruct((s, MAIN_W), BF16), jax.ShapeDtypeStruct((s, LANES), F32)),
        grid=(s // tm, MAIN_W // tn),
        in_specs=[
            pl.BlockSpec((tm, d), lambda i, j: (i, 0), pipeline_mode=pl.Buffered(1)),
            pl.BlockSpec((None, d, tn), lambda i, j: (layer, 0, j)),
            pl.BlockSpec((None, d, LANES), lambda i, j: (layer, 0, 0)),
        ],
        out_specs=(pl.BlockSpec((tm, tn), lambda i, j: (i, j)),
                   pl.BlockSpec((tm, LANES), lambda i, j: (i, 0))),
        scratch_shapes=[pltpu.VMEM((tm, d), BF16)],
        compiler_params=_params("parallel", "arbitrary"),
        name="in_proj",
    )(x, w_all, w_tail)


def _gla_kernel(q_ref, k_ref, v_ref, gr_ref, ga_ref, wa_ref, ba_ref, ng_ref, o_ref, state_ref):
    rows = q_ref.shape[0]
    n_chunks = rows // GLA_CHUNK
    n_sub = GLA_CHUNK // GLA_SUB

    @pl.when(pl.program_id(0) == 0)
    def _():
        state_ref[...] = jnp.zeros_like(state_ref)

    ga_hi, ga_lo = _split2(ga_ref[...])
    wa_hi, wa_lo = _split2(wa_ref[...])
    xa = _dot(ga_hi, wa_hi) + _dot(ga_hi, wa_lo) + _dot(ga_lo, wa_hi) + ba_ref[...]
    la = -_softplus(-xa) * (1.0 / GLA_TAU)

    r_i = lax.broadcasted_iota(jnp.int32, (rows, rows), 0)
    c_i = lax.broadcasted_iota(jnp.int32, (rows, rows), 1)
    chunk_shift = GLA_CHUNK.bit_length() - 1
    same_chunk = (c_i >> chunk_shift) == (r_i >> chunk_shift)
    tri = jnp.where((c_i <= r_i) & same_chunk, 1.0, 0.0).astype(BF16)
    la_hi, la_mid, la_lo = _split3(la)
    b_all = _dot(tri, la_hi) + _dot(tri, la_mid) + _dot(tri, la_lo)

    e_r = lax.broadcasted_iota(jnp.int32, (GLA_QK, GLA_V), 0)
    e_c = lax.broadcasted_iota(jnp.int32, (GLA_QK, GLA_V), 1)
    same_head = (e_r >> GLA_DK.bit_length() - 1) == (e_c >> GLA_DV.bit_length() - 1)
    expand = jnp.where(same_head, 1.0, 0.0).astype(BF16)
    lane_head = lax.broadcasted_iota(jnp.int32, (1, GLA_QK), 1) >> GLA_DK.bit_length() - 1
    head_mask = [jnp.where(lane_head == h, 1.0, 0.0) for h in range(GLA_HEADS)]
    ones_col = jnp.ones((GLA_CHUNK, GLA_DV), BF16)
    t_sub = lax.broadcasted_iota(jnp.int32, (GLA_SUB, GLA_QK), 0)
    s_chunk = lax.broadcasted_iota(jnp.int32, (GLA_CHUNK, GLA_QK), 0)

    def head_rows(x, n):
        return jnp.concatenate(
            [x[h * n:(h + 1) * n, h * GLA_DV:(h + 1) * GLA_DV] for h in range(GLA_HEADS)], axis=1)

    def stack_heads(x):
        return jnp.concatenate([x * head_mask[h] for h in range(GLA_HEADS)], axis=0).astype(BF16)

    for c in range(n_chunks):
        sl = slice(c * GLA_CHUNK, (c + 1) * GLA_CHUNK)
        q_c = q_ref[sl, :].astype(F32) * (GLA_DK ** -0.5)
        v_cb = v_ref[sl, :]
        k_c = k_ref[sl, :].astype(F32)
        v_c = v_cb.astype(F32)
        b_c = b_all[sl, :]
        b_last = b_c[GLA_CHUNK - 1:GLA_CHUNK, :]

        state = state_ref[...]
        inter = _dot(stack_heads(q_c * jnp.exp(b_c)), state.astype(BF16))
        o_inter = jnp.concatenate(
            [inter[h * GLA_CHUNK:(h + 1) * GLA_CHUNK, :] for h in range(GLA_HEADS)], axis=1)

        kv = _dot_tn((k_c * jnp.exp(b_last - b_c)).astype(BF16), v_cb)
        kv = jnp.concatenate(
            [kv[h * GLA_DK:(h + 1) * GLA_DK, h * GLA_DV:(h + 1) * GLA_DV] for h in range(GLA_HEADS)], axis=0)
        tot = _dot_tn(la_hi[sl, :], ones_col) + _dot_tn(la_mid[sl, :], ones_col)
        state_ref[...] = state * jnp.exp(tot) + kv

        o_rows = []
        for i in range(n_sub):
            rs = slice(i * GLA_SUB, (i + 1) * GLA_SUB)
            q_i, k_i, b_i, v_i = q_c[rs, :], k_c[rs, :], b_c[rs, :], v_c[rs, :]
            ps = []
            for s in range(GLA_SUB):
                diff = jnp.where(t_sub >= s, b_i - b_i[s:s + 1, :], -jnp.inf)
                ps.append(q_i * k_i[s:s + 1, :] * jnp.exp(diff))
            att = _dot(jnp.concatenate(ps, axis=0).astype(BF16), expand)
            o_i = att[0:GLA_SUB, :] * v_i[0:1, :]
            for s in range(1, GLA_SUB):
                o_i += att[s * GLA_SUB:(s + 1) * GLA_SUB, :] * v_i[s:s + 1, :]
            if i > 0:
                ref_row = b_c[i * GLA_SUB - 1:i * GLA_SUB, :]
                q_dec = stack_heads(q_i * jnp.exp(b_i - ref_row))
                k_dec = k_c * jnp.exp(jnp.where(s_chunk < i * GLA_SUB, ref_row - b_c, -jnp.inf))
                att_prev = _dot_nt(q_dec, k_dec.astype(BF16))
                o_i += head_rows(_dot(att_prev.astype(BF16), v_cb), GLA_SUB)
            o_rows.append(o_i)
        o_c = jnp.concatenate(o_rows, axis=0) + o_inter

        outs = []
        for h in range(GLA_HEADS):
            o_h = o_c[:, h * GLA_DV:(h + 1) * GLA_DV]
            ms = jnp.mean(o_h * o_h, axis=-1, keepdims=True)
            outs.append(o_h * lax.rsqrt(ms + LN_EPS) * ng_ref[...])
        gr = gr_ref[sl, :].astype(F32)
        o_ref[sl, :] = (jnp.concatenate(outs, axis=-1) * (gr * _sigmoid(gr))).astype(o_ref.dtype)


def _gla(u, tail, wa_pad, b_a, norm_g):
    s = u.shape[0]
    rows = min(GLA_ROWS, s)
    return pl.pallas_call(
        _gla_kernel,
        out_shape=jax.ShapeDtypeStruct((s, GLA_V), BF16),
        grid=(s // rows,),
        in_specs=[
            pl.BlockSpec((rows, GLA_QK), lambda i: (i, 0)),
            pl.BlockSpec((rows, GLA_QK), lambda i: (i, 1)),
            pl.BlockSpec((rows, GLA_V), lambda i: (i, COL_GV // GLA_V)),
            pl.BlockSpec((rows, GLA_V), lambda i: (i, COL_GR // GLA_V)),
            pl.BlockSpec((rows, LANES), lambda i: (i, 0)),
            pl.BlockSpec((LANES, GLA_QK), lambda i: (0, 0)),
            pl.BlockSpec((1, GLA_QK), lambda i: (0, 0)),
            pl.BlockSpec((1, GLA_DV), lambda i: (0, 0)),
        ],
        out_specs=pl.BlockSpec((rows, GLA_V), lambda i: (i, 0)),
        scratch_shapes=[pltpu.VMEM((GLA_QK, GLA_DV), F32)],
        compiler_params=_params("arbitrary"),
        name="gla",
    )(u, u, u, u, tail, wa_pad, b_a.reshape(1, GLA_QK), norm_g.reshape(1, GLA_DV))


def _cumsum_kernel(ff_ref, bf_ref, c_ref, carry_ref):
    rows = ff_ref.shape[0]

    @pl.when(pl.program_id(0) == 0)
    def _():
        carry_ref[...] = jnp.zeros_like(carry_ref)

    log_f = -_softplus(-(ff_ref[...] + bf_ref[...]))
    r_i = lax.broadcasted_iota(jnp.int32, (rows, rows), 0)
    c_i = lax.broadcasted_iota(jnp.int32, (rows, rows), 1)
    tri = jnp.where(c_i <= r_i, 1.0, 0.0).astype(BF16)
    hi, mid, lo = _split3(log_f)
    c = _dot(tri, hi) + _dot(tri, mid) + _dot(tri, lo) + carry_ref[...]
    c_ref[...] = c
    carry_ref[...] = c[rows - 1:rows, :]


def _fox_cumsum(tail, bf_pad):
    s = tail.shape[0]
    rows = min(CUM_ROWS, s)
    return pl.pallas_call(
        _cumsum_kernel,
        out_shape=jax.ShapeDtypeStruct((s, LANES), F32),
        grid=(s // rows,),
        in_specs=[
            pl.BlockSpec((rows, LANES), lambda i: (i, 0)),
            pl.BlockSpec((1, LANES), lambda i: (0, 0)),
        ],
        out_specs=pl.BlockSpec((rows, LANES), lambda i: (i, 0)),
        scratch_shapes=[pltpu.VMEM((1, LANES), F32)],
        compiler_params=_params("arbitrary"),
        name="fox_cumsum",
    )(tail, bf_pad)


def _fox_kernel(q_ref, k_ref, v_ref, c_ref, ck_ref, o_ref, kmax_ref, *, scale):
    tb = q_ref.shape[0]
    heads = q_ref.shape[1] // HEAD_DIM
    qi = pl.program_id(0)
    lanes = [slice(h * HEAD_DIM, (h + 1) * HEAD_DIM) for h in range(heads)]

    @pl.when(qi == 0)
    def _():
        for h in range(heads):
            kf = k_ref[:, lanes[h]].astype(F32)
            kmax_ref[h] = jnp.sqrt(jnp.max(jnp.sum(kf * kf, axis=-1, keepdims=True), axis=0, keepdims=True))

    qs = [q_ref[:, lanes[h]] for h in range(heads)]
    cqs = [c_ref[:, FF_LANE + h:FF_LANE + h + 1] for h in range(heads)]
    bounds = []
    for h in range(heads):
        qf = qs[h].astype(F32)
        bounds.append(jnp.sqrt(jnp.sum(qf * qf, axis=-1, keepdims=True)) * kmax_ref[h] * scale + cqs[h])
    row = lax.broadcasted_iota(jnp.int32, (tb, tb), 0)
    col = lax.broadcasted_iota(jnp.int32, (tb, tb), 1)

    def block(j, state, masked):
        start = pl.multiple_of(j * tb, tb)
        hs = range(heads)
        ss = [_dot_nt(qs[h], k_ref[pl.ds(start, tb), lanes[h]]) for h in hs]
        ss = [ss[h] * scale + cqs[h] - ck_ref[h, j] for h in hs]
        if masked:
            ss = [jnp.where(col <= row, s, MASK_VALUE) for s in ss]
        m_new = [jnp.maximum(state[h][0], jnp.max(ss[h], axis=-1, keepdims=True)) for h in hs]
        ps = [jnp.exp(ss[h] - m_new[h]) for h in hs]
        corr = [jnp.exp(state[h][0] - m_new[h]) for h in hs]
        ls = [corr[h] * state[h][1] + jnp.sum(ps[h], axis=-1, keepdims=True) for h in hs]
        pv = [_dot(ps[h].astype(BF16), v_ref[pl.ds(start, tb), lanes[h]]) for h in hs]
        return tuple((m_new[h], ls[h], corr[h] * state[h][2] + pv[h]) for h in hs)

    def worth(j, state):
        jc = jnp.maximum(j, 0)
        gap = bounds[0] - ck_ref[0, jc][:, tb - 1:tb] - state[0][0]
        for h in range(1, heads):
            gap = jnp.maximum(gap, bounds[h] - ck_ref[h, jc][:, tb - 1:tb] - state[h][0])
        return (jnp.max(gap) > -SKIP_LOG).astype(jnp.int32)

    init = tuple((jnp.full((tb, 1), MASK_VALUE, F32), jnp.zeros((tb, 1), F32), jnp.zeros((tb, HEAD_DIM), F32))
                 for _ in range(heads))
    first = block(qi, init, True)

    def cond(c):
        return (c[0] >= 0) & (c[1] > 0)

    def body(c):
        j, _, state = c
        state = block(j, state, False)
        return j - 1, worth(j - 1, state), state

    _, _, final = lax.while_loop(cond, body, (qi - 1, worth(qi - 1, first), first))
    for h in range(heads):
        _, l, acc = final[h]
        o_ref[:, lanes[h]] = (acc / l).astype(o_ref.dtype)


def _fox(u, c, ck):
    s = u.shape[0]
    tb = min(FOX_BLOCK, s)
    nb = s // tb
    col0 = COL_FOX // FOX_W
    resident = lambda blk: pl.BlockSpec((s, FOX_W), lambda i: (0, blk), pipeline_mode=pl.Buffered(1))
    return pl.pallas_call(
        functools.partial(_fox_kernel, scale=HEAD_DIM ** -0.5),
        out_shape=jax.ShapeDtypeStruct((s, FOX_W), BF16),
        grid=(nb,),
        in_specs=[
            pl.BlockSpec((tb, FOX_W), lambda i: (i, col0)),
            resident(col0 + 1),
            resident(col0 + 2),
            pl.BlockSpec((tb, LANES), lambda i: (i, 0)),
            pl.BlockSpec((FOX_HEADS, nb, 1, tb), lambda i: (0, 0, 0, 0)),
        ],
        out_specs=pl.BlockSpec((tb, FOX_W), lambda i: (i, 0)),
        scratch_shapes=[pltpu.VMEM((FOX_HEADS, 1, 1), F32)],
        compiler_params=_params("arbitrary"),
        name="fox_attn",
    )(u, u, u, c, ck)


def _sb_kernel(q_ref, k_ref, v_ref, o_ref, *, scale):
    tb = q_ref.shape[0]
    heads = q_ref.shape[1] // HEAD_DIM
    qi = pl.program_id(0)
    lanes = [slice(h * HEAD_DIM, (h + 1) * HEAD_DIM) for h in range(heads)]
    qs = [q_ref[:, lanes[h]] for h in range(heads)]
    row = lax.broadcasted_iota(jnp.int32, (tb, tb), 0)
    col = lax.broadcasted_iota(jnp.int32, (tb, tb), 1)
    upper = jnp.where(row > col, 1.0, 0.0).astype(BF16)

    def block(j, state, masked):
        start = pl.multiple_of(j * tb, tb)
        hs = range(heads)
        zs = [_dot_nt(qs[h], k_ref[pl.ds(start, tb), lanes[h]]) * scale for h in hs]
        sps = [_softplus(z) for z in zs]
        if masked:
            valid = col < row
            log_1mb = [jnp.where(valid, -sp, 0.0) for sp in sps]
        else:
            log_1mb = [-sp for sp in sps]
        parts = [_split2(x) for x in log_1mb]
        rest = [_dot(hi, upper) + _dot(lo, upper) for hi, lo in parts]
        a = [jnp.exp(zs[h] - sps[h] + rest[h] + state[h][0]) for h in hs]
        if masked:
            a = [jnp.where(valid, x, 0.0) for x in a]
        pv = [_dot(a[h].astype(BF16), v_ref[pl.ds(start, tb), lanes[h]]) for h in hs]
        return tuple((state[h][0] + rest[h][:, :1] + log_1mb[h][:, :1], state[h][1] + pv[h]) for h in hs)

    def worth(state):
        top = state[0][0]
        for h in range(1, heads):
            top = jnp.maximum(top, state[h][0])
        return (jnp.max(top) > -SKIP_LOG).astype(jnp.int32)

    init = tuple((jnp.zeros((tb, 1), F32), jnp.zeros((tb, HEAD_DIM), F32)) for _ in range(heads))
    first = block(qi, init, True)

    def cond(c):
        return (c[0] >= 0) & (c[1] > 0)

    def body(c):
        j, _, state = c
        state = block(j, state, False)
        return j - 1, worth(state), state

    _, _, final = lax.while_loop(cond, body, (qi - 1, worth(first), first))
    for h in range(heads):
        o_ref[:, lanes[h]] = final[h][1].astype(o_ref.dtype)


def _stick_breaking(u):
    s = u.shape[0]
    tb = min(SB_BLOCK, s)
    col0 = COL_SB // SB_W
    resident = lambda blk: pl.BlockSpec((s, SB_W), lambda i: (0, blk), pipeline_mode=pl.Buffered(1))
    return pl.pallas_call(
        functools.partial(_sb_kernel, scale=HEAD_DIM ** -0.5),
        out_shape=jax.ShapeDtypeStruct((s, SB_W), BF16),
        grid=(s // tb,),
        in_specs=[pl.BlockSpec((tb, SB_W), lambda i: (i, col0)), resident(col0 + 1), resident(col0 + 2)],
        out_specs=pl.BlockSpec((tb, SB_W), lambda i: (i, 0)),
        compiler_params=_params("parallel"),
        name="sb_attn",
    )(u, u, u)


def _merge_kernel(x_ref, og_ref, of_ref, os_ref, wg0_ref, wg1_ref, wg2_ref, wbg_ref, wbf_ref, wbs_ref,
                  wo_ref, g_ref, b_ref, o_ref, xb_ref, x_sem, *, alpha):
    j = pl.program_id(1)
    tm = o_ref.shape[0]

    @pl.when(j == 0)
    def _():
        rows = pl.ds(pl.multiple_of(pl.program_id(0) * tm, tm), tm)
        copy = pltpu.make_async_copy(x_ref.at[rows, :], o_ref, x_sem)
        copy.start()
        copy.wait()
        x = o_ref[...]
        xb_ref[...] = x.astype(BF16)
        o_ref[...] = alpha * x

    xb = xb_ref[...]
    m = _sigmoid(_dot(xb, wg0_ref[...])) * _dot(og_ref[...], wbg_ref[...])
    m += _sigmoid(_dot(xb, wg1_ref[...])) * _dot(of_ref[...], wbf_ref[...])
    m += _sigmoid(_dot(xb, wg2_ref[...])) * _dot(os_ref[...], wbs_ref[...])
    o_ref[...] += _dot(m.astype(BF16), wo_ref[...])

    @pl.when(j == pl.num_programs(1) - 1)
    def _():
        o_ref[...] = _layernorm(o_ref[...], g_ref[...], b_ref[...])


def _merge_ln(x, o_gla, o_fox, o_sb, w_all, w_bg, w_bf, w_bs, w_out, g, b, layer, alpha):
    s, d = x.shape
    tm, tn = min(MERGE_TM, s), min(MERGE_TN, d)
    nj = d // tn
    gate0 = MAIN_W // tn
    row_spec = lambda w: pl.BlockSpec((tm, w), lambda i, j: (i, 0), pipeline_mode=pl.Buffered(1))
    gate_spec = lambda br: pl.BlockSpec((None, d, tn), lambda i, j: (layer, 0, gate0 + br * nj + j))
    col_spec = lambda w: pl.BlockSpec((None, w, tn), lambda i, j: (layer, 0, j))
    vec_spec = pl.BlockSpec((None, 1, d), lambda i, j: (layer, 0, 0))
    return pl.pallas_call(
        functools.partial(_merge_kernel, alpha=alpha),
        out_shape=jax.ShapeDtypeStruct((s, d), F32),
        grid=(s // tm, nj),
        in_specs=[
            pl.BlockSpec(memory_space=pl.ANY), row_spec(GLA_V), row_spec(FOX_W), row_spec(SB_W),
            gate_spec(0), gate_spec(1), gate_spec(2),
            col_spec(GLA_V), col_spec(FOX_W), col_spec(SB_W),
            pl.BlockSpec((None, tn, d), lambda i, j: (layer, j, 0)),
            vec_spec, vec_spec,
        ],
        out_specs=pl.BlockSpec((tm, d), lambda i, j: (i, 0)),
        scratch_shapes=[pltpu.VMEM((tm, d), BF16), pltpu.SemaphoreType.DMA],
        compiler_params=_params("parallel", "arbitrary"),
        name="merge_ln",
    )(x, o_gla, o_fox, o_sb, w_all, w_all, w_all, w_bg, w_bf, w_bs, w_out, g[:, None, :], b[:, None, :])


def _mixer_branches(x, w_all, w_tail, layer, gla_w_a2, gla_b_a, gla_norm_g, fox_b_f):
    s, d = x.shape
    u, tail = _proj(x, w_all, w_tail, layer)

    wa_pad = jnp.zeros((LANES, GLA_QK), F32).at[:GLA_RANK].set(gla_w_a2)
    o_gla = _gla(u, tail, wa_pad, gla_b_a, gla_norm_g)

    bf_pad = jnp.zeros((1, LANES), F32).at[0, FF_LANE:FF_LANE + FOX_HEADS].set(fox_b_f)
    c = _fox_cumsum(tail, bf_pad)
    ct = c[:, FF_LANE:FF_LANE + FOX_HEADS].T
    tb = min(FOX_BLOCK, s)
    o_fox = _fox(u, c, ct.reshape(FOX_HEADS, s // tb, 1, tb))
    o_sb = _stick_breaking(u)
    return o_gla, o_fox, o_sb


def _mixer_ln(x, w_all, w_tail, layer, gla_w_a2, gla_b_a, gla_norm_g, fox_b_f, w_bg, w_bf, w_bs, w_out,
              ln_g, ln_b, alpha):
    o_gla, o_fox, o_sb = _mixer_branches(x, w_all, w_tail, layer, gla_w_a2, gla_b_a, gla_norm_g, fox_b_f)
    return _merge_ln(x, o_gla, o_fox, o_sb, w_all, w_bg, w_bf, w_bs, w_out, ln_g, ln_b, layer, alpha)


def kernel(x, ffn1_w_gate, ffn1_w_up, ffn1_w_down, ln1_g, ln1_b, w_in, gla_w_a2, gla_b_a, gla_norm_g, fox_b_f, w_br_gla, w_br_fox, w_br_sb, w_out, ln2_g, ln2_b, ffn2_w_gate, ffn2_w_up, ffn2_w_down, ln3_g, ln3_b):
    batch, seq, d = x.shape
    depth = w_in.shape[0]
    alpha = (2 * depth) ** 0.25
    cast = lambda w: w.astype(BF16)
    ffn1 = (ffn1_w_gate, ffn1_w_up, ffn1_w_down)
    ffn2 = (ffn2_w_gate, ffn2_w_up, ffn2_w_down)
    branch = (cast(w_br_gla), cast(w_br_fox), cast(w_br_sb), cast(w_out))
    assert (N_BRANCH * d) % REGROUP_W == 0 and w_in.shape[2] == MAIN_W + GLA_RANK + FOX_HEADS + N_BRANCH * d
    w_all, w_tail = _regroup(w_in), _tail_weights(w_in)
    outs = []
    for bi in range(batch):
        h = x[bi]
        for l in range(depth):
            h = _ffn_ln(h, *ffn1, ln1_g, ln1_b, l, alpha)
            h = _mixer_ln(h, w_all, w_tail, l, gla_w_a2[l], gla_b_a[l], gla_norm_g[l], fox_b_f[l], *branch,
                          ln2_g, ln2_b, alpha)
            h = _ffn_ln(h, *ffn2, ln3_g, ln3_b, l, alpha)
        outs.append(h)
    return jnp.stack(outs, axis=0)
```

```python
import functools

import jax
import jax.numpy as jnp
from jax import lax
from jax.experimental import pallas as pl
from jax.experimental.pallas import tpu as pltpu

F32 = jnp.float32
BF16 = jnp.bfloat16

LN_EPS = 1e-5
GLA_HEADS = 4
GLA_DK = 64
GLA_DV = 128
GLA_RANK = 16
GLA_TAU = 16.0
GLA_CHUNK = 64
GLA_SUB = 16
GLA_QK = GLA_HEADS * GLA_DK
GLA_V = GLA_HEADS * GLA_DV
FOX_HEADS = 6
SB_HEADS = 6
HEAD_DIM = 128
FOX_W = FOX_HEADS * HEAD_DIM
SB_W = SB_HEADS * HEAD_DIM
N_BRANCH = 3
LANES = 128

REGROUP_W = 256
MAIN_W = 2 * GLA_QK + 2 * GLA_V + 3 * FOX_W + 3 * SB_W
COL_GV = 2 * GLA_QK
COL_GR = COL_GV + GLA_V
COL_FOX = COL_GR + GLA_V
COL_SB = COL_FOX + 3 * FOX_W
FF_LANE = GLA_RANK

FFN_TM = 1024
FFN_TF = 256
FFN_TN = 256
PROJ_TM = 2048
PROJ_TN = 512
MERGE_TM = 512
MERGE_TN = 512
GLA_ROWS = 256
CUM_ROWS = 256
FOX_BLOCK = 256
SB_BLOCK = 128
VMEM_LIMIT = 52 * 1024 * 1024
MASK_VALUE = -1e30
SKIP_LOG = 30.0


def _dot(a, b):
    return jnp.dot(a, b, preferred_element_type=F32)


def _dot_nt(a, b):
    return lax.dot_general(a, b, (((1,), (1,)), ((), ())), preferred_element_type=F32)


def _dot_tn(a, b):
    return lax.dot_general(a, b, (((0,), (0,)), ((), ())), preferred_element_type=F32)


def _split2(x):
    hi = x.astype(BF16)
    lo = (x - hi.astype(F32)).astype(BF16)
    return hi, lo


def _split3(x):
    hi = x.astype(BF16)
    r = x - hi.astype(F32)
    mid = r.astype(BF16)
    lo = (r - mid.astype(F32)).astype(BF16)
    return hi, mid, lo


def _softplus(x):
    return jnp.maximum(x, 0.0) + jnp.log(1.0 + jnp.exp(-jnp.abs(x)))


def _sigmoid(x):
    return 1.0 / (1.0 + jnp.exp(-x))


def _layernorm(y, g, b):
    mu = jnp.mean(y, axis=-1, keepdims=True)
    d = y - mu
    var = jnp.mean(d * d, axis=-1, keepdims=True)
    return d * lax.rsqrt(var + LN_EPS) * g + b


def _params(*sem):
    return pltpu.CompilerParams(dimension_semantics=sem, vmem_limit_bytes=VMEM_LIMIT)


def _ffn_kernel(x_ref, wg_ref, wu_ref, wd_ref, g_ref, b_ref, o_ref, xb_ref, h_ref, x_sem, *, alpha):
    nf, tm, tf = h_ref.shape
    tn = wd_ref.shape[1]
    j = pl.program_id(1)

    @pl.when(j == 0)
    def _():
        rows = pl.ds(pl.multiple_of(pl.program_id(0) * tm, tm), tm)
        copy = pltpu.make_async_copy(x_ref.at[rows, :], o_ref, x_sem)
        copy.start()
        copy.wait()
        x = o_ref[...]
        xb_ref[...] = x.astype(BF16)
        o_ref[...] = (2.0 * alpha) * x

    @pl.when(j < nf)
    def _():
        xb = xb_ref[...]
        gate = _dot(xb, wg_ref[...])
        up = _dot(xb, wu_ref[...])
        h_ref[j] = (gate * _sigmoid(gate) * up).astype(BF16)

    @pl.when(j >= nf)
    def _():
        acc = _dot(h_ref[0], wd_ref[0:tf, :])
        for f in range(1, nf):
            acc += _dot(h_ref[f], wd_ref[f * tf:(f + 1) * tf, :])
        cols = pl.ds(pl.multiple_of((j - nf) * tn, tn), tn)
        o_ref[:, cols] += acc

    @pl.when(j == pl.num_programs(1) - 1)
    def _():
        o_ref[...] = _layernorm(0.5 * o_ref[...], g_ref[...], b_ref[...])


def _ffn_ln(x, wg, wu, wd, g, b, layer, alpha):
    s, d = x.shape
    f = wg.shape[2]
    tm, tf, tn = min(FFN_TM, s), min(FFN_TF, f), min(FFN_TN, d)
    nf, nn = f // tf, d // tn
    vec_spec = pl.BlockSpec((None, 1, d), lambda i, j: (layer, 0, 0))
    up_spec = pl.BlockSpec((None, d, tf), lambda i, j: (layer, 0, jnp.minimum(j, nf - 1)))
    return pl.pallas_call(
        functools.partial(_ffn_kernel, alpha=alpha),
        out_shape=jax.ShapeDtypeStruct((s, d), F32),
        grid=(s // tm, nf + nn),
        in_specs=[
            pl.BlockSpec(memory_space=pl.ANY),
            up_spec, up_spec,
            pl.BlockSpec((None, f, tn), lambda i, j: (layer, 0, jnp.maximum(j - nf, 0))),
            vec_spec, vec_spec,
        ],
        out_specs=pl.BlockSpec((tm, d), lambda i, j: (i, 0)),
        scratch_shapes=[pltpu.VMEM((tm, d), BF16), pltpu.VMEM((nf, tm, tf), BF16), pltpu.SemaphoreType.DMA],
        compiler_params=_params("parallel", "arbitrary"),
        name="ffn_ln",
    )(x, wg, wu, wd, g[:, None, :], b[:, None, :])


def _column_rows(w_in):
    n_layers, d, cols = w_in.shape
    n_slices = d // LANES
    rows = jnp.transpose(w_in.reshape(n_layers, n_slices, LANES, cols), (3, 1, 0, 2))
    return rows.reshape(cols * n_slices * n_layers, LANES), n_slices


def _read_piece(ref, slice_idx, layer, n_cols, n_slices, n_layers):
    stride = n_slices * n_layers
    return ref[pl.ds(slice_idx * n_layers + layer, n_cols, stride=stride), :].T


def _regroup_kernel(w_ref, o_ref, *, n_slices):
    n_layers, _, tn = o_ref.shape
    for layer in range(n_layers):
        for t in range(n_slices):
            piece = _read_piece(w_ref, t, layer, tn, n_slices, n_layers)
            o_ref[layer, t * LANES:(t + 1) * LANES, :] = piece.astype(o_ref.dtype)


def _regroup(w_in):
    n_layers, d, cols = w_in.shape
    n_out = MAIN_W + N_BRANCH * d
    tn = REGROUP_W
    w_rows, n_slices = _column_rows(w_in)
    per_col = n_slices * n_layers
    n_plain, n_fox = COL_FOX // tn, 3 * FOX_W // tn

    def source(j):
        skipped = jnp.where(j < n_plain, 0, jnp.where(j < n_plain + n_fox, GLA_RANK, GLA_RANK + FOX_HEADS))
        return (j * tn + skipped) * per_col, 0

    return pl.pallas_call(
        functools.partial(_regroup_kernel, n_slices=n_slices),
        out_shape=jax.ShapeDtypeStruct((n_layers, d, n_out), BF16),
        grid=(n_out // tn,),
        in_specs=[pl.BlockSpec((pl.Element(tn * per_col), pl.Element(LANES)), source)],
        out_specs=pl.BlockSpec((n_layers, d, tn), lambda j: (0, 0, j)),
        compiler_params=_params("parallel"),
        name="regroup_w_in",
    )(w_rows)


def _tail_kernel(a_ref, b_ref, o_ref, *, n_slices):
    n_layers = o_ref.shape[0]
    lane = lax.broadcasted_iota(jnp.int32, (LANES, LANES), 1)
    for layer in range(n_layers):
        for t in range(n_slices):
            a = _read_piece(a_ref, t, layer, LANES, n_slices, n_layers)
            b = _read_piece(b_ref, t, layer, LANES, n_slices, n_layers)
            tail = jnp.where(lane < GLA_RANK, a, jnp.where(lane < GLA_RANK + FOX_HEADS, b, 0.0))
            o_ref[layer, t * LANES:(t + 1) * LANES, :] = tail.astype(o_ref.dtype)


def _tail_weights(w_in):
    n_layers, d, _ = w_in.shape
    w_rows, n_slices = _column_rows(w_in)
    blk_rows = LANES * n_slices * n_layers
    ga_blk = COL_FOX // LANES
    ff_blk = (COL_FOX + GLA_RANK + 3 * FOX_W) // LANES
    return pl.pallas_call(
        functools.partial(_tail_kernel, n_slices=n_slices),
        out_shape=jax.ShapeDtypeStruct((n_layers, d, LANES), BF16),
        grid=(1,),
        in_specs=[
            pl.BlockSpec((blk_rows, LANES), lambda i: (ga_blk, 0)),
            pl.BlockSpec((blk_rows, LANES), lambda i: (ff_blk, 0)),
        ],
        out_specs=pl.BlockSpec((n_layers, d, LANES), lambda i: (0, 0, 0)),
        compiler_params=_params("arbitrary"),
        name="tail_w_in",
    )(w_rows, w_rows)


def _proj_kernel(x_ref, w_ref, wt_ref, o_ref, t_ref, xb_ref):
    @pl.when(pl.program_id(1) == 0)
    def _():
        xb_ref[...] = x_ref[...].astype(BF16)
        t_ref[...] = _dot(xb_ref[...], wt_ref[...])

    o_ref[...] = _dot(xb_ref[...], w_ref[...]).astype(o_ref.dtype)


def _proj(x, w_all, w_tail, layer):
    s, d = x.shape
    tm, tn = min(PROJ_TM, s), PROJ_TN
    return pl.pallas_call(
        _proj_kernel,
        out_shape=(jax.ShapeDtypeStruct((s, MAIN_W), BF16), jax.ShapeDtypeStruct((s, LANES), F32)),
        grid=(s // tm, MAIN_W // tn),
        in_specs=[
            pl.BlockSpec((tm, d), lambda i, j: (i, 0), pipeline_mode=pl.Buffered(1)),
            pl.BlockSpec((None, d, tn), lambda i, j: (layer, 0, j)),
            pl.BlockSpec((None, d, LANES), lambda i, j: (layer, 0, 0)),
        ],
        out_specs=(pl.BlockSpec((tm, tn), lambda i, j: (i, j)),
                   pl.BlockSpec((tm, LANES), lambda i, j: (i, 0))),
        scratch_shapes=[pltpu.VMEM((tm, d), BF16)],
        compiler_params=_params("parallel", "arbitrary"),
        name="in_proj",
    )(x, w_all, w_tail)


def _gla_kernel(q_ref, k_ref, v_ref, gr_ref, ga_ref, wa_ref, ba_ref, ng_ref, o_ref, state_ref):
    rows = q_ref.shape[0]
    n_chunks = rows // GLA_CHUNK
    n_sub = GLA_CHUNK // GLA_SUB

    @pl.when(pl.program_id(0) == 0)
    def _():
        state_ref[...] = jnp.zeros_like(state_ref)

    ga_hi, ga_lo = _split2(ga_ref[...])
    wa_hi, wa_lo = _split2(wa_ref[...])
    xa = _dot(ga_hi, wa_hi) + _dot(ga_hi, wa_lo) + _dot(ga_lo, wa_hi) + ba_ref[...]
    la = -_softplus(-xa) * (1.0 / GLA_TAU)

    r_i = lax.broadcasted_iota(jnp.int32, (rows, rows), 0)
    c_i = lax.broadcasted_iota(jnp.int32, (rows, rows), 1)
    chunk_shift = GLA_CHUNK.bit_length() - 1
    same_chunk = (c_i >> chunk_shift) == (r_i >> chunk_shift)
    tri = jnp.where((c_i <= r_i) & same_chunk, 1.0, 0.0).astype(BF16)
    la_hi, la_mid, la_lo = _split3(la)
    b_all = _dot(tri, la_hi) + _dot(tri, la_mid) + _dot(tri, la_lo)

    e_r = lax.broadcasted_iota(jnp.int32, (GLA_QK, GLA_V), 0)
    e_c = lax.broadcasted_iota(jnp.int32, (GLA_QK, GLA_V), 1)
    same_head = (e_r >> GLA_DK.bit_length() - 1) == (e_c >> GLA_DV.bit_length() - 1)
    expand = jnp.where(same_head, 1.0, 0.0).astype(BF16)
    lane_head = lax.broadcasted_iota(jnp.int32, (1, GLA_QK), 1) >> GLA_DK.bit_length() - 1
    head_mask = [jnp.where(lane_head == h, 1.0, 0.0) for h in range(GLA_HEADS)]
    ones_col = jnp.ones((GLA_CHUNK, GLA_DV), BF16)
    t_sub = lax.broadcasted_iota(jnp.int32, (GLA_SUB, GLA_QK), 0)
    s_chunk = lax.broadcasted_iota(jnp.int32, (GLA_CHUNK, GLA_QK), 0)

    def head_rows(x, n):
        return jnp.concatenate(
            [x[h * n:(h + 1) * n, h * GLA_DV:(h + 1) * GLA_DV] for h in range(GLA_HEADS)], axis=1)

    def stack_heads(x):
        return jnp.concatenate([x * head_mask[h] for h in range(GLA_HEADS)], axis=0).astype(BF16)

    for c in range(n_chunks):
        sl = slice(c * GLA_CHUNK, (c + 1) * GLA_CHUNK)
        q_c = q_ref[sl, :].astype(F32) * (GLA_DK ** -0.5)
        v_cb = v_ref[sl, :]
        k_c = k_ref[sl, :].astype(F32)
        v_c = v_cb.astype(F32)
        b_c = b_all[sl, :]
        b_last = b_c[GLA_CHUNK - 1:GLA_CHUNK, :]

        state = state_ref[...]
        inter = _dot(stack_heads(q_c * jnp.exp(b_c)), state.astype(BF16))
        o_inter = jnp.concatenate(
            [inter[h * GLA_CHUNK:(h + 1) * GLA_CHUNK, :] for h in range(GLA_HEADS)], axis=1)

        kv = _dot_tn((k_c * jnp.exp(b_last - b_c)).astype(BF16), v_cb)
        kv = jnp.concatenate(
            [kv[h * GLA_DK:(h + 1) * GLA_DK, h * GLA_DV:(h + 1) * GLA_DV] for h in range(GLA_HEADS)], axis=0)
        tot = _dot_tn(la_hi[sl, :], ones_col) + _dot_tn(la_mid[sl, :], ones_col)
        state_ref[...] = state * jnp.exp(tot) + kv

        o_rows = []
        for i in range(n_sub):
            rs = slice(i * GLA_SUB, (i + 1) * GLA_SUB)
            q_i, k_i, b_i, v_i = q_c[rs, :], k_c[rs, :], b_c[rs, :], v_c[rs, :]
            ps = []
            for s in range(GLA_SUB):
                diff = jnp.where(t_sub >= s, b_i - b_i[s:s + 1, :], -jnp.inf)
                ps.append(q_i * k_i[s:s + 1, :] * jnp.exp(diff))
            att = _dot(jnp.concatenate(ps, axis=0).astype(BF16), expand)
            o_i = att[0:GLA_SUB, :] * v_i[0:1, :]
            for s in range(1, GLA_SUB):
                o_i += att[s * GLA_SUB:(s + 1) * GLA_SUB, :] * v_i[s:s + 1, :]
            if i > 0:
                ref_row = b_c[i * GLA_SUB - 1:i * GLA_SUB, :]
                q_dec = stack_heads(q_i * jnp.exp(b_i - ref_row))
                k_dec = k_c * jnp.exp(jnp.where(s_chunk < i * GLA_SUB, ref_row - b_c, -jnp.inf))
                att_prev = _dot_nt(q_dec, k_dec.astype(BF16))
                o_i += head_rows(_dot(att_prev.astype(BF16), v_cb), GLA_SUB)
            o_rows.append(o_i)
        o_c = jnp.concatenate(o_rows, axis=0) + o_inter

        outs = []
        for h in range(GLA_HEADS):
            o_h = o_c[:, h * GLA_DV:(h + 1) * GLA_DV]
            ms = jnp.mean(o_h * o_h, axis=-1, keepdims=True)
            outs.append(o_h * lax.rsqrt(ms + LN_EPS) * ng_ref[...])
        gr = gr_ref[sl, :].astype(F32)
        o_ref[sl, :] = (jnp.concatenate(outs, axis=-1) * (gr * _sigmoid(gr))).astype(o_ref.dtype)


def _gla(u, tail, wa_pad, b_a, norm_g):
    s = u.shape[0]
    rows = min(GLA_ROWS, s)
    return pl.pallas_call(
        _gla_kernel,
        out_shape=jax.ShapeDtypeStruct((s, GLA_V), BF16),
        grid=(s // rows,),
        in_specs=[
            pl.BlockSpec((rows, GLA_QK), lambda i: (i, 0)),
            pl.BlockSpec((rows, GLA_QK), lambda i: (i, 1)),
            pl.BlockSpec((rows, GLA_V), lambda i: (i, COL_GV // GLA_V)),
            pl.BlockSpec((rows, GLA_V), lambda i: (i, COL_GR // GLA_V)),
            pl.BlockSpec((rows, LANES), lambda i: (i, 0)),
            pl.BlockSpec((LANES, GLA_QK), lambda i: (0, 0)),
            pl.BlockSpec((1, GLA_QK), lambda i: (0, 0)),
            pl.BlockSpec((1, GLA_DV), lambda i: (0, 0)),
        ],
        out_specs=pl.BlockSpec((rows, GLA_V), lambda i: (i, 0)),
        scratch_shapes=[pltpu.VMEM((GLA_QK, GLA_DV), F32)],
        compiler_params=_params("arbitrary"),
        name="gla",
    )(u, u, u, u, tail, wa_pad, b_a.reshape(1, GLA_QK), norm_g.reshape(1, GLA_DV))


def _cumsum_kernel(ff_ref, bf_ref, c_ref, carry_ref):
    rows = ff_ref.shape[0]

    @pl.when(pl.program_id(0) == 0)
    def _():
        carry_ref[...] = jnp.zeros_like(carry_ref)

    log_f = -_softplus(-(ff_ref[...] + bf_ref[...]))
    r_i = lax.broadcasted_iota(jnp.int32, (rows, rows), 0)
    c_i = lax.broadcasted_iota(jnp.int32, (rows, rows), 1)
    tri = jnp.where(c_i <= r_i, 1.0, 0.0).astype(BF16)
    hi, mid, lo = _split3(log_f)
    c = _dot(tri, hi) + _dot(tri, mid) + _dot(tri, lo) + carry_ref[...]
    c_ref[...] = c
    carry_ref[...] = c[rows - 1:rows, :]


def _fox_cumsum(tail, bf_pad):
    s = tail.shape[0]
    rows = min(CUM_ROWS, s)
    return pl.pallas_call(
        _cumsum_kernel,
        out_shape=jax.ShapeDtypeStruct((s, LANES), F32),
        grid=(s // rows,),
        in_specs=[
            pl.BlockSpec((rows, LANES), lambda i: (i, 0)),
            pl.BlockSpec((1, LANES), lambda i: (0, 0)),
        ],
        out_specs=pl.BlockSpec((rows, LANES), lambda i: (i, 0)),
        scratch_shapes=[pltpu.VMEM((1, LANES), F32)],
        compiler_params=_params("arbitrary"),
        name="fox_cumsum",
    )(tail, bf_pad)


def _fox_kernel(q_ref, k_ref, v_ref, c_ref, ck_ref, o_ref, kmax_ref, *, scale):
    tb = q_ref.shape[0]
    heads = q_ref.shape[1] // HEAD_DIM
    qi = pl.program_id(0)
    lanes = [slice(h * HEAD_DIM, (h + 1) * HEAD_DIM) for h in range(heads)]

    @pl.when(qi == 0)
    def _():
        for h in range(heads):
            kf = k_ref[:, lanes[h]].astype(F32)
            kmax_ref[h] = jnp.sqrt(jnp.max(jnp.sum(kf * kf, axis=-1, keepdims=True), axis=0, keepdims=True))

    qs = [q_ref[:, lanes[h]] for h in range(heads)]
    cqs = [c_ref[:, FF_LANE + h:FF_LANE + h + 1] for h in range(heads)]
    bounds = []
    for h in range(heads):
        qf = qs[h].astype(F32)
        bounds.append(jnp.sqrt(jnp.sum(qf * qf, axis=-1, keepdims=True)) * kmax_ref[h] * scale + cqs[h])
    row = lax.broadcasted_iota(jnp.int32, (tb, tb), 0)
    col = lax.broadcasted_iota(jnp.int32, (tb, tb), 1)

    def block(j, state, masked):
        start = pl.multiple_of(j * tb, tb)
        hs = range(heads)
        ss = [_dot_nt(qs[h], k_ref[pl.ds(start, tb), lanes[h]]) for h in hs]
        ss = [ss[h] * scale + cqs[h] - ck_ref[h, j] for h in hs]
        if masked:
            ss = [jnp.where(col <= row, s, MASK_VALUE) for s in ss]
        m_new = [jnp.maximum(state[h][0], jnp.max(ss[h], axis=-1, keepdims=True)) for h in hs]
        ps = [jnp.exp(ss[h] - m_new[h]) for h in hs]
        corr = [jnp.exp(state[h][0] - m_new[h]) for h in hs]
        ls = [corr[h] * state[h][1] + jnp.sum(ps[h], axis=-1, keepdims=True) for h in hs]
        pv = [_dot(ps[h].astype(BF16), v_ref[pl.ds(start, tb), lanes[h]]) for h in hs]
        return tuple((m_new[h], ls[h], corr[h] * state[h][2] + pv[h]) for h in hs)

    def worth(j, state):
        jc = jnp.maximum(j, 0)
        gap = bounds[0] - ck_ref[0, jc][:, tb - 1:tb] - state[0][0]
        for h in range(1, heads):
            gap = jnp.maximum(gap, bounds[h] - ck_ref[h, jc][:, tb - 1:tb] - state[h][0])
        return (jnp.max(gap) > -SKIP_LOG).astype(jnp.int32)

    init = tuple((jnp.full((tb, 1), MASK_VALUE, F32), jnp.zeros((tb, 1), F32), jnp.zeros((tb, HEAD_DIM), F32))
                 for _ in range(heads))
    first = block(qi, init, True)

    def cond(c):
        return (c[0] >= 0) & (c[1] > 0)

    def body(c):
        j, _, state = c
        state = block(j, state, False)
        return j - 1, worth(j - 1, state), state

    _, _, final = lax.while_loop(cond, body, (qi - 1, worth(qi - 1, first), first))
    for h in range(heads):
        _, l, acc = final[h]
        o_ref[:, lanes[h]] = (acc / l).astype(o_ref.dtype)


def _fox(u, c, ck):
    s = u.shape[0]
    tb = min(FOX_BLOCK, s)
    nb = s // tb
    col0 = COL_FOX // FOX_W
    resident = lambda blk: pl.BlockSpec((s, FOX_W), lambda i: (0, blk), pipeline_mode=pl.Buffered(1))
    return pl.pallas_call(
        functools.partial(_fox_kernel, scale=HEAD_DIM ** -0.5),
        out_shape=jax.ShapeDtypeStruct((s, FOX_W), BF16),
        grid=(nb,),
        in_specs=[
            pl.BlockSpec((tb, FOX_W), lambda i: (i, col0)),
            resident(col0 + 1),
            resident(col0 + 2),
            pl.BlockSpec((tb, LANES), lambda i: (i, 0)),
            pl.BlockSpec((FOX_HEADS, nb, 1, tb), lambda i: (0, 0, 0, 0)),
        ],
        out_specs=pl.BlockSpec((tb, FOX_W), lambda i: (i, 0)),
        scratch_shapes=[pltpu.VMEM((FOX_HEADS, 1, 1), F32)],
        compiler_params=_params("arbitrary"),
        name="fox_attn",
    )(u, u, u, c, ck)


def _sb_kernel(q_ref, k_ref, v_ref, o_ref, *, scale):
    tb = q_ref.shape[0]
    heads = q_ref.shape[1] // HEAD_DIM
    qi = pl.program_id(0)
    lanes = [slice(h * HEAD_DIM, (h + 1) * HEAD_DIM) for h in range(heads)]
    qs = [q_ref[:, lanes[h]] for h in range(heads)]
    row = lax.broadcasted_iota(jnp.int32, (tb, tb), 0)
    col = lax.broadcasted_iota(jnp.int32, (tb, tb), 1)
    upper = jnp.where(row > col, 1.0, 0.0).astype(BF16)

    def block(j, state, masked):
        start = pl.multiple_of(j * tb, tb)
        hs = range(heads)
        zs = [_dot_nt(qs[h], k_ref[pl.ds(start, tb), lanes[h]]) * scale for h in hs]
        sps = [_softplus(z) for z in zs]
        if masked:
            valid = col < row
            log_1mb = [jnp.where(valid, -sp, 0.0) for sp in sps]
        else:
            log_1mb = [-sp for sp in sps]
        parts = [_split2(x) for x in log_1mb]
        rest = [_dot(hi, upper) + _dot(lo, upper) for hi, lo in parts]
        a = [jnp.exp(zs[h] - sps[h] + rest[h] + state[h][0]) for h in hs]
        if masked:
            a = [jnp.where(valid, x, 0.0) for x in a]
        pv = [_dot(a[h].astype(BF16), v_ref[pl.ds(start, tb), lanes[h]]) for h in hs]
        return tuple((state[h][0] + rest[h][:, :1] + log_1mb[h][:, :1], state[h][1] + pv[h]) for h in hs)

    def worth(state):
        top = state[0][0]
        for h in range(1, heads):
            top = jnp.maximum(top, state[h][0])
        return (jnp.max(top) > -SKIP_LOG).astype(jnp.int32)

    init = tuple((jnp.zeros((tb, 1), F32), jnp.zeros((tb, HEAD_DIM), F32)) for _ in range(heads))
    first = block(qi, init, True)

    def cond(c):
        return (c[0] >= 0) & (c[1] > 0)

    def body(c):
        j, _, state = c
        state = block(j, state, False)
        return j - 1, worth(state), state

    _, _, final = lax.while_loop(cond, body, (qi - 1, worth(first), first))
    for h in range(heads):
        o_ref[:, lanes[h]] = final[h][1].astype(o_ref.dtype)


def _stick_breaking(u):
    s = u.shape[0]
    tb = min(SB_BLOCK, s)
    col0 = COL_SB // SB_W
    resident = lambda blk: pl.BlockSpec((s, SB_W), lambda i: (0, blk), pipeline_mode=pl.Buffered(1))
    return pl.pallas_call(
        functools.partial(_sb_kernel, scale=HEAD_DIM ** -0.5),
        out_shape=jax.ShapeDtypeStruct((s, SB_W), BF16),
        grid=(s // tb,),
        in_specs=[pl.BlockSpec((tb, SB_W), lambda i: (i, col0)), resident(col0 + 1), resident(col0 + 2)],
        out_specs=pl.BlockSpec((tb, SB_W), lambda i: (i, 0)),
        compiler_params=_params("parallel"),
        name="sb_attn",
    )(u, u, u)


def _merge_kernel(x_ref, og_ref, of_ref, os_ref, wg0_ref, wg1_ref, wg2_ref, wbg_ref, wbf_ref, wbs_ref,
                  wo_ref, g_ref, b_ref, o_ref, xb_ref, *, alpha):
    j = pl.program_id(1)

    @pl.when(j == 0)
    def _():
        x = x_ref[...]
        xb_ref[...] = x.astype(BF16)
        o_ref[...] = alpha * x

    xb = xb_ref[...]
    m = _sigmoid(_dot(xb, wg0_ref[...])) * _dot(og_ref[...], wbg_ref[...])
    m += _sigmoid(_dot(xb, wg1_ref[...])) * _dot(of_ref[...], wbf_ref[...])
    m += _sigmoid(_dot(xb, wg2_ref[...])) * _dot(os_ref[...], wbs_ref[...])
    o_ref[...] += _dot(m.astype(BF16), wo_ref[...])

    @pl.when(j == pl.num_programs(1) - 1)
    def _():
        o_ref[...] = _layernorm(o_ref[...], g_ref[...], b_ref[...])


def _merge_ln(x, o_gla, o_fox, o_sb, w_all, w_bg, w_bf, w_bs, w_out, g, b, layer, alpha):
    s, d = x.shape
    tm, tn = min(MERGE_TM, s), min(MERGE_TN, d)
    nj = d // tn
    gate0 = MAIN_W // tn
    row_spec = lambda w: pl.BlockSpec((tm, w), lambda i, j: (i, 0), pipeline_mode=pl.Buffered(1))
    gate_spec = lambda br: pl.BlockSpec((None, d, tn), lambda i, j: (layer, 0, gate0 + br * nj + j))
    col_spec = lambda w: pl.BlockSpec((None, w, tn), lambda i, j: (layer, 0, j))
    vec_spec = pl.BlockSpec((None, 1, d), lambda i, j: (layer, 0, 0))
    return pl.pallas_call(
        functools.partial(_merge_kernel, alpha=alpha),
        out_shape=jax.ShapeDtypeStruct((s, d), F32),
        grid=(s // tm, nj),
        in_specs=[
            row_spec(d), row_spec(GLA_V), row_spec(FOX_W), row_spec(SB_W),
            gate_spec(0), gate_spec(1), gate_spec(2),
            col_spec(GLA_V), col_spec(FOX_W), col_spec(SB_W),
            pl.BlockSpec((None, tn, d), lambda i, j: (layer, j, 0)),
            vec_spec, vec_spec,
        ],
        out_specs=pl.BlockSpec((tm, d), lambda i, j: (i, 0)),
        scratch_shapes=[pltpu.VMEM((tm, d), BF16)],
        compiler_params=_params("parallel", "arbitrary"),
        name="merge_ln",
    )(x, o_gla, o_fox, o_sb, w_all, w_all, w_all, w_bg, w_bf, w_bs, w_out, g[:, None, :], b[:, None, :])


def _mixer_branches(x, w_all, w_tail, layer, gla_w_a2, gla_b_a, gla_norm_g, fox_b_f):
    s, d = x.shape
    u, tail = _proj(x, w_all, w_tail, layer)

    wa_pad = jnp.zeros((LANES, GLA_QK), F32).at[:GLA_RANK].set(gla_w_a2)
    o_gla = _gla(u, tail, wa_pad, gla_b_a, gla_norm_g)

    bf_pad = jnp.zeros((1, LANES), F32).at[0, FF_LANE:FF_LANE + FOX_HEADS].set(fox_b_f)
    c = _fox_cumsum(tail, bf_pad)
    ct = c[:, FF_LANE:FF_LANE + FOX_HEADS].T
    tb = min(FOX_BLOCK, s)
    o_fox = _fox(u, c, ct.reshape(FOX_HEADS, s // tb, 1, tb))
    o_sb = _stick_breaking(u)
    return o_gla, o_fox, o_sb


def _mixer_ln(x, w_all, w_tail, layer, gla_w_a2, gla_b_a, gla_norm_g, fox_b_f, w_bg, w_bf, w_bs, w_out,
              ln_g, ln_b, alpha):
    o_gla, o_fox, o_sb = _mixer_branches(x, w_all, w_tail, layer, gla_w_a2, gla_b_a, gla_norm_g, fox_b_f)
    return _merge_ln(x, o_gla, o_fox, o_sb, w_all, w_bg, w_bf, w_bs, w_out, ln_g, ln_b, layer, alpha)


def kernel(x, ffn1_w_gate, ffn1_w_up, ffn1_w_down, ln1_g, ln1_b, w_in, gla_w_a2, gla_b_a, gla_norm_g, fox_b_f, w_br_gla, w_br_fox, w_br_sb, w_out, ln2_g, ln2_b, ffn2_w_gate, ffn2_w_up, ffn2_w_down, ln3_g, ln3_b):
    batch, seq, d = x.shape
    depth = w_in.shape[0]
    alpha = (2 * depth) ** 0.25
    cast = lambda w: w.astype(BF16)
    ffn1 = (cast(ffn1_w_gate), cast(ffn1_w_up), cast(ffn1_w_down))
    ffn2 = (cast(ffn2_w_gate), cast(ffn2_w_up), cast(ffn2_w_down))
    branch = (cast(w_br_gla), cast(w_br_fox), cast(w_br_sb), cast(w_out))
    assert (N_BRANCH * d) % REGROUP_W == 0 and w_in.shape[2] == MAIN_W + GLA_RANK + FOX_HEADS + N_BRANCH * d
    w_all, w_tail = _regroup(w_in), _tail_weights(w_in)
    outs = []
    for bi in range(batch):
        h = x[bi]
        for l in range(depth):
            h = _ffn_ln(h, *ffn1, ln1_g, ln1_b, l, alpha)
            h = _mixer_ln(h, w_all, w_tail, l, gla_w_a2[l], gla_b_a[l], gla_norm_g[l], fox_b_f[l], *branch,
                          ln2_g, ln2_b, alpha)
            h = _ffn_ln(h, *ffn2, ln3_g, ln3_b, l, alpha)
        outs.append(h)
    return jnp.stack(outs, axis=0)
```

```python
import functools

import jax
import jax.numpy as jnp
from jax import lax
from jax.experimental import pallas as pl
from jax.experimental.pallas import tpu as pltpu

F32 = jnp.float32
BF16 = jnp.bfloat16

LN_EPS = 1e-5
GLA_HEADS = 4
GLA_DK = 64
GLA_DV = 128
GLA_RANK = 16
GLA_TAU = 16.0
GLA_CHUNK = 64
GLA_SUB = 16
GLA_QK = GLA_HEADS * GLA_DK
GLA_V = GLA_HEADS * GLA_DV
FOX_HEADS = 6
SB_HEADS = 6
HEAD_DIM = 128
FOX_W = FOX_HEADS * HEAD_DIM
SB_W = SB_HEADS * HEAD_DIM
N_BRANCH = 3
LANES = 128

REGROUP_W = 256
MAIN_W = 2 * GLA_QK + 2 * GLA_V + 3 * FOX_W + 3 * SB_W
COL_GV = 2 * GLA_QK
COL_GR = COL_GV + GLA_V
COL_FOX = COL_GR + GLA_V
COL_SB = COL_FOX + 3 * FOX_W
FF_LANE = GLA_RANK

FFN_TM = 1024
FFN_TF = 256
PROJ_TM = 2048
PROJ_TN = 512
MERGE_TM = 512
MERGE_TN = 512
GLA_ROWS = 256
CUM_ROWS = 1024
FOX_BLOCK = 256
SB_BLOCK = 128
VMEM_LIMIT = 52 * 1024 * 1024
MASK_VALUE = -1e30
SKIP_LOG = 22.0


def _dot(a, b):
    return jnp.dot(a, b, preferred_element_type=F32)


def _dot_nt(a, b):
    return lax.dot_general(a, b, (((1,), (1,)), ((), ())), preferred_element_type=F32)


def _dot_tn(a, b):
    return lax.dot_general(a, b, (((0,), (0,)), ((), ())), preferred_element_type=F32)


def _split2(x):
    hi = x.astype(BF16)
    lo = (x - hi.astype(F32)).astype(BF16)
    return hi, lo


def _split3(x):
    hi = x.astype(BF16)
    r = x - hi.astype(F32)
    mid = r.astype(BF16)
    lo = (r - mid.astype(F32)).astype(BF16)
    return hi, mid, lo


def _softplus(x):
    return jnp.maximum(x, 0.0) + jnp.log(1.0 + jnp.exp(-jnp.abs(x)))


def _sigmoid(x):
    return 1.0 / (1.0 + jnp.exp(-x))


def _layernorm(y, g, b):
    mu = jnp.mean(y, axis=-1, keepdims=True)
    d = y - mu
    var = jnp.mean(d * d, axis=-1, keepdims=True)
    return d * lax.rsqrt(var + LN_EPS) * g + b


def _params(*sem):
    return pltpu.CompilerParams(dimension_semantics=sem, vmem_limit_bytes=VMEM_LIMIT)


def _ffn_kernel(x_ref, wg_ref, wu_ref, wd_ref, g_ref, b_ref, o_ref, xb_ref, *, alpha):
    f = pl.program_id(1)

    @pl.when(f == 0)
    def _():
        x = x_ref[...]
        xb_ref[...] = x.astype(BF16)
        o_ref[...] = (2.0 * alpha) * x

    xb = xb_ref[...]
    gate = _dot(xb, wg_ref[...].astype(BF16))
    up = _dot(xb, wu_ref[...].astype(BF16))
    h = gate * _sigmoid(gate) * up
    o_ref[...] += _dot(h.astype(BF16), wd_ref[...].astype(BF16))

    @pl.when(f == pl.num_programs(1) - 1)
    def _():
        o_ref[...] = _layernorm(0.5 * o_ref[...], g_ref[...], b_ref[...])


def _ffn_ln(x, wg, wu, wd, g, b, layer, alpha):
    s, d = x.shape
    f = wg.shape[2]
    tm, tf = min(FFN_TM, s), min(FFN_TF, f)
    vec_spec = pl.BlockSpec((None, 1, d), lambda i, j: (layer, 0, 0))
    return pl.pallas_call(
        functools.partial(_ffn_kernel, alpha=alpha),
        out_shape=jax.ShapeDtypeStruct((s, d), F32),
        grid=(s // tm, f // tf),
        in_specs=[
            pl.BlockSpec((tm, d), lambda i, j: (i, 0), pipeline_mode=pl.Buffered(1)),
            pl.BlockSpec((None, d, tf), lambda i, j: (layer, 0, j)),
            pl.BlockSpec((None, d, tf), lambda i, j: (layer, 0, j)),
            pl.BlockSpec((None, tf, d), lambda i, j: (layer, j, 0)),
            vec_spec, vec_spec,
        ],
        out_specs=pl.BlockSpec((tm, d), lambda i, j: (i, 0)),
        scratch_shapes=[pltpu.VMEM((tm, d), BF16)],
        compiler_params=_params("parallel", "arbitrary"),
        name="ffn_ln",
    )(x, wg, wu, wd, g[:, None, :], b[:, None, :])


def _column_rows(w_in):
    n_layers, d, cols = w_in.shape
    n_slices = d // LANES
    rows = jnp.transpose(w_in.reshape(n_layers, n_slices, LANES, cols), (3, 1, 0, 2))
    return rows.reshape(cols * n_slices * n_layers, LANES), n_slices


def _read_piece(ref, slice_idx, layer, n_cols, n_slices, n_layers):
    stride = n_slices * n_layers
    return ref[pl.ds(slice_idx * n_layers + layer, n_cols, stride=stride), :].T


def _regroup_kernel(w_ref, o_ref, *, n_slices):
    n_layers, _, tn = o_ref.shape
    for layer in range(n_layers):
        for t in range(n_slices):
            piece = _read_piece(w_ref, t, layer, tn, n_slices, n_layers)
            o_ref[layer, t * LANES:(t + 1) * LANES, :] = piece.astype(o_ref.dtype)


def _regroup(w_in):
    n_layers, d, cols = w_in.shape
    n_out = MAIN_W + N_BRANCH * d
    tn = REGROUP_W
    w_rows, n_slices = _column_rows(w_in)
    per_col = n_slices * n_layers
    n_plain, n_fox = COL_FOX // tn, 3 * FOX_W // tn

    def source(j):
        skipped = jnp.where(j < n_plain, 0, jnp.where(j < n_plain + n_fox, GLA_RANK, GLA_RANK + FOX_HEADS))
        return (j * tn + skipped) * per_col, 0

    return pl.pallas_call(
        functools.partial(_regroup_kernel, n_slices=n_slices),
        out_shape=jax.ShapeDtypeStruct((n_layers, d, n_out), BF16),
        grid=(n_out // tn,),
        in_specs=[pl.BlockSpec((pl.Element(tn * per_col), pl.Element(LANES)), source)],
        out_specs=pl.BlockSpec((n_layers, d, tn), lambda j: (0, 0, j)),
        compiler_params=_params("parallel"),
        name="regroup_w_in",
    )(w_rows)


def _tail_kernel(a_ref, b_ref, o_ref, *, n_slices):
    n_layers = o_ref.shape[0]
    lane = lax.broadcasted_iota(jnp.int32, (LANES, LANES), 1)
    for layer in range(n_layers):
        for t in range(n_slices):
            a = _read_piece(a_ref, t, layer, LANES, n_slices, n_layers)
            b = _read_piece(b_ref, t, layer, LANES, n_slices, n_layers)
            tail = jnp.where(lane < GLA_RANK, a, jnp.where(lane < GLA_RANK + FOX_HEADS, b, 0.0))
            o_ref[layer, t * LANES:(t + 1) * LANES, :] = tail.astype(o_ref.dtype)


def _tail_weights(w_in):
    n_layers, d, _ = w_in.shape
    w_rows, n_slices = _column_rows(w_in)
    blk_rows = LANES * n_slices * n_layers
    ga_blk = COL_FOX // LANES
    ff_blk = (COL_FOX + GLA_RANK + 3 * FOX_W) // LANES
    return pl.pallas_call(
        functools.partial(_tail_kernel, n_slices=n_slices),
        out_shape=jax.ShapeDtypeStruct((n_layers, d, LANES), BF16),
        grid=(1,),
        in_specs=[
            pl.BlockSpec((blk_rows, LANES), lambda i: (ga_blk, 0)),
            pl.BlockSpec((blk_rows, LANES), lambda i: (ff_blk, 0)),
        ],
        out_specs=pl.BlockSpec((n_layers, d, LANES), lambda i: (0, 0, 0)),
        compiler_params=_params("arbitrary"),
        name="tail_w_in",
    )(w_rows, w_rows)


def _proj_kernel(x_ref, w_ref, wt_ref, o_ref, t_ref, xb_ref):
    @pl.when(pl.program_id(1) == 0)
    def _():
        xb_ref[...] = x_ref[...].astype(BF16)
        t_ref[...] = _dot(xb_ref[...], wt_ref[...])

    o_ref[...] = _dot(xb_ref[...], w_ref[...]).astype(o_ref.dtype)


def _proj(x, w_all, w_tail, layer):
    s, d = x.shape
    tm, tn = min(PROJ_TM, s), PROJ_TN
    return pl.pallas_call(
        _proj_kernel,
        out_shape=(jax.ShapeDtypeStruct((s, MAIN_W), BF16), jax.ShapeDtypeStruct((s, LANES), F32)),
        grid=(s // tm, MAIN_W // tn),
        in_specs=[
            pl.BlockSpec((tm, d), lambda i, j: (i, 0), pipeline_mode=pl.Buffered(1)),
            pl.BlockSpec((None, d, tn), lambda i, j: (layer, 0, j)),
            pl.BlockSpec((None, d, LANES), lambda i, j: (layer, 0, 0)),
        ],
        out_specs=(pl.BlockSpec((tm, tn), lambda i, j: (i, j)),
                   pl.BlockSpec((tm, LANES), lambda i, j: (i, 0))),
        scratch_shapes=[pltpu.VMEM((tm, d), BF16)],
        compiler_params=_params("parallel", "arbitrary"),
        name="in_proj",
    )(x, w_all, w_tail)


def _gla_kernel(q_ref, k_ref, v_ref, gr_ref, ga_ref, wa_ref, ba_ref, ng_ref, o_ref, state_ref):
    rows = q_ref.shape[0]
    n_chunks = rows // GLA_CHUNK
    n_sub = GLA_CHUNK // GLA_SUB

    @pl.when(pl.program_id(0) == 0)
    def _():
        state_ref[...] = jnp.zeros_like(state_ref)

    ga_hi, ga_lo = _split2(ga_ref[...])
    wa_hi, wa_lo = _split2(wa_ref[...])
    xa = _dot(ga_hi, wa_hi) + _dot(ga_hi, wa_lo) + _dot(ga_lo, wa_hi) + ba_ref[...]
    la = -_softplus(-xa) * (1.0 / GLA_TAU)

    r_i = lax.broadcasted_iota(jnp.int32, (rows, rows), 0)
    c_i = lax.broadcasted_iota(jnp.int32, (rows, rows), 1)
    chunk_shift = GLA_CHUNK.bit_length() - 1
    same_chunk = (c_i >> chunk_shift) == (r_i >> chunk_shift)
    tri = jnp.where((c_i <= r_i) & same_chunk, 1.0, 0.0).astype(BF16)
    la_hi, la_mid, la_lo = _split3(la)
    b_all = _dot(tri, la_hi) + _dot(tri, la_mid) + _dot(tri, la_lo)

    e_r = lax.broadcasted_iota(jnp.int32, (GLA_QK, GLA_V), 0)
    e_c = lax.broadcasted_iota(jnp.int32, (GLA_QK, GLA_V), 1)
    same_head = (e_r >> GLA_DK.bit_length() - 1) == (e_c >> GLA_DV.bit_length() - 1)
    expand = jnp.where(same_head, 1.0, 0.0).astype(BF16)
    lane_head = lax.broadcasted_iota(jnp.int32, (1, GLA_QK), 1) >> GLA_DK.bit_length() - 1
    head_mask = [jnp.where(lane_head == h, 1.0, 0.0) for h in range(GLA_HEADS)]
    ones_col = jnp.ones((GLA_CHUNK, GLA_DV), BF16)
    t_sub = lax.broadcasted_iota(jnp.int32, (GLA_SUB, GLA_QK), 0)
    s_chunk = lax.broadcasted_iota(jnp.int32, (GLA_CHUNK, GLA_QK), 0)

    def head_rows(x, n):
        return jnp.concatenate(
            [x[h * n:(h + 1) * n, h * GLA_DV:(h + 1) * GLA_DV] for h in range(GLA_HEADS)], axis=1)

    def stack_heads(x):
        return jnp.concatenate([x * head_mask[h] for h in range(GLA_HEADS)], axis=0).astype(BF16)

    chunks = range(n_chunks)
    sls = [slice(c * GLA_CHUNK, (c + 1) * GLA_CHUNK) for c in chunks]
    q = [q_ref[sl, :].astype(F32) * (GLA_DK ** -0.5) for sl in sls]
    v_b = [v_ref[sl, :] for sl in sls]
    k = [k_ref[sl, :].astype(F32) for sl in sls]
    v = [x.astype(F32) for x in v_b]
    b = [b_all[sl, :] for sl in sls]
    b_last = [x[GLA_CHUNK - 1:GLA_CHUNK, :] for x in b]

    kv = [_dot_tn((k[c] * jnp.exp(b_last[c] - b[c])).astype(BF16), v_b[c]) for c in chunks]
    kv = [jnp.concatenate([x[h * GLA_DK:(h + 1) * GLA_DK, h * GLA_DV:(h + 1) * GLA_DV] for h in range(GLA_HEADS)],
                          axis=0) for x in kv]
    decay = [jnp.exp(_dot_tn(la_hi[sl, :], ones_col) + _dot_tn(la_mid[sl, :], ones_col)) for sl in sls]
    states = [state_ref[...]]
    for c in chunks:
        states.append(states[c] * decay[c] + kv[c])
    state_ref[...] = states[n_chunks]

    inter = [_dot(stack_heads(q[c] * jnp.exp(b[c])), states[c].astype(BF16)) for c in chunks]

    subs = [(c, i) for c in chunks for i in range(n_sub)]
    rs = [slice(i * GLA_SUB, (i + 1) * GLA_SUB) for i in range(n_sub)]
    probs = {}
    for c, i in subs:
        q_i, k_i, b_i = q[c][rs[i], :], k[c][rs[i], :], b[c][rs[i], :]
        ps = []
        for s in range(GLA_SUB):
            diff = jnp.where(t_sub >= s, b_i - b_i[s:s + 1, :], -jnp.inf)
            ps.append(q_i * k_i[s:s + 1, :] * jnp.exp(diff))
        probs[c, i] = jnp.concatenate(ps, axis=0).astype(BF16)
    att = {key: _dot(probs[key], expand) for key in subs}
    out = {}
    for c, i in subs:
        v_i = v[c][rs[i], :]
        o_i = att[c, i][0:GLA_SUB, :] * v_i[0:1, :]
        for s in range(1, GLA_SUB):
            o_i += att[c, i][s * GLA_SUB:(s + 1) * GLA_SUB, :] * v_i[s:s + 1, :]
        out[c, i] = o_i

    later = [(c, i) for c, i in subs if i > 0]
    ref_row = {(c, i): b[c][i * GLA_SUB - 1:i * GLA_SUB, :] for c, i in later}
    q_dec = {(c, i): stack_heads(q[c][rs[i], :] * jnp.exp(b[c][rs[i], :] - ref_row[c, i])) for c, i in later}
    k_dec = {(c, i): (k[c] * jnp.exp(jnp.where(s_chunk < i * GLA_SUB, ref_row[c, i] - b[c], -jnp.inf))).astype(BF16)
             for c, i in later}
    att_prev = {key: _dot_nt(q_dec[key], k_dec[key]).astype(BF16) for key in later}
    for c, i in later:
        out[c, i] += head_rows(_dot(att_prev[c, i], v_b[c]), GLA_SUB)

    for c in chunks:
        o_inter = jnp.concatenate(
            [inter[c][h * GLA_CHUNK:(h + 1) * GLA_CHUNK, :] for h in range(GLA_HEADS)], axis=1)
        o_c = jnp.concatenate([out[c, i] for i in range(n_sub)], axis=0) + o_inter
        outs = []
        for h in range(GLA_HEADS):
            o_h = o_c[:, h * GLA_DV:(h + 1) * GLA_DV]
            ms = jnp.mean(o_h * o_h, axis=-1, keepdims=True)
            outs.append(o_h * lax.rsqrt(ms + LN_EPS) * ng_ref[...])
        gr = gr_ref[sls[c], :].astype(F32)
        o_ref[sls[c], :] = (jnp.concatenate(outs, axis=-1) * (gr * _sigmoid(gr))).astype(o_ref.dtype)


def _gla(u, tail, wa_pad, b_a, norm_g):
    s = u.shape[0]
    rows = min(GLA_ROWS, s)
    return pl.pallas_call(
        _gla_kernel,
        out_shape=jax.ShapeDtypeStruct((s, GLA_V), BF16),
        grid=(s // rows,),
        in_specs=[
            pl.BlockSpec((rows, GLA_QK), lambda i: (i, 0)),
            pl.BlockSpec((rows, GLA_QK), lambda i: (i, 1)),
            pl.BlockSpec((rows, GLA_V), lambda i: (i, COL_GV // GLA_V)),
            pl.BlockSpec((rows, GLA_V), lambda i: (i, COL_GR // GLA_V)),
            pl.BlockSpec((rows, LANES), lambda i: (i, 0)),
            pl.BlockSpec((LANES, GLA_QK), lambda i: (0, 0)),
            pl.BlockSpec((1, GLA_QK), lambda i: (0, 0)),
            pl.BlockSpec((1, GLA_DV), lambda i: (0, 0)),
        ],
        out_specs=pl.BlockSpec((rows, GLA_V), lambda i: (i, 0)),
        scratch_shapes=[pltpu.VMEM((GLA_QK, GLA_DV), F32)],
        compiler_params=_params("arbitrary"),
        name="gla",
    )(u, u, u, u, tail, wa_pad, b_a.reshape(1, GLA_QK), norm_g.reshape(1, GLA_DV))


def _cumsum_kernel(ff_ref, bf_ref, c_ref, carry_ref):
    rows = ff_ref.shape[0]

    @pl.when(pl.program_id(0) == 0)
    def _():
        carry_ref[...] = jnp.zeros_like(carry_ref)

    log_f = -_softplus(-(ff_ref[...] + bf_ref[...]))
    r_i = lax.broadcasted_iota(jnp.int32, (rows, rows), 0)
    c_i = lax.broadcasted_iota(jnp.int32, (rows, rows), 1)
    tri = jnp.where(c_i <= r_i, 1.0, 0.0).astype(BF16)
    hi, mid, lo = _split3(log_f)
    c = _dot(tri, hi) + _dot(tri, mid) + _dot(tri, lo) + carry_ref[...]
    c_ref[...] = c
    carry_ref[...] = c[rows - 1:rows, :]


def _fox_cumsum(tail, bf_pad):
    s = tail.shape[0]
    rows = min(CUM_ROWS, s)
    return pl.pallas_call(
        _cumsum_kernel,
        out_shape=jax.ShapeDtypeStruct((s, LANES), F32),
        grid=(s // rows,),
        in_specs=[
            pl.BlockSpec((rows, LANES), lambda i: (i, 0)),
            pl.BlockSpec((1, LANES), lambda i: (0, 0)),
        ],
        out_specs=pl.BlockSpec((rows, LANES), lambda i: (i, 0)),
        scratch_shapes=[pltpu.VMEM((1, LANES), F32)],
        compiler_params=_params("arbitrary"),
        name="fox_cumsum",
    )(tail, bf_pad)


def _fox_kernel(q_ref, k_ref, v_ref, c_ref, ck_ref, o_ref, kmax_ref, *, scale):
    tb = q_ref.shape[0]
    heads = q_ref.shape[1] // HEAD_DIM
    qi = pl.program_id(0)
    lanes = [slice(h * HEAD_DIM, (h + 1) * HEAD_DIM) for h in range(heads)]

    @pl.when(qi == 0)
    def _():
        for h in range(heads):
            kf = k_ref[:, lanes[h]].astype(F32)
            kmax_ref[h] = jnp.sqrt(jnp.max(jnp.sum(kf * kf, axis=-1, keepdims=True), axis=0, keepdims=True))

    qs = [q_ref[:, lanes[h]] for h in range(heads)]
    cqs = [c_ref[:, FF_LANE + h:FF_LANE + h + 1] for h in range(heads)]
    bounds = []
    for h in range(heads):
        qf = qs[h].astype(F32)
        bounds.append(jnp.sqrt(jnp.sum(qf * qf, axis=-1, keepdims=True)) * kmax_ref[h] * scale + cqs[h])
    row = lax.broadcasted_iota(jnp.int32, (tb, tb), 0)
    col = lax.broadcasted_iota(jnp.int32, (tb, tb), 1)

    def block(j, state, masked):
        start = pl.multiple_of(j * tb, tb)
        hs = range(heads)
        ss = [_dot_nt(qs[h], k_ref[pl.ds(start, tb), lanes[h]]) for h in hs]
        ss = [ss[h] * scale + cqs[h] - ck_ref[h, j] for h in hs]
        if masked:
            ss = [jnp.where(col <= row, s, MASK_VALUE) for s in ss]
        m_new = [jnp.maximum(state[h][0], jnp.max(ss[h], axis=-1, keepdims=True)) for h in hs]
        ps = [jnp.exp(ss[h] - m_new[h]) for h in hs]
        corr = [jnp.exp(state[h][0] - m_new[h]) for h in hs]
        ls = [corr[h] * state[h][1] + jnp.sum(ps[h], axis=-1, keepdims=True) for h in hs]
        pv = [_dot(ps[h].astype(BF16), v_ref[pl.ds(start, tb), lanes[h]]) for h in hs]
        return tuple((m_new[h], ls[h], corr[h] * state[h][2] + pv[h]) for h in hs)

    def worth(j, state):
        jc = jnp.maximum(j, 0)
        gap = bounds[0] - ck_ref[0, jc][:, tb - 1:tb] - state[0][0]
        for h in range(1, heads):
            gap = jnp.maximum(gap, bounds[h] - ck_ref[h, jc][:, tb - 1:tb] - state[h][0])
        return (jnp.max(gap) > -SKIP_LOG).astype(jnp.int32)

    init = tuple((jnp.full((tb, 1), MASK_VALUE, F32), jnp.zeros((tb, 1), F32), jnp.zeros((tb, HEAD_DIM), F32))
                 for _ in range(heads))
    first = block(qi, init, True)

    def cond(c):
        return (c[0] >= 0) & (c[1] > 0)

    def body(c):
        j, _, state = c
        state = block(j, state, False)
        return j - 1, worth(j - 1, state), state

    _, _, final = lax.while_loop(cond, body, (qi - 1, worth(qi - 1, first), first))
    for h in range(heads):
        _, l, acc = final[h]
        o_ref[:, lanes[h]] = (acc / l).astype(o_ref.dtype)


def _fox(u, c, ck):
    s = u.shape[0]
    tb = min(FOX_BLOCK, s)
    nb = s // tb
    col0 = COL_FOX // FOX_W
    resident = lambda blk: pl.BlockSpec((s, FOX_W), lambda i: (0, blk), pipeline_mode=pl.Buffered(1))
    return pl.pallas_call(
        functools.partial(_fox_kernel, scale=HEAD_DIM ** -0.5),
        out_shape=jax.ShapeDtypeStruct((s, FOX_W), BF16),
        grid=(nb,),
        in_specs=[
            pl.BlockSpec((tb, FOX_W), lambda i: (i, col0)),
            resident(col0 + 1),
            resident(col0 + 2),
            pl.BlockSpec((tb, LANES), lambda i: (i, 0)),
            pl.BlockSpec((FOX_HEADS, nb, 1, tb), lambda i: (0, 0, 0, 0)),
        ],
        out_specs=pl.BlockSpec((tb, FOX_W), lambda i: (i, 0)),
        scratch_shapes=[pltpu.VMEM((FOX_HEADS, 1, 1), F32)],
        compiler_params=_params("arbitrary"),
        name="fox_attn",
    )(u, u, u, c, ck)


def _sb_kernel(q_ref, k_ref, v_ref, o_ref, *, scale):
    tb = q_ref.shape[0]
    heads = q_ref.shape[1] // HEAD_DIM
    qi = pl.program_id(0)
    lanes = [slice(h * HEAD_DIM, (h + 1) * HEAD_DIM) for h in range(heads)]
    qs = [q_ref[:, lanes[h]] for h in range(heads)]
    row = lax.broadcasted_iota(jnp.int32, (tb, tb), 0)
    col = lax.broadcasted_iota(jnp.int32, (tb, tb), 1)
    upper = jnp.where(row > col, 1.0, 0.0).astype(BF16)

    def block(j, state, masked):
        start = pl.multiple_of(j * tb, tb)
        hs = range(heads)
        zs = [_dot_nt(qs[h], k_ref[pl.ds(start, tb), lanes[h]]) * scale for h in hs]
        sps = [_softplus(z) for z in zs]
        if masked:
            valid = col < row
            log_1mb = [jnp.where(valid, -sp, 0.0) for sp in sps]
        else:
            log_1mb = [-sp for sp in sps]
        parts = [_split2(x) for x in log_1mb]
        rest = [_dot(hi, upper) + _dot(lo, upper) for hi, lo in parts]
        a = [jnp.exp(zs[h] - sps[h] + rest[h] + state[h][0]) for h in hs]
        if masked:
            a = [jnp.where(valid, x, 0.0) for x in a]
        pv = [_dot(a[h].astype(BF16), v_ref[pl.ds(start, tb), lanes[h]]) for h in hs]
        return tuple((state[h][0] + rest[h][:, :1] + log_1mb[h][:, :1], state[h][1] + pv[h]) for h in hs)

    def worth(state):
        top = state[0][0]
        for h in range(1, heads):
            top = jnp.maximum(top, state[h][0])
        return (jnp.max(top) > -SKIP_LOG).astype(jnp.int32)

    init = tuple((jnp.zeros((tb, 1), F32), jnp.zeros((tb, HEAD_DIM), F32)) for _ in range(heads))
    first = block(qi, init, True)

    def cond(c):
        return (c[0] >= 0) & (c[1] > 0)

    def body(c):
        j, _, state = c
        state = block(j, state, False)
        return j - 1, worth(state), state

    _, _, final = lax.while_loop(cond, body, (qi - 1, worth(first), first))
    for h in range(heads):
        o_ref[:, lanes[h]] = final[h][1].astype(o_ref.dtype)


def _stick_breaking(u):
    s = u.shape[0]
    tb = min(SB_BLOCK, s)
    col0 = COL_SB // SB_W
    resident = lambda blk: pl.BlockSpec((s, SB_W), lambda i: (0, blk), pipeline_mode=pl.Buffered(1))
    return pl.pallas_call(
        functools.partial(_sb_kernel, scale=HEAD_DIM ** -0.5),
        out_shape=jax.ShapeDtypeStruct((s, SB_W), BF16),
        grid=(s // tb,),
        in_specs=[pl.BlockSpec((tb, SB_W), lambda i: (i, col0)), resident(col0 + 1), resident(col0 + 2)],
        out_specs=pl.BlockSpec((tb, SB_W), lambda i: (i, 0)),
        compiler_params=_params("parallel"),
        name="sb_attn",
    )(u, u, u)


def _merge_kernel(x_ref, og_ref, of_ref, os_ref, wg0_ref, wg1_ref, wg2_ref, wbg_ref, wbf_ref, wbs_ref,
                  wo_ref, g_ref, b_ref, o_ref, xb_ref, *, alpha):
    j = pl.program_id(1)

    @pl.when(j == 0)
    def _():
        x = x_ref[...]
        xb_ref[...] = x.astype(BF16)
        o_ref[...] = alpha * x

    xb = xb_ref[...]
    m = _sigmoid(_dot(xb, wg0_ref[...])) * _dot(og_ref[...], wbg_ref[...])
    m += _sigmoid(_dot(xb, wg1_ref[...])) * _dot(of_ref[...], wbf_ref[...])
    m += _sigmoid(_dot(xb, wg2_ref[...])) * _dot(os_ref[...], wbs_ref[...])
    o_ref[...] += _dot(m.astype(BF16), wo_ref[...])

    @pl.when(j == pl.num_programs(1) - 1)
    def _():
        o_ref[...] = _layernorm(o_ref[...], g_ref[...], b_ref[...])


def _merge_ln(x, o_gla, o_fox, o_sb, w_all, w_bg, w_bf, w_bs, w_out, g, b, layer, alpha):
    s, d = x.shape
    tm, tn = min(MERGE_TM, s), min(MERGE_TN, d)
    nj = d // tn
    gate0 = MAIN_W // tn
    row_spec = lambda w: pl.BlockSpec((tm, w), lambda i, j: (i, 0), pipeline_mode=pl.Buffered(1))
    gate_spec = lambda br: pl.BlockSpec((None, d, tn), lambda i, j: (layer, 0, gate0 + br * nj + j))
    col_spec = lambda w: pl.BlockSpec((None, w, tn), lambda i, j: (layer, 0, j))
    vec_spec = pl.BlockSpec((None, 1, d), lambda i, j: (layer, 0, 0))
    return pl.pallas_call(
        functools.partial(_merge_kernel, alpha=alpha),
        out_shape=jax.ShapeDtypeStruct((s, d), F32),
        grid=(s // tm, nj),
        in_specs=[
            row_spec(d), row_spec(GLA_V), row_spec(FOX_W), row_spec(SB_W),
            gate_spec(0), gate_spec(1), gate_spec(2),
            col_spec(GLA_V), col_spec(FOX_W), col_spec(SB_W),
            pl.BlockSpec((None, tn, d), lambda i, j: (layer, j, 0)),
            vec_spec, vec_spec,
        ],
        out_specs=pl.BlockSpec((tm, d), lambda i, j: (i, 0)),
        scratch_shapes=[pltpu.VMEM((tm, d), BF16)],
        compiler_params=_params("parallel", "arbitrary"),
        name="merge_ln",
    )(x, o_gla, o_fox, o_sb, w_all, w_all, w_all, w_bg, w_bf, w_bs, w_out, g[:, None, :], b[:, None, :])


def _mixer_branches(x, w_all, w_tail, layer, gla_w_a2, gla_b_a, gla_norm_g, fox_b_f):
    s, d = x.shape
    u, tail = _proj(x, w_all, w_tail, layer)

    wa_pad = jnp.zeros((LANES, GLA_QK), F32).at[:GLA_RANK].set(gla_w_a2)
    o_gla = _gla(u, tail, wa_pad, gla_b_a, gla_norm_g)

    bf_pad = jnp.zeros((1, LANES), F32).at[0, FF_LANE:FF_LANE + FOX_HEADS].set(fox_b_f)
    c = _fox_cumsum(tail, bf_pad)
    ct = c[:, FF_LANE:FF_LANE + FOX_HEADS].T
    tb = min(FOX_BLOCK, s)
    o_fox = _fox(u, c, ct.reshape(FOX_HEADS, s // tb, 1, tb))
    o_sb = _stick_breaking(u)
    return o_gla, o_fox, o_sb


def _mixer_ln(x, w_all, w_tail, layer, gla_w_a2, gla_b_a, gla_norm_g, fox_b_f, w_bg, w_bf, w_bs, w_out,
              ln_g, ln_b, alpha):
    o_gla, o_fox, o_sb = _mixer_branches(x, w_all, w_tail, layer, gla_w_a2, gla_b_a, gla_norm_g, fox_b_f)
    return _merge_ln(x, o_gla, o_fox, o_sb, w_all, w_bg, w_bf, w_bs, w_out, ln_g, ln_b, layer, alpha)


def kernel(x, ffn1_w_gate, ffn1_w_up, ffn1_w_down, ln1_g, ln1_b, w_in, gla_w_a2, gla_b_a, gla_norm_g, fox_b_f, w_br_gla, w_br_fox, w_br_sb, w_out, ln2_g, ln2_b, ffn2_w_gate, ffn2_w_up, ffn2_w_down, ln3_g, ln3_b):
    batch, seq, d = x.shape
    depth = w_in.shape[0]
    alpha = (2 * depth) ** 0.25
    cast = lambda w: w.astype(BF16)
    ffn1 = (ffn1_w_gate, ffn1_w_up, ffn1_w_down)
    ffn2 = (ffn2_w_gate, ffn2_w_up, ffn2_w_down)
    branch = (cast(w_br_gla), cast(w_br_fox), cast(w_br_sb), cast(w_out))
    assert (N_BRANCH * d) % REGROUP_W == 0 and w_in.shape[2] == MAIN_W + GLA_RANK + FOX_HEADS + N_BRANCH * d
    w_all, w_tail = _regroup(w_in), _tail_weights(w_in)
    outs = []
    for bi in range(batch):
        h = x[bi]
        for l in range(depth):
            h = _ffn_ln(h, *ffn1, ln1_g, ln1_b, l, alpha)
            h = _mixer_ln(h, w_all, w_tail, l, gla_w_a2[l], gla_b_a[l], gla_norm_g[l], fox_b_f[l], *branch,
                          ln2_g, ln2_b, alpha)
            h = _ffn_ln(h, *ffn2, ln3_g, ln3_b, l, alpha)
        outs.append(h)
    return jnp.stack(outs, axis=0)
```

```python
import functools

import jax
import jax.numpy as jnp
from jax import lax
from jax.experimental import pallas as pl
from jax.experimental.pallas import tpu as pltpu

F32 = jnp.float32
BF16 = jnp.bfloat16

LN_EPS = 1e-5
GLA_HEADS = 4
GLA_DK = 64
GLA_DV = 128
GLA_RANK = 16
GLA_TAU = 16.0
GLA_CHUNK = 64
GLA_SUB = 16
GLA_QK = GLA_HEADS * GLA_DK
GLA_V = GLA_HEADS * GLA_DV
FOX_HEADS = 6
SB_HEADS = 6
HEAD_DIM = 128
FOX_W = FOX_HEADS * HEAD_DIM
SB_W = SB_HEADS * HEAD_DIM
N_BRANCH = 3
LANES = 128

REGROUP_W = 256
MAIN_W = 2 * GLA_QK + 2 * GLA_V + 3 * FOX_W + 3 * SB_W
COL_GV = 2 * GLA_QK
COL_GR = COL_GV + GLA_V
COL_FOX = COL_GR + GLA_V
COL_SB = COL_FOX + 3 * FOX_W
FF_LANE = GLA_RANK

FFN_TM = 1024
FFN_TF = 256
PROJ_TM = 2048
PROJ_TN = 512
MERGE_TM = 512
MERGE_TN = 512
GLA_ROWS = 256
CUM_ROWS = 1024
FOX_BLOCK = 256
SB_BLOCK = 128
SB_SUBS = 4
VMEM_LIMIT = 52 * 1024 * 1024
MASK_VALUE = -1e30
SKIP_LOG = 22.0


def _dot(a, b):
    return jnp.dot(a, b, preferred_element_type=F32)


def _dot_nt(a, b):
    return lax.dot_general(a, b, (((1,), (1,)), ((), ())), preferred_element_type=F32)


def _dot_tn(a, b):
    return lax.dot_general(a, b, (((0,), (0,)), ((), ())), preferred_element_type=F32)


def _split2(x):
    hi = x.astype(BF16)
    lo = (x - hi.astype(F32)).astype(BF16)
    return hi, lo


def _split3(x):
    hi = x.astype(BF16)
    r = x - hi.astype(F32)
    mid = r.astype(BF16)
    lo = (r - mid.astype(F32)).astype(BF16)
    return hi, mid, lo


def _softplus(x):
    return jnp.maximum(x, 0.0) + jnp.log(1.0 + jnp.exp(-jnp.abs(x)))


def _sigmoid(x):
    return 1.0 / (1.0 + jnp.exp(-x))


def _layernorm(y, g, b):
    mu = jnp.mean(y, axis=-1, keepdims=True)
    d = y - mu
    var = jnp.mean(d * d, axis=-1, keepdims=True)
    return d * lax.rsqrt(var + LN_EPS) * g + b


def _params(*sem):
    return pltpu.CompilerParams(dimension_semantics=sem, vmem_limit_bytes=VMEM_LIMIT)


def _ffn_kernel(x_ref, wg_ref, wu_ref, wd_ref, g_ref, b_ref, o_ref, xb_ref, *, alpha):
    f = pl.program_id(1)

    @pl.when(f == 0)
    def _():
        x = x_ref[...]
        xb_ref[...] = x.astype(BF16)
        o_ref[...] = (2.0 * alpha) * x

    xb = xb_ref[...]
    gate = _dot(xb, wg_ref[...].astype(BF16))
    up = _dot(xb, wu_ref[...].astype(BF16))
    h = gate * _sigmoid(gate) * up
    o_ref[...] += _dot(h.astype(BF16), wd_ref[...].astype(BF16))

    @pl.when(f == pl.num_programs(1) - 1)
    def _():
        o_ref[...] = _layernorm(0.5 * o_ref[...], g_ref[...], b_ref[...])


def _ffn_ln(x, wg, wu, wd, g, b, layer, alpha):
    s, d = x.shape
    f = wg.shape[2]
    tm, tf = min(FFN_TM, s), min(FFN_TF, f)
    vec_spec = pl.BlockSpec((None, 1, d), lambda i, j: (layer, 0, 0))
    return pl.pallas_call(
        functools.partial(_ffn_kernel, alpha=alpha),
        out_shape=jax.ShapeDtypeStruct((s, d), F32),
        grid=(s // tm, f // tf),
        in_specs=[
            pl.BlockSpec((tm, d), lambda i, j: (i, 0), pipeline_mode=pl.Buffered(1)),
            pl.BlockSpec((None, d, tf), lambda i, j: (layer, 0, j)),
            pl.BlockSpec((None, d, tf), lambda i, j: (layer, 0, j)),
            pl.BlockSpec((None, tf, d), lambda i, j: (layer, j, 0)),
            vec_spec, vec_spec,
        ],
        out_specs=pl.BlockSpec((tm, d), lambda i, j: (i, 0)),
        scratch_shapes=[pltpu.VMEM((tm, d), BF16)],
        compiler_params=_params("parallel", "arbitrary"),
        name="ffn_ln",
    )(x, wg, wu, wd, g[:, None, :], b[:, None, :])


def _column_rows(w_in):
    n_layers, d, cols = w_in.shape
    n_slices = d // LANES
    rows = jnp.transpose(w_in.reshape(n_layers, n_slices, LANES, cols), (3, 1, 0, 2))
    return rows.reshape(cols * n_slices * n_layers, LANES), n_slices


def _read_piece(ref, slice_idx, layer, n_cols, n_slices, n_layers):
    stride = n_slices * n_layers
    return ref[pl.ds(slice_idx * n_layers + layer, n_cols, stride=stride), :].T


def _regroup_kernel(w_ref, o_ref, *, n_slices):
    n_layers, _, tn = o_ref.shape
    for layer in range(n_layers):
        for t in range(n_slices):
            piece = _read_piece(w_ref, t, layer, tn, n_slices, n_layers)
            o_ref[layer, t * LANES:(t + 1) * LANES, :] = piece.astype(o_ref.dtype)


def _regroup(w_in):
    n_layers, d, cols = w_in.shape
    n_out = MAIN_W + N_BRANCH * d
    tn = REGROUP_W
    w_rows, n_slices = _column_rows(w_in)
    per_col = n_slices * n_layers
    n_plain, n_fox = COL_FOX // tn, 3 * FOX_W // tn

    def source(j):
        skipped = jnp.where(j < n_plain, 0, jnp.where(j < n_plain + n_fox, GLA_RANK, GLA_RANK + FOX_HEADS))
        return (j * tn + skipped) * per_col, 0

    return pl.pallas_call(
        functools.partial(_regroup_kernel, n_slices=n_slices),
        out_shape=jax.ShapeDtypeStruct((n_layers, d, n_out), BF16),
        grid=(n_out // tn,),
        in_specs=[pl.BlockSpec((pl.Element(tn * per_col), pl.Element(LANES)), source)],
        out_specs=pl.BlockSpec((n_layers, d, tn), lambda j: (0, 0, j)),
        compiler_params=_params("parallel"),
        name="regroup_w_in",
    )(w_rows)


def _tail_kernel(a_ref, b_ref, o_ref, *, n_slices):
    n_layers = o_ref.shape[0]
    lane = lax.broadcasted_iota(jnp.int32, (LANES, LANES), 1)
    for layer in range(n_layers):
        for t in range(n_slices):
            a = _read_piece(a_ref, t, layer, LANES, n_slices, n_layers)
            b = _read_piece(b_ref, t, layer, LANES, n_slices, n_layers)
            tail = jnp.where(lane < GLA_RANK, a, jnp.where(lane < GLA_RANK + FOX_HEADS, b, 0.0))
            o_ref[layer, t * LANES:(t + 1) * LANES, :] = tail.astype(o_ref.dtype)


def _tail_weights(w_in):
    n_layers, d, _ = w_in.shape
    w_rows, n_slices = _column_rows(w_in)
    blk_rows = LANES * n_slices * n_layers
    ga_blk = COL_FOX // LANES
    ff_blk = (COL_FOX + GLA_RANK + 3 * FOX_W) // LANES
    return pl.pallas_call(
        functools.partial(_tail_kernel, n_slices=n_slices),
        out_shape=jax.ShapeDtypeStruct((n_layers, d, LANES), BF16),
        grid=(1,),
        in_specs=[
            pl.BlockSpec((blk_rows, LANES), lambda i: (ga_blk, 0)),
            pl.BlockSpec((blk_rows, LANES), lambda i: (ff_blk, 0)),
        ],
        out_specs=pl.BlockSpec((n_layers, d, LANES), lambda i: (0, 0, 0)),
        compiler_params=_params("arbitrary"),
        name="tail_w_in",
    )(w_rows, w_rows)


def _proj_kernel(x_ref, w_ref, wt_ref, o_ref, t_ref, xb_ref):
    @pl.when(pl.program_id(1) == 0)
    def _():
        xb_ref[...] = x_ref[...].astype(BF16)
        t_ref[...] = _dot(xb_ref[...], wt_ref[...])

    o_ref[...] = _dot(xb_ref[...], w_ref[...]).astype(o_ref.dtype)


def _proj(x, w_all, w_tail, layer):
    s, d = x.shape
    tm, tn = min(PROJ_TM, s), PROJ_TN
    return pl.pallas_call(
        _proj_kernel,
        out_shape=(jax.ShapeDtypeStruct((s, MAIN_W), BF16), jax.ShapeDtypeStruct((s, LANES), F32)),
        grid=(s // tm, MAIN_W // tn),
        in_specs=[
            pl.BlockSpec((tm, d), lambda i, j: (i, 0), pipeline_mode=pl.Buffered(1)),
            pl.BlockSpec((None, d, tn), lambda i, j: (layer, 0, j)),
            pl.BlockSpec((None, d, LANES), lambda i, j: (layer, 0, 0)),
        ],
        out_specs=(pl.BlockSpec((tm, tn), lambda i, j: (i, j)),
                   pl.BlockSpec((tm, LANES), lambda i, j: (i, 0))),
        scratch_shapes=[pltpu.VMEM((tm, d), BF16)],
        compiler_params=_params("parallel", "arbitrary"),
        name="in_proj",
    )(x, w_all, w_tail)


def _gla_kernel(q_ref, k_ref, v_ref, gr_ref, ga_ref, wa_ref, ba_ref, ng_ref, o_ref, state_ref):
    rows = q_ref.shape[0]
    n_chunks = rows // GLA_CHUNK
    n_sub = GLA_CHUNK // GLA_SUB

    @pl.when(pl.program_id(0) == 0)
    def _():
        state_ref[...] = jnp.zeros_like(state_ref)

    ga_hi, ga_lo = _split2(ga_ref[...])
    wa_hi, wa_lo = _split2(wa_ref[...])
    xa = _dot(ga_hi, wa_hi) + _dot(ga_hi, wa_lo) + _dot(ga_lo, wa_hi) + ba_ref[...]
    la = -_softplus(-xa) * (1.0 / GLA_TAU)

    r_i = lax.broadcasted_iota(jnp.int32, (rows, rows), 0)
    c_i = lax.broadcasted_iota(jnp.int32, (rows, rows), 1)
    chunk_shift = GLA_CHUNK.bit_length() - 1
    same_chunk = (c_i >> chunk_shift) == (r_i >> chunk_shift)
    tri = jnp.where((c_i <= r_i) & same_chunk, 1.0, 0.0).astype(BF16)
    la_hi, la_mid, la_lo = _split3(la)
    b_all = _dot(tri, la_hi) + _dot(tri, la_mid) + _dot(tri, la_lo)

    e_r = lax.broadcasted_iota(jnp.int32, (GLA_QK, GLA_V), 0)
    e_c = lax.broadcasted_iota(jnp.int32, (GLA_QK, GLA_V), 1)
    same_head = (e_r >> GLA_DK.bit_length() - 1) == (e_c >> GLA_DV.bit_length() - 1)
    expand = jnp.where(same_head, 1.0, 0.0).astype(BF16)
    lane_head = lax.broadcasted_iota(jnp.int32, (1, GLA_QK), 1) >> GLA_DK.bit_length() - 1
    head_mask = [jnp.where(lane_head == h, 1.0, 0.0) for h in range(GLA_HEADS)]
    ones_col = jnp.ones((GLA_CHUNK, GLA_DV), BF16)
    t_sub = lax.broadcasted_iota(jnp.int32, (GLA_SUB, GLA_QK), 0)
    s_chunk = lax.broadcasted_iota(jnp.int32, (GLA_CHUNK, GLA_QK), 0)

    def head_rows(x, n):
        return jnp.concatenate(
            [x[h * n:(h + 1) * n, h * GLA_DV:(h + 1) * GLA_DV] for h in range(GLA_HEADS)], axis=1)

    def stack_heads(x):
        return jnp.concatenate([x * head_mask[h] for h in range(GLA_HEADS)], axis=0).astype(BF16)

    chunks = range(n_chunks)
    sls = [slice(c * GLA_CHUNK, (c + 1) * GLA_CHUNK) for c in chunks]
    q = [q_ref[sl, :].astype(F32) * (GLA_DK ** -0.5) for sl in sls]
    v_b = [v_ref[sl, :] for sl in sls]
    k = [k_ref[sl, :].astype(F32) for sl in sls]
    v = [x.astype(F32) for x in v_b]
    b = [b_all[sl, :] for sl in sls]
    b_last = [x[GLA_CHUNK - 1:GLA_CHUNK, :] for x in b]

    kv = [_dot_tn((k[c] * jnp.exp(b_last[c] - b[c])).astype(BF16), v_b[c]) for c in chunks]
    kv = [jnp.concatenate([x[h * GLA_DK:(h + 1) * GLA_DK, h * GLA_DV:(h + 1) * GLA_DV] for h in range(GLA_HEADS)],
                          axis=0) for x in kv]
    decay = [jnp.exp(_dot_tn(la_hi[sl, :], ones_col) + _dot_tn(la_mid[sl, :], ones_col)) for sl in sls]
    states = [state_ref[...]]
    for c in chunks:
        states.append(states[c] * decay[c] + kv[c])
    state_ref[...] = states[n_chunks]

    inter = [_dot(stack_heads(q[c] * jnp.exp(b[c])), states[c].astype(BF16)) for c in chunks]

    subs = [(c, i) for c in chunks for i in range(n_sub)]
    rs = [slice(i * GLA_SUB, (i + 1) * GLA_SUB) for i in range(n_sub)]
    probs = {}
    for c, i in subs:
        q_i, k_i, b_i = q[c][rs[i], :], k[c][rs[i], :], b[c][rs[i], :]
        ps = []
        for s in range(GLA_SUB):
            diff = jnp.where(t_sub >= s, b_i - b_i[s:s + 1, :], -jnp.inf)
            ps.append(q_i * k_i[s:s + 1, :] * jnp.exp(diff))
        probs[c, i] = jnp.concatenate(ps, axis=0).astype(BF16)
    att = {key: _dot(probs[key], expand) for key in subs}
    out = {}
    for c, i in subs:
        v_i = v[c][rs[i], :]
        o_i = att[c, i][0:GLA_SUB, :] * v_i[0:1, :]
        for s in range(1, GLA_SUB):
            o_i += att[c, i][s * GLA_SUB:(s + 1) * GLA_SUB, :] * v_i[s:s + 1, :]
        out[c, i] = o_i

    later = [(c, i) for c, i in subs if i > 0]
    ref_row = {(c, i): b[c][i * GLA_SUB - 1:i * GLA_SUB, :] for c, i in later}
    q_dec = {(c, i): stack_heads(q[c][rs[i], :] * jnp.exp(b[c][rs[i], :] - ref_row[c, i])) for c, i in later}
    k_dec = {(c, i): (k[c] * jnp.exp(jnp.where(s_chunk < i * GLA_SUB, ref_row[c, i] - b[c], -jnp.inf))).astype(BF16)
             for c, i in later}
    att_prev = {key: _dot_nt(q_dec[key], k_dec[key]).astype(BF16) for key in later}
    for c, i in later:
        out[c, i] += head_rows(_dot(att_prev[c, i], v_b[c]), GLA_SUB)

    for c in chunks:
        o_inter = jnp.concatenate(
            [inter[c][h * GLA_CHUNK:(h + 1) * GLA_CHUNK, :] for h in range(GLA_HEADS)], axis=1)
        o_c = jnp.concatenate([out[c, i] for i in range(n_sub)], axis=0) + o_inter
        outs = []
        for h in range(GLA_HEADS):
            o_h = o_c[:, h * GLA_DV:(h + 1) * GLA_DV]
            ms = jnp.mean(o_h * o_h, axis=-1, keepdims=True)
            outs.append(o_h * lax.rsqrt(ms + LN_EPS) * ng_ref[...])
        gr = gr_ref[sls[c], :].astype(F32)
        o_ref[sls[c], :] = (jnp.concatenate(outs, axis=-1) * (gr * _sigmoid(gr))).astype(o_ref.dtype)


def _gla(u, tail, wa_pad, b_a, norm_g):
    s = u.shape[0]
    rows = min(GLA_ROWS, s)
    return pl.pallas_call(
        _gla_kernel,
        out_shape=jax.ShapeDtypeStruct((s, GLA_V), BF16),
        grid=(s // rows,),
        in_specs=[
            pl.BlockSpec((rows, GLA_QK), lambda i: (i, 0)),
            pl.BlockSpec((rows, GLA_QK), lambda i: (i, 1)),
            pl.BlockSpec((rows, GLA_V), lambda i: (i, COL_GV // GLA_V)),
            pl.BlockSpec((rows, GLA_V), lambda i: (i, COL_GR // GLA_V)),
            pl.BlockSpec((rows, LANES), lambda i: (i, 0)),
            pl.BlockSpec((LANES, GLA_QK), lambda i: (0, 0)),
            pl.BlockSpec((1, GLA_QK), lambda i: (0, 0)),
            pl.BlockSpec((1, GLA_DV), lambda i: (0, 0)),
        ],
        out_specs=pl.BlockSpec((rows, GLA_V), lambda i: (i, 0)),
        scratch_shapes=[pltpu.VMEM((GLA_QK, GLA_DV), F32)],
        compiler_params=_params("arbitrary"),
        name="gla",
    )(u, u, u, u, tail, wa_pad, b_a.reshape(1, GLA_QK), norm_g.reshape(1, GLA_DV))


def _cumsum_kernel(ff_ref, bf_ref, c_ref, carry_ref):
    rows = ff_ref.shape[0]

    @pl.when(pl.program_id(0) == 0)
    def _():
        carry_ref[...] = jnp.zeros_like(carry_ref)

    log_f = -_softplus(-(ff_ref[...] + bf_ref[...]))
    r_i = lax.broadcasted_iota(jnp.int32, (rows, rows), 0)
    c_i = lax.broadcasted_iota(jnp.int32, (rows, rows), 1)
    tri = jnp.where(c_i <= r_i, 1.0, 0.0).astype(BF16)
    hi, mid, lo = _split3(log_f)
    c = _dot(tri, hi) + _dot(tri, mid) + _dot(tri, lo) + carry_ref[...]
    c_ref[...] = c
    carry_ref[...] = c[rows - 1:rows, :]


def _fox_cumsum(tail, bf_pad):
    s = tail.shape[0]
    rows = min(CUM_ROWS, s)
    return pl.pallas_call(
        _cumsum_kernel,
        out_shape=jax.ShapeDtypeStruct((s, LANES), F32),
        grid=(s // rows,),
        in_specs=[
            pl.BlockSpec((rows, LANES), lambda i: (i, 0)),
            pl.BlockSpec((1, LANES), lambda i: (0, 0)),
        ],
        out_specs=pl.BlockSpec((rows, LANES), lambda i: (i, 0)),
        scratch_shapes=[pltpu.VMEM((1, LANES), F32)],
        compiler_params=_params("arbitrary"),
        name="fox_cumsum",
    )(tail, bf_pad)


def _fox_kernel(q_ref, k_ref, v_ref, c_ref, ck_ref, o_ref, kmax_ref, *, scale):
    tb = q_ref.shape[0]
    heads = q_ref.shape[1] // HEAD_DIM
    qi = pl.program_id(0)
    lanes = [slice(h * HEAD_DIM, (h + 1) * HEAD_DIM) for h in range(heads)]

    @pl.when(qi == 0)
    def _():
        for h in range(heads):
            kf = k_ref[:, lanes[h]].astype(F32)
            kmax_ref[h] = jnp.sqrt(jnp.max(jnp.sum(kf * kf, axis=-1, keepdims=True), axis=0, keepdims=True))

    qs = [q_ref[:, lanes[h]] for h in range(heads)]
    cqs = [c_ref[:, FF_LANE + h:FF_LANE + h + 1] for h in range(heads)]
    bounds = []
    for h in range(heads):
        qf = qs[h].astype(F32)
        bounds.append(jnp.sqrt(jnp.sum(qf * qf, axis=-1, keepdims=True)) * kmax_ref[h] * scale + cqs[h])
    row = lax.broadcasted_iota(jnp.int32, (tb, tb), 0)
    col = lax.broadcasted_iota(jnp.int32, (tb, tb), 1)

    def block(j, state, masked):
        start = pl.multiple_of(j * tb, tb)
        hs = range(heads)
        ss = [_dot_nt(qs[h], k_ref[pl.ds(start, tb), lanes[h]]) for h in hs]
        ss = [ss[h] * scale + cqs[h] - ck_ref[h, j] for h in hs]
        if masked:
            ss = [jnp.where(col <= row, s, MASK_VALUE) for s in ss]
        m_new = [jnp.maximum(state[h][0], jnp.max(ss[h], axis=-1, keepdims=True)) for h in hs]
        ps = [jnp.exp(ss[h] - m_new[h]) for h in hs]
        corr = [jnp.exp(state[h][0] - m_new[h]) for h in hs]
        ls = [corr[h] * state[h][1] + jnp.sum(ps[h], axis=-1, keepdims=True) for h in hs]
        pv = [_dot(ps[h].astype(BF16), v_ref[pl.ds(start, tb), lanes[h]]) for h in hs]
        return tuple((m_new[h], ls[h], corr[h] * state[h][2] + pv[h]) for h in hs)

    def worth(j, state):
        jc = jnp.maximum(j, 0)
        gap = bounds[0] - ck_ref[0, jc][:, tb - 1:tb] - state[0][0]
        for h in range(1, heads):
            gap = jnp.maximum(gap, bounds[h] - ck_ref[h, jc][:, tb - 1:tb] - state[h][0])
        return (jnp.max(gap) > -SKIP_LOG).astype(jnp.int32)

    init = tuple((jnp.full((tb, 1), MASK_VALUE, F32), jnp.zeros((tb, 1), F32), jnp.zeros((tb, HEAD_DIM), F32))
                 for _ in range(heads))
    first = block(qi, init, True)

    def cond(c):
        return (c[0] >= 0) & (c[1] > 0)

    def body(c):
        j, _, state = c
        state = block(j, state, False)
        return j - 1, worth(j - 1, state), state

    _, _, final = lax.while_loop(cond, body, (qi - 1, worth(qi - 1, first), first))
    for h in range(heads):
        _, l, acc = final[h]
        o_ref[:, lanes[h]] = (acc / l).astype(o_ref.dtype)


def _fox(u, c, ck):
    s = u.shape[0]
    tb = min(FOX_BLOCK, s)
    nb = s // tb
    col0 = COL_FOX // FOX_W
    resident = lambda blk: pl.BlockSpec((s, FOX_W), lambda i: (0, blk), pipeline_mode=pl.Buffered(1))
    return pl.pallas_call(
        functools.partial(_fox_kernel, scale=HEAD_DIM ** -0.5),
        out_shape=jax.ShapeDtypeStruct((s, FOX_W), BF16),
        grid=(nb,),
        in_specs=[
            pl.BlockSpec((tb, FOX_W), lambda i: (i, col0)),
            resident(col0 + 1),
            resident(col0 + 2),
            pl.BlockSpec((tb, LANES), lambda i: (i, 0)),
            pl.BlockSpec((FOX_HEADS, nb, 1, tb), lambda i: (0, 0, 0, 0)),
        ],
        out_specs=pl.BlockSpec((tb, FOX_W), lambda i: (i, 0)),
        scratch_shapes=[pltpu.VMEM((FOX_HEADS, 1, 1), F32)],
        compiler_params=_params("arbitrary"),
        name="fox_attn",
    )(u, u, u, c, ck)


def _sb_kernel(q_ref, k_ref, v_ref, o_ref, *, scale, tb):
    n_sub = q_ref.shape[0] // tb
    heads = q_ref.shape[1] // HEAD_DIM
    first_block = pl.program_id(0) * n_sub
    lanes = [slice(h * HEAD_DIM, (h + 1) * HEAD_DIM) for h in range(heads)]
    units = [(u, h) for u in range(n_sub) for h in range(heads)]
    qs = {(u, h): q_ref[u * tb:(u + 1) * tb, lanes[h]] for u, h in units}
    row = lax.broadcasted_iota(jnp.int32, (tb, tb), 0)
    col = lax.broadcasted_iota(jnp.int32, (tb, tb), 1)
    upper = jnp.where(row > col, 1.0, 0.0).astype(BF16)

    def block(j0, state, masked):
        js = [j0 + u for u in range(n_sub)]
        starts = [pl.multiple_of(jnp.maximum(j, 0) * tb, tb) for j in js]
        keys = {(u, h): k_ref[pl.ds(starts[u], tb), lanes[h]] for u, h in units}
        zs = {key: _dot_nt(qs[key], keys[key]) * scale for key in units}
        sps = {key: _softplus(zs[key]) for key in units}
        if masked:
            valid = col < row
            log_1mb = {key: jnp.where(valid, -sps[key], 0.0) for key in units}
            carry = {key: state[n][0] for n, key in enumerate(units)}
        else:
            log_1mb = {key: -sps[key] for key in units}
            carry = {key: jnp.where(js[key[0]] >= 0, state[n][0], MASK_VALUE) for n, key in enumerate(units)}
        parts = {key: _split2(log_1mb[key]) for key in units}
        rest = {key: _dot(parts[key][0], upper) + _dot(parts[key][1], upper) for key in units}
        a = {key: jnp.exp(zs[key] - sps[key] + rest[key] + carry[key]) for key in units}
        if masked:
            a = {key: jnp.where(valid, a[key], 0.0) for key in units}
        pv = {(u, h): _dot(a[u, h].astype(BF16), v_ref[pl.ds(starts[u], tb), lanes[h]]) for u, h in units}
        return tuple((carry[key] + rest[key][:, :1] + log_1mb[key][:, :1], state[n][1] + pv[key])
                     for n, key in enumerate(units))

    def worth(state):
        top = state[0][0]
        for n in range(1, len(units)):
            top = jnp.maximum(top, state[n][0])
        return (jnp.max(top) > -SKIP_LOG).astype(jnp.int32)

    init = tuple((jnp.zeros((tb, 1), F32), jnp.zeros((tb, HEAD_DIM), F32)) for _ in units)
    first = block(first_block, init, True)

    def cond(c):
        return (c[0] + n_sub - 1 >= 0) & (c[1] > 0)

    def body(c):
        j0, _, state = c
        state = block(j0, state, False)
        return j0 - 1, worth(state), state

    _, _, final = lax.while_loop(cond, body, (first_block - 1, worth(first), first))
    for n, (u, h) in enumerate(units):
        o_ref[u * tb:(u + 1) * tb, lanes[h]] = final[n][1].astype(o_ref.dtype)


def _stick_breaking(u):
    s = u.shape[0]
    tb = min(SB_BLOCK, s)
    rows = min(SB_BLOCK * SB_SUBS, s)
    col0 = COL_SB // SB_W
    resident = lambda blk: pl.BlockSpec((s, SB_W), lambda i: (0, blk), pipeline_mode=pl.Buffered(1))
    return pl.pallas_call(
        functools.partial(_sb_kernel, scale=HEAD_DIM ** -0.5, tb=tb),
        out_shape=jax.ShapeDtypeStruct((s, SB_W), BF16),
        grid=(s // rows,),
        in_specs=[pl.BlockSpec((rows, SB_W), lambda i: (i, col0)), resident(col0 + 1), resident(col0 + 2)],
        out_specs=pl.BlockSpec((rows, SB_W), lambda i: (i, 0)),
        compiler_params=_params("parallel"),
        name="sb_attn",
    )(u, u, u)


def _merge_kernel(x_ref, og_ref, of_ref, os_ref, wg0_ref, wg1_ref, wg2_ref, wbg_ref, wbf_ref, wbs_ref,
                  wo_ref, g_ref, b_ref, o_ref, xb_ref, *, alpha):
    j = pl.program_id(1)

    @pl.when(j == 0)
    def _():
        x = x_ref[...]
        xb_ref[...] = x.astype(BF16)
        o_ref[...] = alpha * x

    xb = xb_ref[...]
    m = _sigmoid(_dot(xb, wg0_ref[...])) * _dot(og_ref[...], wbg_ref[...])
    m += _sigmoid(_dot(xb, wg1_ref[...])) * _dot(of_ref[...], wbf_ref[...])
    m += _sigmoid(_dot(xb, wg2_ref[...])) * _dot(os_ref[...], wbs_ref[...])
    o_ref[...] += _dot(m.astype(BF16), wo_ref[...])

    @pl.when(j == pl.num_programs(1) - 1)
    def _():
        o_ref[...] = _layernorm(o_ref[...], g_ref[...], b_ref[...])


def _merge_ln(x, o_gla, o_fox, o_sb, w_all, w_bg, w_bf, w_bs, w_out, g, b, layer, alpha):
    s, d = x.shape
    tm, tn = min(MERGE_TM, s), min(MERGE_TN, d)
    nj = d // tn
    gate0 = MAIN_W // tn
    row_spec = lambda w: pl.BlockSpec((tm, w), lambda i, j: (i, 0), pipeline_mode=pl.Buffered(1))
    gate_spec = lambda br: pl.BlockSpec((None, d, tn), lambda i, j: (layer, 0, gate0 + br * nj + j))
    col_spec = lambda w: pl.BlockSpec((None, w, tn), lambda i, j: (layer, 0, j))
    vec_spec = pl.BlockSpec((None, 1, d), lambda i, j: (layer, 0, 0))
    return pl.pallas_call(
        functools.partial(_merge_kernel, alpha=alpha),
        out_shape=jax.ShapeDtypeStruct((s, d), F32),
        grid=(s // tm, nj),
        in_specs=[
            row_spec(d), row_spec(GLA_V), row_spec(FOX_W), row_spec(SB_W),
            gate_spec(0), gate_spec(1), gate_spec(2),
            col_spec(GLA_V), col_spec(FOX_W), col_spec(SB_W),
            pl.BlockSpec((None, tn, d), lambda i, j: (layer, j, 0)),
            vec_spec, vec_spec,
        ],
        out_specs=pl.BlockSpec((tm, d), lambda i, j: (i, 0)),
        scratch_shapes=[pltpu.VMEM((tm, d), BF16)],
        compiler_params=_params("parallel", "arbitrary"),
        name="merge_ln",
    )(x, o_gla, o_fox, o_sb, w_all, w_all, w_all, w_bg, w_bf, w_bs, w_out, g[:, None, :], b[:, None, :])


def _mixer_branches(x, w_all, w_tail, layer, gla_w_a2, gla_b_a, gla_norm_g, fox_b_f):
    s, d = x.shape
    u, tail = _proj(x, w_all, w_tail, layer)

    wa_pad = jnp.zeros((LANES, GLA_QK), F32).at[:GLA_RANK].set(gla_w_a2)
    o_gla = _gla(u, tail, wa_pad, gla_b_a, gla_norm_g)

    bf_pad = jnp.zeros((1, LANES), F32).at[0, FF_LANE:FF_LANE + FOX_HEADS].set(fox_b_f)
    c = _fox_cumsum(tail, bf_pad)
    ct = c[:, FF_LANE:FF_LANE + FOX_HEADS].T
    tb = min(FOX_BLOCK, s)
    o_fox = _fox(u, c, ct.reshape(FOX_HEADS, s // tb, 1, tb))
    o_sb = _stick_breaking(u)
    return o_gla, o_fox, o_sb


def _mixer_ln(x, w_all, w_tail, layer, gla_w_a2, gla_b_a, gla_norm_g, fox_b_f, w_bg, w_bf, w_bs, w_out,
              ln_g, ln_b, alpha):
    o_gla, o_fox, o_sb = _mixer_branches(x, w_all, w_tail, layer, gla_w_a2, gla_b_a, gla_norm_g, fox_b_f)
    return _merge_ln(x, o_gla, o_fox, o_sb, w_all, w_bg, w_bf, w_bs, w_out, ln_g, ln_b, layer, alpha)


def kernel(x, ffn1_w_gate, ffn1_w_up, ffn1_w_down, ln1_g, ln1_b, w_in, gla_w_a2, gla_b_a, gla_norm_g, fox_b_f, w_br_gla, w_br_fox, w_br_sb, w_out, ln2_g, ln2_b, ffn2_w_gate, ffn2_w_up, ffn2_w_down, ln3_g, ln3_b):
    batch, seq, d = x.shape
    depth = w_in.shape[0]
    alpha = (2 * depth) ** 0.25
    cast = lambda w: w.astype(BF16)
    ffn1 = (ffn1_w_gate, ffn1_w_up, ffn1_w_down)
    ffn2 = (ffn2_w_gate, ffn2_w_up, ffn2_w_down)
    branch = (cast(w_br_gla), cast(w_br_fox), cast(w_br_sb), cast(w_out))
    assert (N_BRANCH * d) % REGROUP_W == 0 and w_in.shape[2] == MAIN_W + GLA_RANK + FOX_HEADS + N_BRANCH * d
    w_all, w_tail = _regroup(w_in), _tail_weights(w_in)
    outs = []
    for bi in range(batch):
        h = x[bi]
        for l in range(depth):
            h = _ffn_ln(h, *ffn1, ln1_g, ln1_b, l, alpha)
            h = _mixer_ln(h, w_all, w_tail, l, gla_w_a2[l], gla_b_a[l], gla_norm_g[l], fox_b_f[l], *branch,
                          ln2_g, ln2_b, alpha)
            h = _ffn_ln(h, *ffn2, ln3_g, ln3_b, l, alpha)
        outs.append(h)
    return jnp.stack(outs, axis=0)
```

```python
import functools

import jax
import jax.numpy as jnp
from jax import lax
from jax.experimental import pallas as pl
from jax.experimental.pallas import tpu as pltpu

F32 = jnp.float32
BF16 = jnp.bfloat16

LN_EPS = 1e-5
GLA_HEADS = 4
GLA_DK = 64
GLA_DV = 128
GLA_RANK = 16
GLA_TAU = 16.0
GLA_CHUNK = 64
GLA_SUB = 16
GLA_QK = GLA_HEADS * GLA_DK
GLA_V = GLA_HEADS * GLA_DV
FOX_HEADS = 6
SB_HEADS = 6
HEAD_DIM = 128
FOX_W = FOX_HEADS * HEAD_DIM
SB_W = SB_HEADS * HEAD_DIM
N_BRANCH = 3
LANES = 128

REGROUP_W = 256
MAIN_W = 2 * GLA_QK + 2 * GLA_V + 3 * FOX_W + 3 * SB_W
COL_GV = 2 * GLA_QK
COL_GR = COL_GV + GLA_V
COL_FOX = COL_GR + GLA_V
COL_SB = COL_FOX + 3 * FOX_W
FF_LANE = GLA_RANK

FFN_TM = 1024
FFN_TF = 256
PROJ_TM = 2048
PROJ_TN = 512
MERGE_TM = 512
MERGE_TN = 512
GLA_ROWS = 256
CUM_ROWS = 1024
FOX_BLOCK = 256
SB_BLOCK = 128
SB_SUBS = 4
VMEM_LIMIT = 52 * 1024 * 1024
MASK_VALUE = -1e30
SKIP_LOG = 22.0


def _dot(a, b):
    return jnp.dot(a, b, preferred_element_type=F32)


def _dot_nt(a, b):
    return lax.dot_general(a, b, (((1,), (1,)), ((), ())), preferred_element_type=F32)


def _dot_tn(a, b):
    return lax.dot_general(a, b, (((0,), (0,)), ((), ())), preferred_element_type=F32)


def _split2(x):
    hi = x.astype(BF16)
    lo = (x - hi.astype(F32)).astype(BF16)
    return hi, lo


def _split3(x):
    hi = x.astype(BF16)
    r = x - hi.astype(F32)
    mid = r.astype(BF16)
    lo = (r - mid.astype(F32)).astype(BF16)
    return hi, mid, lo


def _softplus(x):
    return jnp.maximum(x, 0.0) + jnp.log(1.0 + jnp.exp(-jnp.abs(x)))


def _sigmoid(x):
    return 1.0 / (1.0 + jnp.exp(-x))


def _layernorm(y, g, b):
    mu = jnp.mean(y, axis=-1, keepdims=True)
    d = y - mu
    var = jnp.mean(d * d, axis=-1, keepdims=True)
    return d * lax.rsqrt(var + LN_EPS) * g + b


def _params(*sem):
    return pltpu.CompilerParams(dimension_semantics=sem, vmem_limit_bytes=VMEM_LIMIT)


def _ffn_kernel(x_ref, wg_ref, wu_ref, wd_ref, g_ref, b_ref, o_ref, xb_ref, *, alpha):
    f = pl.program_id(1)

    @pl.when(f == 0)
    def _():
        x = x_ref[...]
        xb_ref[...] = x.astype(BF16)
        o_ref[...] = (2.0 * alpha) * x

    xb = xb_ref[...]
    gate = _dot(xb, wg_ref[...].astype(BF16))
    up = _dot(xb, wu_ref[...].astype(BF16))
    h = gate * _sigmoid(gate) * up
    o_ref[...] += _dot(h.astype(BF16), wd_ref[...].astype(BF16))

    @pl.when(f == pl.num_programs(1) - 1)
    def _():
        o_ref[...] = _layernorm(0.5 * o_ref[...], g_ref[...], b_ref[...])


def _ffn_ln(x, wg, wu, wd, g, b, layer, alpha):
    s, d = x.shape
    f = wg.shape[2]
    tm, tf = min(FFN_TM, s), min(FFN_TF, f)
    vec_spec = pl.BlockSpec((None, 1, d), lambda i, j: (layer, 0, 0))
    return pl.pallas_call(
        functools.partial(_ffn_kernel, alpha=alpha),
        out_shape=jax.ShapeDtypeStruct((s, d), F32),
        grid=(s // tm, f // tf),
        in_specs=[
            pl.BlockSpec((tm, d), lambda i, j: (i, 0), pipeline_mode=pl.Buffered(1)),
            pl.BlockSpec((None, d, tf), lambda i, j: (layer, 0, j)),
            pl.BlockSpec((None, d, tf), lambda i, j: (layer, 0, j)),
            pl.BlockSpec((None, tf, d), lambda i, j: (layer, j, 0)),
            vec_spec, vec_spec,
        ],
        out_specs=pl.BlockSpec((tm, d), lambda i, j: (i, 0)),
        scratch_shapes=[pltpu.VMEM((tm, d), BF16)],
        compiler_params=_params("parallel", "arbitrary"),
        name="ffn_ln",
    )(x, wg, wu, wd, g[:, None, :], b[:, None, :])


def _column_rows(w_in):
    n_layers, d, cols = w_in.shape
    n_slices = d // LANES
    rows = jnp.transpose(w_in.reshape(n_layers, n_slices, LANES, cols), (3, 1, 0, 2))
    return rows.reshape(cols * n_slices * n_layers, LANES), n_slices


def _read_piece(ref, slice_idx, layer, n_cols, n_slices, n_layers):
    stride = n_slices * n_layers
    return ref[pl.ds(slice_idx * n_layers + layer, n_cols, stride=stride), :].T


def _regroup_kernel(w_ref, o_ref, *, n_slices):
    n_layers, _, tn = o_ref.shape
    for layer in range(n_layers):
        for t in range(n_slices):
            piece = _read_piece(w_ref, t, layer, tn, n_slices, n_layers)
            o_ref[layer, t * LANES:(t + 1) * LANES, :] = piece.astype(o_ref.dtype)


def _regroup(w_in):
    n_layers, d, cols = w_in.shape
    n_out = MAIN_W + N_BRANCH * d
    tn = REGROUP_W
    w_rows, n_slices = _column_rows(w_in)
    per_col = n_slices * n_layers
    n_plain, n_fox = COL_FOX // tn, 3 * FOX_W // tn

    def source(j):
        skipped = jnp.where(j < n_plain, 0, jnp.where(j < n_plain + n_fox, GLA_RANK, GLA_RANK + FOX_HEADS))
        return (j * tn + skipped) * per_col, 0

    return pl.pallas_call(
        functools.partial(_regroup_kernel, n_slices=n_slices),
        out_shape=jax.ShapeDtypeStruct((n_layers, d, n_out), BF16),
        grid=(n_out // tn,),
        in_specs=[pl.BlockSpec((pl.Element(tn * per_col), pl.Element(LANES)), source)],
        out_specs=pl.BlockSpec((n_layers, d, tn), lambda j: (0, 0, j)),
        compiler_params=_params("parallel"),
        name="regroup_w_in",
    )(w_rows)


def _tail_kernel(a_ref, b_ref, o_ref, *, n_slices):
    n_layers = o_ref.shape[0]
    lane = lax.broadcasted_iota(jnp.int32, (LANES, LANES), 1)
    for layer in range(n_layers):
        for t in range(n_slices):
            a = _read_piece(a_ref, t, layer, LANES, n_slices, n_layers)
            b = _read_piece(b_ref, t, layer, LANES, n_slices, n_layers)
            tail = jnp.where(lane < GLA_RANK, a, jnp.where(lane < GLA_RANK + FOX_HEADS, b, 0.0))
            o_ref[layer, t * LANES:(t + 1) * LANES, :] = tail.astype(o_ref.dtype)


def _tail_weights(w_in):
    n_layers, d, _ = w_in.shape
    w_rows, n_slices = _column_rows(w_in)
    blk_rows = LANES * n_slices * n_layers
    ga_blk = COL_FOX // LANES
    ff_blk = (COL_FOX + GLA_RANK + 3 * FOX_W) // LANES
    return pl.pallas_call(
        functools.partial(_tail_kernel, n_slices=n_slices),
        out_shape=jax.ShapeDtypeStruct((n_layers, d, LANES), BF16),
        grid=(1,),
        in_specs=[
            pl.BlockSpec((blk_rows, LANES), lambda i: (ga_blk, 0)),
            pl.BlockSpec((blk_rows, LANES), lambda i: (ff_blk, 0)),
        ],
        out_specs=pl.BlockSpec((n_layers, d, LANES), lambda i: (0, 0, 0)),
        compiler_params=_params("arbitrary"),
        name="tail_w_in",
    )(w_rows, w_rows)


def _proj_kernel(x_ref, w_ref, wt_ref, o_ref, t_ref, xb_ref):
    @pl.when(pl.program_id(1) == 0)
    def _():
        xb_ref[...] = x_ref[...].astype(BF16)
        t_ref[...] = _dot(xb_ref[...], wt_ref[...])

    o_ref[...] = _dot(xb_ref[...], w_ref[...]).astype(o_ref.dtype)


def _proj(x, w_all, w_tail, layer):
    s, d = x.shape
    tm, tn = min(PROJ_TM, s), PROJ_TN
    return pl.pallas_call(
        _proj_kernel,
        out_shape=(jax.ShapeDtypeStruct((s, MAIN_W), BF16), jax.ShapeDtypeStruct((s, LANES), F32)),
        grid=(s // tm, MAIN_W // tn),
        in_specs=[
            pl.BlockSpec((tm, d), lambda i, j: (i, 0), pipeline_mode=pl.Buffered(1)),
            pl.BlockSpec((None, d, tn), lambda i, j: (layer, 0, j)),
            pl.BlockSpec((None, d, LANES), lambda i, j: (layer, 0, 0)),
        ],
        out_specs=(pl.BlockSpec((tm, tn), lambda i, j: (i, j)),
                   pl.BlockSpec((tm, LANES), lambda i, j: (i, 0))),
        scratch_shapes=[pltpu.VMEM((tm, d), BF16)],
        compiler_params=_params("parallel", "arbitrary"),
        name="in_proj",
    )(x, w_all, w_tail)


def _gla_kernel(q_ref, k_ref, v_ref, gr_ref, ga_ref, wa_ref, ba_ref, ng_ref, o_ref, state_ref):
    rows = q_ref.shape[0]
    n_chunks = rows // GLA_CHUNK
    n_sub = GLA_CHUNK // GLA_SUB

    @pl.when(pl.program_id(0) == 0)
    def _():
        state_ref[...] = jnp.zeros_like(state_ref)

    ga_hi, ga_lo = _split2(ga_ref[...])
    wa_hi, wa_lo = _split2(wa_ref[...])
    xa = _dot(ga_hi, wa_hi) + _dot(ga_hi, wa_lo) + _dot(ga_lo, wa_hi) + ba_ref[...]
    la = -_softplus(-xa) * (1.0 / GLA_TAU)

    r_i = lax.broadcasted_iota(jnp.int32, (rows, rows), 0)
    c_i = lax.broadcasted_iota(jnp.int32, (rows, rows), 1)
    chunk_shift = GLA_CHUNK.bit_length() - 1
    same_chunk = (c_i >> chunk_shift) == (r_i >> chunk_shift)
    tri = jnp.where((c_i <= r_i) & same_chunk, 1.0, 0.0).astype(BF16)
    la_hi, la_mid, la_lo = _split3(la)
    b_all = _dot(tri, la_hi) + _dot(tri, la_mid) + _dot(tri, la_lo)

    e_r = lax.broadcasted_iota(jnp.int32, (GLA_QK, GLA_V), 0)
    e_c = lax.broadcasted_iota(jnp.int32, (GLA_QK, GLA_V), 1)
    same_head = (e_r >> GLA_DK.bit_length() - 1) == (e_c >> GLA_DV.bit_length() - 1)
    expand = jnp.where(same_head, 1.0, 0.0).astype(BF16)
    lane_head = lax.broadcasted_iota(jnp.int32, (1, GLA_QK), 1) >> GLA_DK.bit_length() - 1
    head_mask = [jnp.where(lane_head == h, 1.0, 0.0) for h in range(GLA_HEADS)]
    ones_col = jnp.ones((GLA_CHUNK, GLA_DV), BF16)
    t_sub = lax.broadcasted_iota(jnp.int32, (GLA_SUB, GLA_QK), 0)
    s_chunk = lax.broadcasted_iota(jnp.int32, (GLA_CHUNK, GLA_QK), 0)

    def head_rows(x, n):
        return jnp.concatenate(
            [x[h * n:(h + 1) * n, h * GLA_DV:(h + 1) * GLA_DV] for h in range(GLA_HEADS)], axis=1)

    def stack_heads(x):
        return jnp.concatenate([x * head_mask[h] for h in range(GLA_HEADS)], axis=0).astype(BF16)

    chunks = range(n_chunks)
    sls = [slice(c * GLA_CHUNK, (c + 1) * GLA_CHUNK) for c in chunks]
    q = [q_ref[sl, :].astype(F32) * (GLA_DK ** -0.5) for sl in sls]
    v_b = [v_ref[sl, :] for sl in sls]
    k = [k_ref[sl, :].astype(F32) for sl in sls]
    v = [x.astype(F32) for x in v_b]
    b = [b_all[sl, :] for sl in sls]
    b_last = [x[GLA_CHUNK - 1:GLA_CHUNK, :] for x in b]

    kv = [_dot_tn((k[c] * jnp.exp(b_last[c] - b[c])).astype(BF16), v_b[c]) for c in chunks]
    kv = [jnp.concatenate([x[h * GLA_DK:(h + 1) * GLA_DK, h * GLA_DV:(h + 1) * GLA_DV] for h in range(GLA_HEADS)],
                          axis=0) for x in kv]
    decay = [jnp.exp(_dot_tn(la_hi[sl, :], ones_col) + _dot_tn(la_mid[sl, :], ones_col)) for sl in sls]
    states = [state_ref[...]]
    for c in chunks:
        states.append(states[c] * decay[c] + kv[c])
    state_ref[...] = states[n_chunks]

    inter = [_dot(stack_heads(q[c] * jnp.exp(b[c])), states[c].astype(BF16)) for c in chunks]

    subs = [(c, i) for c in chunks for i in range(n_sub)]
    rs = [slice(i * GLA_SUB, (i + 1) * GLA_SUB) for i in range(n_sub)]
    probs = {}
    for c, i in subs:
        q_i, k_i, b_i = q[c][rs[i], :], k[c][rs[i], :], b[c][rs[i], :]
        ps = []
        for s in range(GLA_SUB):
            diff = jnp.where(t_sub >= s, b_i - b_i[s:s + 1, :], -jnp.inf)
            ps.append(q_i * k_i[s:s + 1, :] * jnp.exp(diff))
        probs[c, i] = jnp.concatenate(ps, axis=0).astype(BF16)
    att = {key: _dot(probs[key], expand) for key in subs}
    out = {}
    for c, i in subs:
        v_i = v[c][rs[i], :]
        o_i = att[c, i][0:GLA_SUB, :] * v_i[0:1, :]
        for s in range(1, GLA_SUB):
            o_i += att[c, i][s * GLA_SUB:(s + 1) * GLA_SUB, :] * v_i[s:s + 1, :]
        out[c, i] = o_i

    later = [(c, i) for c, i in subs if i > 0]
    ref_row = {(c, i): b[c][i * GLA_SUB - 1:i * GLA_SUB, :] for c, i in later}
    q_dec = {(c, i): stack_heads(q[c][rs[i], :] * jnp.exp(b[c][rs[i], :] - ref_row[c, i])) for c, i in later}
    k_dec = {(c, i): (k[c] * jnp.exp(jnp.where(s_chunk < i * GLA_SUB, ref_row[c, i] - b[c], -jnp.inf))).astype(BF16)
             for c, i in later}
    att_prev = {key: _dot_nt(q_dec[key], k_dec[key]).astype(BF16) for key in later}
    for c, i in later:
        out[c, i] += head_rows(_dot(att_prev[c, i], v_b[c]), GLA_SUB)

    for c in chunks:
        o_inter = jnp.concatenate(
            [inter[c][h * GLA_CHUNK:(h + 1) * GLA_CHUNK, :] for h in range(GLA_HEADS)], axis=1)
        o_c = jnp.concatenate([out[c, i] for i in range(n_sub)], axis=0) + o_inter
        outs = []
        for h in range(GLA_HEADS):
            o_h = o_c[:, h * GLA_DV:(h + 1) * GLA_DV]
            ms = jnp.mean(o_h * o_h, axis=-1, keepdims=True)
            outs.append(o_h * lax.rsqrt(ms + LN_EPS) * ng_ref[...])
        gr = gr_ref[sls[c], :].astype(F32)
        o_ref[sls[c], :] = (jnp.concatenate(outs, axis=-1) * (gr * _sigmoid(gr))).astype(o_ref.dtype)


def _gla(u, tail, wa_pad, b_a, norm_g):
    s = u.shape[0]
    rows = min(GLA_ROWS, s)
    return pl.pallas_call(
        _gla_kernel,
        out_shape=jax.ShapeDtypeStruct((s, GLA_V), BF16),
        grid=(s // rows,),
        in_specs=[
            pl.BlockSpec((rows, GLA_QK), lambda i: (i, 0)),
            pl.BlockSpec((rows, GLA_QK), lambda i: (i, 1)),
            pl.BlockSpec((rows, GLA_V), lambda i: (i, COL_GV // GLA_V)),
            pl.BlockSpec((rows, GLA_V), lambda i: (i, COL_GR // GLA_V)),
            pl.BlockSpec((rows, LANES), lambda i: (i, 0)),
            pl.BlockSpec((LANES, GLA_QK), lambda i: (0, 0)),
            pl.BlockSpec((1, GLA_QK), lambda i: (0, 0)),
            pl.BlockSpec((1, GLA_DV), lambda i: (0, 0)),
        ],
        out_specs=pl.BlockSpec((rows, GLA_V), lambda i: (i, 0)),
        scratch_shapes=[pltpu.VMEM((GLA_QK, GLA_DV), F32)],
        compiler_params=_params("arbitrary"),
        name="gla",
    )(u, u, u, u, tail, wa_pad, b_a.reshape(1, GLA_QK), norm_g.reshape(1, GLA_DV))


def _cumsum_kernel(ff_ref, bf_ref, c_ref, carry_ref):
    rows = ff_ref.shape[0]

    @pl.when(pl.program_id(0) == 0)
    def _():
        carry_ref[...] = jnp.zeros_like(carry_ref)

    log_f = -_softplus(-(ff_ref[...] + bf_ref[...]))
    r_i = lax.broadcasted_iota(jnp.int32, (rows, rows), 0)
    c_i = lax.broadcasted_iota(jnp.int32, (rows, rows), 1)
    tri = jnp.where(c_i <= r_i, 1.0, 0.0).astype(BF16)
    hi, mid, lo = _split3(log_f)
    c = _dot(tri, hi) + _dot(tri, mid) + _dot(tri, lo) + carry_ref[...]
    c_ref[...] = c
    carry_ref[...] = c[rows - 1:rows, :]


def _fox_cumsum(tail, bf_pad):
    s = tail.shape[0]
    rows = min(CUM_ROWS, s)
    return pl.pallas_call(
        _cumsum_kernel,
        out_shape=jax.ShapeDtypeStruct((s, LANES), F32),
        grid=(s // rows,),
        in_specs=[
            pl.BlockSpec((rows, LANES), lambda i: (i, 0)),
            pl.BlockSpec((1, LANES), lambda i: (0, 0)),
        ],
        out_specs=pl.BlockSpec((rows, LANES), lambda i: (i, 0)),
        scratch_shapes=[pltpu.VMEM((1, LANES), F32)],
        compiler_params=_params("arbitrary"),
        name="fox_cumsum",
    )(tail, bf_pad)


def _fox_kernel(q_ref, k_ref, v_ref, c_ref, ck_ref, o_ref, kmax_ref, *, scale):
    tb = q_ref.shape[0]
    heads = q_ref.shape[1] // HEAD_DIM
    qi = pl.program_id(0)
    lanes = [slice(h * HEAD_DIM, (h + 1) * HEAD_DIM) for h in range(heads)]

    @pl.when(qi == 0)
    def _():
        for h in range(heads):
            kf = k_ref[:, lanes[h]].astype(F32)
            kmax_ref[h] = jnp.sqrt(jnp.max(jnp.sum(kf * kf, axis=-1, keepdims=True), axis=0, keepdims=True))

    qs = [q_ref[:, lanes[h]] for h in range(heads)]
    cqs = [c_ref[:, FF_LANE + h:FF_LANE + h + 1] for h in range(heads)]
    bounds = []
    for h in range(heads):
        qf = qs[h].astype(F32)
        bounds.append(jnp.sqrt(jnp.sum(qf * qf, axis=-1, keepdims=True)) * kmax_ref[h] * scale + cqs[h])
    row = lax.broadcasted_iota(jnp.int32, (tb, tb), 0)
    col = lax.broadcasted_iota(jnp.int32, (tb, tb), 1)

    def block(j, state, masked):
        start = pl.multiple_of(j * tb, tb)
        hs = range(heads)
        ss = [_dot_nt(qs[h], k_ref[pl.ds(start, tb), lanes[h]]) for h in hs]
        ss = [ss[h] * scale + cqs[h] - ck_ref[h, j] for h in hs]
        if masked:
            ss = [jnp.where(col <= row, s, MASK_VALUE) for s in ss]
        m_new = [jnp.maximum(state[h][0], jnp.max(ss[h], axis=-1, keepdims=True)) for h in hs]
        ps = [jnp.exp(ss[h] - m_new[h]) for h in hs]
        corr = [jnp.exp(state[h][0] - m_new[h]) for h in hs]
        ls = [corr[h] * state[h][1] + jnp.sum(ps[h], axis=-1, keepdims=True) for h in hs]
        pv = [_dot(ps[h].astype(BF16), v_ref[pl.ds(start, tb), lanes[h]]) for h in hs]
        return tuple((m_new[h], ls[h], corr[h] * state[h][2] + pv[h]) for h in hs)

    def worth(j, state):
        jc = jnp.maximum(j, 0)
        gap = bounds[0] - ck_ref[0, jc][:, tb - 1:tb] - state[0][0]
        for h in range(1, heads):
            gap = jnp.maximum(gap, bounds[h] - ck_ref[h, jc][:, tb - 1:tb] - state[h][0])
        return (jnp.max(gap) > -SKIP_LOG).astype(jnp.int32)

    init = tuple((jnp.full((tb, 1), MASK_VALUE, F32), jnp.zeros((tb, 1), F32), jnp.zeros((tb, HEAD_DIM), F32))
                 for _ in range(heads))
    first = block(qi, init, True)

    def cond(c):
        return (c[0] >= 0) & (c[1] > 0)

    def body(c):
        j, _, state = c
        state = block(j, state, False)
        return j - 1, worth(j - 1, state), state

    _, _, final = lax.while_loop(cond, body, (qi - 1, worth(qi - 1, first), first))
    for h in range(heads):
        _, l, acc = final[h]
        o_ref[:, lanes[h]] = (acc / l).astype(o_ref.dtype)


def _fox(u, c, ck):
    s = u.shape[0]
    tb = min(FOX_BLOCK, s)
    nb = s // tb
    col0 = COL_FOX // FOX_W
    resident = lambda blk: pl.BlockSpec((s, FOX_W), lambda i: (0, blk), pipeline_mode=pl.Buffered(1))
    return pl.pallas_call(
        functools.partial(_fox_kernel, scale=HEAD_DIM ** -0.5),
        out_shape=jax.ShapeDtypeStruct((s, FOX_W), BF16),
        grid=(nb,),
        in_specs=[
            pl.BlockSpec((tb, FOX_W), lambda i: (i, col0)),
            resident(col0 + 1),
            resident(col0 + 2),
            pl.BlockSpec((tb, LANES), lambda i: (i, 0)),
            pl.BlockSpec((FOX_HEADS, nb, 1, tb), lambda i: (0, 0, 0, 0)),
        ],
        out_specs=pl.BlockSpec((tb, FOX_W), lambda i: (i, 0)),
        scratch_shapes=[pltpu.VMEM((FOX_HEADS, 1, 1), F32)],
        compiler_params=_params("arbitrary"),
        name="fox_attn",
    )(u, u, u, c, ck)


def _sb_kernel(q_ref, k_ref, v_ref, o_ref, *, scale, tb):
    n_sub = q_ref.shape[0] // tb
    heads = q_ref.shape[1] // HEAD_DIM
    first_block = pl.program_id(0) * n_sub
    lanes = [slice(h * HEAD_DIM, (h + 1) * HEAD_DIM) for h in range(heads)]
    units = [(u, h) for u in range(n_sub) for h in range(heads)]
    qs = {(u, h): q_ref[u * tb:(u + 1) * tb, lanes[h]] for u, h in units}
    row = lax.broadcasted_iota(jnp.int32, (tb, tb), 0)
    col = lax.broadcasted_iota(jnp.int32, (tb, tb), 1)
    upper = jnp.where(row > col, 1.0, 0.0).astype(BF16)

    def block(j0, state, masked):
        js = [j0 + u for u in range(n_sub)]
        starts = [pl.multiple_of(jnp.maximum(j, 0) * tb, tb) for j in js]
        keys = {(u, h): k_ref[pl.ds(starts[u], tb), lanes[h]] for u, h in units}
        zs = {key: _dot_nt(qs[key], keys[key]) * scale for key in units}
        sps = {key: _softplus(zs[key]) for key in units}
        if masked:
            valid = col < row
            log_1mb = {key: jnp.where(valid, -sps[key], 0.0) for key in units}
            carry = {key: state[n][0] for n, key in enumerate(units)}
        else:
            log_1mb = {key: -sps[key] for key in units}
            carry = {key: jnp.where(js[key[0]] >= 0, state[n][0], MASK_VALUE) for n, key in enumerate(units)}
        parts = {key: _split2(log_1mb[key]) for key in units}
        rest = {key: _dot(parts[key][0], upper) + _dot(parts[key][1], upper) for key in units}
        a = {key: jnp.exp(zs[key] - sps[key] + rest[key] + carry[key]) for key in units}
        if masked:
            a = {key: jnp.where(valid, a[key], 0.0) for key in units}
        pv = {(u, h): _dot(a[u, h].astype(BF16), v_ref[pl.ds(starts[u], tb), lanes[h]]) for u, h in units}
        return tuple((carry[key] + rest[key][:, :1] + log_1mb[key][:, :1], state[n][1] + pv[key])
                     for n, key in enumerate(units))

    def worth(state):
        top = state[0][0]
        for n in range(1, len(units)):
            top = jnp.maximum(top, state[n][0])
        return (jnp.max(top) > -SKIP_LOG).astype(jnp.int32)

    init = tuple((jnp.zeros((tb, 1), F32), jnp.zeros((tb, HEAD_DIM), F32)) for _ in units)
    first = block(first_block, init, True)

    def cond(c):
        return (c[0] + n_sub - 1 >= 0) & (c[1] > 0)

    def body(c):
        j0, _, state = c
        state = block(j0, state, False)
        return j0 - 1, worth(state), state

    _, _, final = lax.while_loop(cond, body, (first_block - 1, worth(first), first))
    for n, (u, h) in enumerate(units):
        o_ref[u * tb:(u + 1) * tb, lanes[h]] = final[n][1].astype(o_ref.dtype)


def _stick_breaking(u):
    s = u.shape[0]
    tb = min(SB_BLOCK, s)
    rows = min(SB_BLOCK * SB_SUBS, s)
    col0 = COL_SB // SB_W
    resident = lambda blk: pl.BlockSpec((s, SB_W), lambda i: (0, blk), pipeline_mode=pl.Buffered(1))
    return pl.pallas_call(
        functools.partial(_sb_kernel, scale=HEAD_DIM ** -0.5, tb=tb),
        out_shape=jax.ShapeDtypeStruct((s, SB_W), BF16),
        grid=(s // rows,),
        in_specs=[pl.BlockSpec((rows, SB_W), lambda i: (i, col0)), resident(col0 + 1), resident(col0 + 2)],
        out_specs=pl.BlockSpec((rows, SB_W), lambda i: (i, 0)),
        compiler_params=_params("parallel"),
        name="sb_attn",
    )(u, u, u)


def _merge_kernel(x_ref, og_ref, of_ref, os_ref, wg0_ref, wg1_ref, wg2_ref, wbg_ref, wbf_ref, wbs_ref,
                  wo_ref, g_ref, b_ref, o_ref, xb_ref, *, alpha):
    j = pl.program_id(1)

    @pl.when(j == 0)
    def _():
        x = x_ref[...]
        xb_ref[...] = x.astype(BF16)
        o_ref[...] = alpha * x

    xb = xb_ref[...]
    m = _sigmoid(_dot(xb, wg0_ref[...])) * _dot(og_ref[...], wbg_ref[...])
    m += _sigmoid(_dot(xb, wg1_ref[...])) * _dot(of_ref[...], wbf_ref[...])
    m += _sigmoid(_dot(xb, wg2_ref[...])) * _dot(os_ref[...], wbs_ref[...])
    o_ref[...] += _dot(m.astype(BF16), wo_ref[...])

    @pl.when(j == pl.num_programs(1) - 1)
    def _():
        o_ref[...] = _layernorm(o_ref[...], g_ref[...], b_ref[...])


def _merge_ln(x, o_gla, o_fox, o_sb, w_all, w_bg, w_bf, w_bs, w_out, g, b, layer, alpha):
    s, d = x.shape
    tm, tn = min(MERGE_TM, s), min(MERGE_TN, d)
    nj = d // tn
    gate0 = MAIN_W // tn
    row_spec = lambda w: pl.BlockSpec((tm, w), lambda i, j: (i, 0))
    gate_spec = lambda br: pl.BlockSpec((None, d, tn), lambda i, j: (layer, 0, gate0 + br * nj + j))
    col_spec = lambda w: pl.BlockSpec((None, w, tn), lambda i, j: (layer, 0, j))
    vec_spec = pl.BlockSpec((None, 1, d), lambda i, j: (layer, 0, 0))
    return pl.pallas_call(
        functools.partial(_merge_kernel, alpha=alpha),
        out_shape=jax.ShapeDtypeStruct((s, d), F32),
        grid=(s // tm, nj),
        in_specs=[
            row_spec(d), row_spec(GLA_V), row_spec(FOX_W), row_spec(SB_W),
            gate_spec(0), gate_spec(1), gate_spec(2),
            col_spec(GLA_V), col_spec(FOX_W), col_spec(SB_W),
            pl.BlockSpec((None, tn, d), lambda i, j: (layer, j, 0)),
            vec_spec, vec_spec,
        ],
        out_specs=pl.BlockSpec((tm, d), lambda i, j: (i, 0)),
        scratch_shapes=[pltpu.VMEM((tm, d), BF16)],
        compiler_params=_params("parallel", "arbitrary"),
        name="merge_ln",
    )(x, o_gla, o_fox, o_sb, w_all, w_all, w_all, w_bg, w_bf, w_bs, w_out, g[:, None, :], b[:, None, :])


def _mixer_branches(x, w_all, w_tail, layer, gla_w_a2, gla_b_a, gla_norm_g, fox_b_f):
    s, d = x.shape
    u, tail = _proj(x, w_all, w_tail, layer)

    wa_pad = jnp.zeros((LANES, GLA_QK), F32).at[:GLA_RANK].set(gla_w_a2)
    o_gla = _gla(u, tail, wa_pad, gla_b_a, gla_norm_g)

    bf_pad = jnp.zeros((1, LANES), F32).at[0, FF_LANE:FF_LANE + FOX_HEADS].set(fox_b_f)
    c = _fox_cumsum(tail, bf_pad)
    ct = c[:, FF_LANE:FF_LANE + FOX_HEADS].T
    tb = min(FOX_BLOCK, s)
    o_fox = _fox(u, c, ct.reshape(FOX_HEADS, s // tb, 1, tb))
    o_sb = _stick_breaking(u)
    return o_gla, o_fox, o_sb


def _mixer_ln(x, w_all, w_tail, layer, gla_w_a2, gla_b_a, gla_norm_g, fox_b_f, w_bg, w_bf, w_bs, w_out,
              ln_g, ln_b, alpha):
    o_gla, o_fox, o_sb = _mixer_branches(x, w_all, w_tail, layer, gla_w_a2, gla_b_a, gla_norm_g, fox_b_f)
    return _merge_ln(x, o_gla, o_fox, o_sb, w_all, w_bg, w_bf, w_bs, w_out, ln_g, ln_b, layer, alpha)


def kernel(x, ffn1_w_gate, ffn1_w_up, ffn1_w_down, ln1_g, ln1_b, w_in, gla_w_a2, gla_b_a, gla_norm_g, fox_b_f, w_br_gla, w_br_fox, w_br_sb, w_out, ln2_g, ln2_b, ffn2_w_gate, ffn2_w_up, ffn2_w_down, ln3_g, ln3_b):
    batch, seq, d = x.shape
    depth = w_in.shape[0]
    alpha = (2 * depth) ** 0.25
    cast = lambda w: w.astype(BF16)
    ffn1 = (ffn1_w_gate, ffn1_w_up, ffn1_w_down)
    ffn2 = (ffn2_w_gate, ffn2_w_up, ffn2_w_down)
    branch = (cast(w_br_gla), cast(w_br_fox), cast(w_br_sb), cast(w_out))
    assert (N_BRANCH * d) % REGROUP_W == 0 and w_in.shape[2] == MAIN_W + GLA_RANK + FOX_HEADS + N_BRANCH * d
    w_all, w_tail = _regroup(w_in), _tail_weights(w_in)
    outs = []
    for bi in range(batch):
        h = x[bi]
        for l in range(depth):
            h = _ffn_ln(h, *ffn1, ln1_g, ln1_b, l, alpha)
            h = _mixer_ln(h, w_all, w_tail, l, gla_w_a2[l], gla_b_a[l], gla_norm_g[l], fox_b_f[l], *branch,
                          ln2_g, ln2_b, alpha)
            h = _ffn_ln(h, *ffn2, ln3_g, ln3_b, l, alpha)
        outs.append(h)
    return jnp.stack(outs, axis=0)
```

```python
import functools

import jax
import jax.numpy as jnp
from jax import lax
from jax.experimental import pallas as pl
from jax.experimental.pallas import tpu as pltpu

F32 = jnp.float32
BF16 = jnp.bfloat16

LN_EPS = 1e-5
GLA_HEADS = 4
GLA_DK = 64
GLA_DV = 128
GLA_RANK = 16
GLA_TAU = 16.0
GLA_CHUNK = 64
GLA_SUB = 16
GLA_QK = GLA_HEADS * GLA_DK
GLA_V = GLA_HEADS * GLA_DV
FOX_HEADS = 6
SB_HEADS = 6
HEAD_DIM = 128
FOX_W = FOX_HEADS * HEAD_DIM
SB_W = SB_HEADS * HEAD_DIM
N_BRANCH = 3
LANES = 128

REGROUP_W = 256
MAIN_W = 2 * GLA_QK + 2 * GLA_V + 3 * FOX_W + 3 * SB_W
COL_GV = 2 * GLA_QK
COL_GR = COL_GV + GLA_V
COL_FOX = COL_GR + GLA_V
COL_SB = COL_FOX + 3 * FOX_W
FF_LANE = GLA_RANK

FFN_TM = 1024
FFN_TF = 256
PROJ_TM = 1024
PROJ_TN = 1024
MERGE_TM = 512
MERGE_TN = 512
GLA_ROWS = 256
CUM_ROWS = 1024
FOX_BLOCK = 256
SB_BLOCK = 128
SB_SUBS = 4
VMEM_LIMIT = 52 * 1024 * 1024
MASK_VALUE = -1e30
SKIP_LOG = 22.0


def _dot(a, b):
    return jnp.dot(a, b, preferred_element_type=F32)


def _dot_nt(a, b):
    return lax.dot_general(a, b, (((1,), (1,)), ((), ())), preferred_element_type=F32)


def _dot_tn(a, b):
    return lax.dot_general(a, b, (((0,), (0,)), ((), ())), preferred_element_type=F32)


def _split2(x):
    hi = x.astype(BF16)
    lo = (x - hi.astype(F32)).astype(BF16)
    return hi, lo


def _split3(x):
    hi = x.astype(BF16)
    r = x - hi.astype(F32)
    mid = r.astype(BF16)
    lo = (r - mid.astype(F32)).astype(BF16)
    return hi, mid, lo


def _softplus(x):
    return jnp.maximum(x, 0.0) + jnp.log(1.0 + jnp.exp(-jnp.abs(x)))


def _sigmoid(x):
    return 1.0 / (1.0 + jnp.exp(-x))


def _layernorm(y, g, b):
    mu = jnp.mean(y, axis=-1, keepdims=True)
    d = y - mu
    var = jnp.mean(d * d, axis=-1, keepdims=True)
    return d * lax.rsqrt(var + LN_EPS) * g + b


def _params(*sem):
    return pltpu.CompilerParams(dimension_semantics=sem, vmem_limit_bytes=VMEM_LIMIT)


def _ffn_kernel(x_ref, wg_ref, wu_ref, wd_ref, g_ref, b_ref, o_ref, xb_ref, *, alpha):
    f = pl.program_id(1)

    @pl.when(f == 0)
    def _():
        x = x_ref[...]
        xb_ref[...] = x.astype(BF16)
        o_ref[...] = (2.0 * alpha) * x

    xb = xb_ref[...]
    gate = _dot(xb, wg_ref[...].astype(BF16))
    up = _dot(xb, wu_ref[...].astype(BF16))
    h = gate * _sigmoid(gate) * up
    o_ref[...] += _dot(h.astype(BF16), wd_ref[...].astype(BF16))

    @pl.when(f == pl.num_programs(1) - 1)
    def _():
        o_ref[...] = _layernorm(0.5 * o_ref[...], g_ref[...], b_ref[...])


def _ffn_ln(x, wg, wu, wd, g, b, layer, alpha):
    s, d = x.shape
    f = wg.shape[2]
    tm, tf = min(FFN_TM, s), min(FFN_TF, f)
    vec_spec = pl.BlockSpec((None, 1, d), lambda i, j: (layer, 0, 0))
    return pl.pallas_call(
        functools.partial(_ffn_kernel, alpha=alpha),
        out_shape=jax.ShapeDtypeStruct((s, d), F32),
        grid=(s // tm, f // tf),
        in_specs=[
            pl.BlockSpec((tm, d), lambda i, j: (i, 0), pipeline_mode=pl.Buffered(1)),
            pl.BlockSpec((None, d, tf), lambda i, j: (layer, 0, j)),
            pl.BlockSpec((None, d, tf), lambda i, j: (layer, 0, j)),
            pl.BlockSpec((None, tf, d), lambda i, j: (layer, j, 0)),
            vec_spec, vec_spec,
        ],
        out_specs=pl.BlockSpec((tm, d), lambda i, j: (i, 0)),
        scratch_shapes=[pltpu.VMEM((tm, d), BF16)],
        compiler_params=_params("parallel", "arbitrary"),
        name="ffn_ln",
    )(x, wg, wu, wd, g[:, None, :], b[:, None, :])


def _column_rows(w_in):
    n_layers, d, cols = w_in.shape
    n_slices = d // LANES
    rows = jnp.transpose(w_in.reshape(n_layers, n_slices, LANES, cols), (3, 1, 0, 2))
    return rows.reshape(cols * n_slices * n_layers, LANES), n_slices


def _read_piece(ref, slice_idx, layer, n_cols, n_slices, n_layers):
    stride = n_slices * n_layers
    return ref[pl.ds(slice_idx * n_layers + layer, n_cols, stride=stride), :].T


def _regroup_kernel(w_ref, o_ref, *, n_slices):
    n_layers, _, tn = o_ref.shape
    for layer in range(n_layers):
        for t in range(n_slices):
            piece = _read_piece(w_ref, t, layer, tn, n_slices, n_layers)
            o_ref[layer, t * LANES:(t + 1) * LANES, :] = piece.astype(o_ref.dtype)


def _regroup(w_in):
    n_layers, d, cols = w_in.shape
    n_out = MAIN_W + N_BRANCH * d
    tn = REGROUP_W
    w_rows, n_slices = _column_rows(w_in)
    per_col = n_slices * n_layers
    n_plain, n_fox = COL_FOX // tn, 3 * FOX_W // tn

    def source(j):
        skipped = jnp.where(j < n_plain, 0, jnp.where(j < n_plain + n_fox, GLA_RANK, GLA_RANK + FOX_HEADS))
        return (j * tn + skipped) * per_col, 0

    return pl.pallas_call(
        functools.partial(_regroup_kernel, n_slices=n_slices),
        out_shape=jax.ShapeDtypeStruct((n_layers, d, n_out), BF16),
        grid=(n_out // tn,),
        in_specs=[pl.BlockSpec((pl.Element(tn * per_col), pl.Element(LANES)), source)],
        out_specs=pl.BlockSpec((n_layers, d, tn), lambda j: (0, 0, j)),
        compiler_params=_params("parallel"),
        name="regroup_w_in",
    )(w_rows)


def _tail_kernel(a_ref, b_ref, o_ref, *, n_slices):
    n_layers = o_ref.shape[0]
    lane = lax.broadcasted_iota(jnp.int32, (LANES, LANES), 1)
    for layer in range(n_layers):
        for t in range(n_slices):
            a = _read_piece(a_ref, t, layer, LANES, n_slices, n_layers)
            b = _read_piece(b_ref, t, layer, LANES, n_slices, n_layers)
            tail = jnp.where(lane < GLA_RANK, a, jnp.where(lane < GLA_RANK + FOX_HEADS, b, 0.0))
            o_ref[layer, t * LANES:(t + 1) * LANES, :] = tail.astype(o_ref.dtype)


def _tail_weights(w_in):
    n_layers, d, _ = w_in.shape
    w_rows, n_slices = _column_rows(w_in)
    blk_rows = LANES * n_slices * n_layers
    ga_blk = COL_FOX // LANES
    ff_blk = (COL_FOX + GLA_RANK + 3 * FOX_W) // LANES
    return pl.pallas_call(
        functools.partial(_tail_kernel, n_slices=n_slices),
        out_shape=jax.ShapeDtypeStruct((n_layers, d, LANES), BF16),
        grid=(1,),
        in_specs=[
            pl.BlockSpec((blk_rows, LANES), lambda i: (ga_blk, 0)),
            pl.BlockSpec((blk_rows, LANES), lambda i: (ff_blk, 0)),
        ],
        out_specs=pl.BlockSpec((n_layers, d, LANES), lambda i: (0, 0, 0)),
        compiler_params=_params("arbitrary"),
        name="tail_w_in",
    )(w_rows, w_rows)


def _proj_kernel(x_ref, w_ref, wt_ref, o_ref, t_ref, xb_ref):
    @pl.when(pl.program_id(1) == 0)
    def _():
        xb_ref[...] = x_ref[...].astype(BF16)
        t_ref[...] = _dot(xb_ref[...], wt_ref[...])

    o_ref[...] = _dot(xb_ref[...], w_ref[...]).astype(o_ref.dtype)


def _proj(x, w_all, w_tail, layer):
    s, d = x.shape
    tm, tn = min(PROJ_TM, s), PROJ_TN
    return pl.pallas_call(
        _proj_kernel,
        out_shape=(jax.ShapeDtypeStruct((s, MAIN_W), BF16), jax.ShapeDtypeStruct((s, LANES), F32)),
        grid=(s // tm, MAIN_W // tn),
        in_specs=[
            pl.BlockSpec((tm, d), lambda i, j: (i, 0)),
            pl.BlockSpec((None, d, tn), lambda i, j: (layer, 0, j)),
            pl.BlockSpec((None, d, LANES), lambda i, j: (layer, 0, 0)),
        ],
        out_specs=(pl.BlockSpec((tm, tn), lambda i, j: (i, j)),
                   pl.BlockSpec((tm, LANES), lambda i, j: (i, 0))),
        scratch_shapes=[pltpu.VMEM((tm, d), BF16)],
        compiler_params=_params("parallel", "arbitrary"),
        name="in_proj",
    )(x, w_all, w_tail)


def _gla_kernel(q_ref, k_ref, v_ref, gr_ref, ga_ref, wa_ref, ba_ref, ng_ref, o_ref, state_ref):
    rows = q_ref.shape[0]
    n_chunks = rows // GLA_CHUNK
    n_sub = GLA_CHUNK // GLA_SUB

    @pl.when(pl.program_id(0) == 0)
    def _():
        state_ref[...] = jnp.zeros_like(state_ref)

    ga_hi, ga_lo = _split2(ga_ref[...])
    wa_hi, wa_lo = _split2(wa_ref[...])
    xa = _dot(ga_hi, wa_hi) + _dot(ga_hi, wa_lo) + _dot(ga_lo, wa_hi) + ba_ref[...]
    la = -_softplus(-xa) * (1.0 / GLA_TAU)

    r_i = lax.broadcasted_iota(jnp.int32, (rows, rows), 0)
    c_i = lax.broadcasted_iota(jnp.int32, (rows, rows), 1)
    chunk_shift = GLA_CHUNK.bit_length() - 1
    same_chunk = (c_i >> chunk_shift) == (r_i >> chunk_shift)
    tri = jnp.where((c_i <= r_i) & same_chunk, 1.0, 0.0).astype(BF16)
    la_hi, la_mid, la_lo = _split3(la)
    b_all = _dot(tri, la_hi) + _dot(tri, la_mid) + _dot(tri, la_lo)

    e_r = lax.broadcasted_iota(jnp.int32, (GLA_QK, GLA_V), 0)
    e_c = lax.broadcasted_iota(jnp.int32, (GLA_QK, GLA_V), 1)
    same_head = (e_r >> GLA_DK.bit_length() - 1) == (e_c >> GLA_DV.bit_length() - 1)
    expand = jnp.where(same_head, 1.0, 0.0).astype(BF16)
    lane_head = lax.broadcasted_iota(jnp.int32, (1, GLA_QK), 1) >> GLA_DK.bit_length() - 1
    head_mask = [jnp.where(lane_head == h, 1.0, 0.0) for h in range(GLA_HEADS)]
    ones_col = jnp.ones((GLA_CHUNK, GLA_DV), BF16)
    t_sub = lax.broadcasted_iota(jnp.int32, (GLA_SUB, GLA_QK), 0)
    s_chunk = lax.broadcasted_iota(jnp.int32, (GLA_CHUNK, GLA_QK), 0)

    def head_rows(x, n):
        return jnp.concatenate(
            [x[h * n:(h + 1) * n, h * GLA_DV:(h + 1) * GLA_DV] for h in range(GLA_HEADS)], axis=1)

    def stack_heads(x):
        return jnp.concatenate([x * head_mask[h] for h in range(GLA_HEADS)], axis=0).astype(BF16)

    chunks = range(n_chunks)
    sls = [slice(c * GLA_CHUNK, (c + 1) * GLA_CHUNK) for c in chunks]
    q = [q_ref[sl, :].astype(F32) * (GLA_DK ** -0.5) for sl in sls]
    v_b = [v_ref[sl, :] for sl in sls]
    k = [k_ref[sl, :].astype(F32) for sl in sls]
    v = [x.astype(F32) for x in v_b]
    b = [b_all[sl, :] for sl in sls]
    b_last = [x[GLA_CHUNK - 1:GLA_CHUNK, :] for x in b]

    kv = [_dot_tn((k[c] * jnp.exp(b_last[c] - b[c])).astype(BF16), v_b[c]) for c in chunks]
    kv = [jnp.concatenate([x[h * GLA_DK:(h + 1) * GLA_DK, h * GLA_DV:(h + 1) * GLA_DV] for h in range(GLA_HEADS)],
                          axis=0) for x in kv]
    decay = [jnp.exp(_dot_tn(la_hi[sl, :], ones_col) + _dot_tn(la_mid[sl, :], ones_col)) for sl in sls]
    states = [state_ref[...]]
    for c in chunks:
        states.append(states[c] * decay[c] + kv[c])
    state_ref[...] = states[n_chunks]

    inter = [_dot(stack_heads(q[c] * jnp.exp(b[c])), states[c].astype(BF16)) for c in chunks]

    subs = [(c, i) for c in chunks for i in range(n_sub)]
    rs = [slice(i * GLA_SUB, (i + 1) * GLA_SUB) for i in range(n_sub)]
    probs = {}
    for c, i in subs:
        q_i, k_i, b_i = q[c][rs[i], :], k[c][rs[i], :], b[c][rs[i], :]
        ps = []
        for s in range(GLA_SUB):
            diff = jnp.where(t_sub >= s, b_i - b_i[s:s + 1, :], -jnp.inf)
            ps.append(q_i * k_i[s:s + 1, :] * jnp.exp(diff))
        probs[c, i] = jnp.concatenate(ps, axis=0).astype(BF16)
    att = {key: _dot(probs[key], expand) for key in subs}
    out = {}
    for c, i in subs:
        v_i = v[c][rs[i], :]
        o_i = att[c, i][0:GLA_SUB, :] * v_i[0:1, :]
        for s in range(1, GLA_SUB):
            o_i += att[c, i][s * GLA_SUB:(s + 1) * GLA_SUB, :] * v_i[s:s + 1, :]
        out[c, i] = o_i

    later = [(c, i) for c, i in subs if i > 0]
    ref_row = {(c, i): b[c][i * GLA_SUB - 1:i * GLA_SUB, :] for c, i in later}
    q_dec = {(c, i): stack_heads(q[c][rs[i], :] * jnp.exp(b[c][rs[i], :] - ref_row[c, i])) for c, i in later}
    k_dec = {(c, i): (k[c] * jnp.exp(jnp.where(s_chunk < i * GLA_SUB, ref_row[c, i] - b[c], -jnp.inf))).astype(BF16)
             for c, i in later}
    att_prev = {key: _dot_nt(q_dec[key], k_dec[key]).astype(BF16) for key in later}
    for c, i in later:
        out[c, i] += head_rows(_dot(att_prev[c, i], v_b[c]), GLA_SUB)

    for c in chunks:
        o_inter = jnp.concatenate(
            [inter[c][h * GLA_CHUNK:(h + 1) * GLA_CHUNK, :] for h in range(GLA_HEADS)], axis=1)
        o_c = jnp.concatenate([out[c, i] for i in range(n_sub)], axis=0) + o_inter
        outs = []
        for h in range(GLA_HEADS):
            o_h = o_c[:, h * GLA_DV:(h + 1) * GLA_DV]
            ms = jnp.mean(o_h * o_h, axis=-1, keepdims=True)
            outs.append(o_h * lax.rsqrt(ms + LN_EPS) * ng_ref[...])
        gr = gr_ref[sls[c], :].astype(F32)
        o_ref[sls[c], :] = (jnp.concatenate(outs, axis=-1) * (gr * _sigmoid(gr))).astype(o_ref.dtype)


def _gla(u, tail, wa_pad, b_a, norm_g):
    s = u.shape[0]
    rows = min(GLA_ROWS, s)
    return pl.pallas_call(
        _gla_kernel,
        out_shape=jax.ShapeDtypeStruct((s, GLA_V), BF16),
        grid=(s // rows,),
        in_specs=[
            pl.BlockSpec((rows, GLA_QK), lambda i: (i, 0)),
            pl.BlockSpec((rows, GLA_QK), lambda i: (i, 1)),
            pl.BlockSpec((rows, GLA_V), lambda i: (i, COL_GV // GLA_V)),
            pl.BlockSpec((rows, GLA_V), lambda i: (i, COL_GR // GLA_V)),
            pl.BlockSpec((rows, LANES), lambda i: (i, 0)),
            pl.BlockSpec((LANES, GLA_QK), lambda i: (0, 0)),
            pl.BlockSpec((1, GLA_QK), lambda i: (0, 0)),
            pl.BlockSpec((1, GLA_DV), lambda i: (0, 0)),
        ],
        out_specs=pl.BlockSpec((rows, GLA_V), lambda i: (i, 0)),
        scratch_shapes=[pltpu.VMEM((GLA_QK, GLA_DV), F32)],
        compiler_params=_params("arbitrary"),
        name="gla",
    )(u, u, u, u, tail, wa_pad, b_a.reshape(1, GLA_QK), norm_g.reshape(1, GLA_DV))


def _cumsum_kernel(ff_ref, bf_ref, c_ref, carry_ref):
    rows = ff_ref.shape[0]

    @pl.when(pl.program_id(0) == 0)
    def _():
        carry_ref[...] = jnp.zeros_like(carry_ref)

    log_f = -_softplus(-(ff_ref[...] + bf_ref[...]))
    r_i = lax.broadcasted_iota(jnp.int32, (rows, rows), 0)
    c_i = lax.broadcasted_iota(jnp.int32, (rows, rows), 1)
    tri = jnp.where(c_i <= r_i, 1.0, 0.0).astype(BF16)
    hi, mid, lo = _split3(log_f)
    c = _dot(tri, hi) + _dot(tri, mid) + _dot(tri, lo) + carry_ref[...]
    c_ref[...] = c
    carry_ref[...] = c[rows - 1:rows, :]


def _fox_cumsum(tail, bf_pad):
    s = tail.shape[0]
    rows = min(CUM_ROWS, s)
    return pl.pallas_call(
        _cumsum_kernel,
        out_shape=jax.ShapeDtypeStruct((s, LANES), F32),
        grid=(s // rows,),
        in_specs=[
            pl.BlockSpec((rows, LANES), lambda i: (i, 0)),
            pl.BlockSpec((1, LANES), lambda i: (0, 0)),
        ],
        out_specs=pl.BlockSpec((rows, LANES), lambda i: (i, 0)),
        scratch_shapes=[pltpu.VMEM((1, LANES), F32)],
        compiler_params=_params("arbitrary"),
        name="fox_cumsum",
    )(tail, bf_pad)


def _fox_kernel(q_ref, k_ref, v_ref, c_ref, ck_ref, o_ref, kmax_ref, *, scale):
    tb = q_ref.shape[0]
    heads = q_ref.shape[1] // HEAD_DIM
    qi = pl.program_id(0)
    lanes = [slice(h * HEAD_DIM, (h + 1) * HEAD_DIM) for h in range(heads)]

    @pl.when(qi == 0)
    def _():
        for h in range(heads):
            kf = k_ref[:, lanes[h]].astype(F32)
            kmax_ref[h] = jnp.sqrt(jnp.max(jnp.sum(kf * kf, axis=-1, keepdims=True), axis=0, keepdims=True))

    qs = [q_ref[:, lanes[h]] for h in range(heads)]
    cqs = [c_ref[:, FF_LANE + h:FF_LANE + h + 1] for h in range(heads)]
    bounds = []
    for h in range(heads):
        qf = qs[h].astype(F32)
        bounds.append(jnp.sqrt(jnp.sum(qf * qf, axis=-1, keepdims=True)) * kmax_ref[h] * scale + cqs[h])
    row = lax.broadcasted_iota(jnp.int32, (tb, tb), 0)
    col = lax.broadcasted_iota(jnp.int32, (tb, tb), 1)

    def block(j, state, masked):
        start = pl.multiple_of(j * tb, tb)
        hs = range(heads)
        ss = [_dot_nt(qs[h], k_ref[pl.ds(start, tb), lanes[h]]) for h in hs]
        ss = [ss[h] * scale + cqs[h] - ck_ref[h, j] for h in hs]
        if masked:
            ss = [jnp.where(col <= row, s, MASK_VALUE) for s in ss]
        m_new = [jnp.maximum(state[h][0], jnp.max(ss[h], axis=-1, keepdims=True)) for h in hs]
        ps = [jnp.exp(ss[h] - m_new[h]) for h in hs]
        corr = [jnp.exp(state[h][0] - m_new[h]) for h in hs]
        ls = [corr[h] * state[h][1] + jnp.sum(ps[h], axis=-1, keepdims=True) for h in hs]
        pv = [_dot(ps[h].astype(BF16), v_ref[pl.ds(start, tb), lanes[h]]) for h in hs]
        return tuple((m_new[h], ls[h], corr[h] * state[h][2] + pv[h]) for h in hs)

    def worth(j, state):
        jc = jnp.maximum(j, 0)
        gap = bounds[0] - ck_ref[0, jc][:, tb - 1:tb] - state[0][0]
        for h in range(1, heads):
            gap = jnp.maximum(gap, bounds[h] - ck_ref[h, jc][:, tb - 1:tb] - state[h][0])
        return (jnp.max(gap) > -SKIP_LOG).astype(jnp.int32)

    init = tuple((jnp.full((tb, 1), MASK_VALUE, F32), jnp.zeros((tb, 1), F32), jnp.zeros((tb, HEAD_DIM), F32))
                 for _ in range(heads))
    first = block(qi, init, True)

    def cond(c):
        return (c[0] >= 0) & (c[1] > 0)

    def body(c):
        j, _, state = c
        state = block(j, state, False)
        return j - 1, worth(j - 1, state), state

    _, _, final = lax.while_loop(cond, body, (qi - 1, worth(qi - 1, first), first))
    for h in range(heads):
        _, l, acc = final[h]
        o_ref[:, lanes[h]] = (acc / l).astype(o_ref.dtype)


def _fox(u, c, ck):
    s = u.shape[0]
    tb = min(FOX_BLOCK, s)
    nb = s // tb
    col0 = COL_FOX // FOX_W
    resident = lambda blk: pl.BlockSpec((s, FOX_W), lambda i: (0, blk), pipeline_mode=pl.Buffered(1))
    return pl.pallas_call(
        functools.partial(_fox_kernel, scale=HEAD_DIM ** -0.5),
        out_shape=jax.ShapeDtypeStruct((s, FOX_W), BF16),
        grid=(nb,),
        in_specs=[
            pl.BlockSpec((tb, FOX_W), lambda i: (i, col0)),
            resident(col0 + 1),
            resident(col0 + 2),
            pl.BlockSpec((tb, LANES), lambda i: (i, 0)),
            pl.BlockSpec((FOX_HEADS, nb, 1, tb), lambda i: (0, 0, 0, 0)),
        ],
        out_specs=pl.BlockSpec((tb, FOX_W), lambda i: (i, 0)),
        scratch_shapes=[pltpu.VMEM((FOX_HEADS, 1, 1), F32)],
        compiler_params=_params("arbitrary"),
        name="fox_attn",
    )(u, u, u, c, ck)


def _sb_kernel(q_ref, k_ref, v_ref, o_ref, *, scale, tb):
    n_sub = q_ref.shape[0] // tb
    heads = q_ref.shape[1] // HEAD_DIM
    first_block = pl.program_id(0) * n_sub
    lanes = [slice(h * HEAD_DIM, (h + 1) * HEAD_DIM) for h in range(heads)]
    units = [(u, h) for u in range(n_sub) for h in range(heads)]
    qs = {(u, h): q_ref[u * tb:(u + 1) * tb, lanes[h]] for u, h in units}
    row = lax.broadcasted_iota(jnp.int32, (tb, tb), 0)
    col = lax.broadcasted_iota(jnp.int32, (tb, tb), 1)
    upper = jnp.where(row > col, 1.0, 0.0).astype(BF16)

    def block(j0, state, masked):
        js = [j0 + u for u in range(n_sub)]
        starts = [pl.multiple_of(jnp.maximum(j, 0) * tb, tb) for j in js]
        keys = {(u, h): k_ref[pl.ds(starts[u], tb), lanes[h]] for u, h in units}
        zs = {key: _dot_nt(qs[key], keys[key]) * scale for key in units}
        sps = {key: _softplus(zs[key]) for key in units}
        if masked:
            valid = col < row
            log_1mb = {key: jnp.where(valid, -sps[key], 0.0) for key in units}
            carry = {key: state[n][0] for n, key in enumerate(units)}
        else:
            log_1mb = {key: -sps[key] for key in units}
            carry = {key: jnp.where(js[key[0]] >= 0, state[n][0], MASK_VALUE) for n, key in enumerate(units)}
        parts = {key: _split2(log_1mb[key]) for key in units}
        rest = {key: _dot(parts[key][0], upper) + _dot(parts[key][1], upper) for key in units}
        a = {key: jnp.exp(zs[key] - sps[key] + rest[key] + carry[key]) for key in units}
        if masked:
            a = {key: jnp.where(valid, a[key], 0.0) for key in units}
        pv = {(u, h): _dot(a[u, h].astype(BF16), v_ref[pl.ds(starts[u], tb), lanes[h]]) for u, h in units}
        return tuple((carry[key] + rest[key][:, :1] + log_1mb[key][:, :1], state[n][1] + pv[key])
                     for n, key in enumerate(units))

    def worth(state):
        top = state[0][0]
        for n in range(1, len(units)):
            top = jnp.maximum(top, state[n][0])
        return (jnp.max(top) > -SKIP_LOG).astype(jnp.int32)

    init = tuple((jnp.zeros((tb, 1), F32), jnp.zeros((tb, HEAD_DIM), F32)) for _ in units)
    first = block(first_block, init, True)

    def cond(c):
        return (c[0] + n_sub - 1 >= 0) & (c[1] > 0)

    def body(c):
        j0, _, state = c
        state = block(j0, state, False)
        return j0 - 1, worth(state), state

    _, _, final = lax.while_loop(cond, body, (first_block - 1, worth(first), first))
    for n, (u, h) in enumerate(units):
        o_ref[u * tb:(u + 1) * tb, lanes[h]] = final[n][1].astype(o_ref.dtype)


def _stick_breaking(u):
    s = u.shape[0]
    tb = min(SB_BLOCK, s)
    rows = min(SB_BLOCK * SB_SUBS, s)
    col0 = COL_SB // SB_W
    resident = lambda blk: pl.BlockSpec((s, SB_W), lambda i: (0, blk), pipeline_mode=pl.Buffered(1))
    return pl.pallas_call(
        functools.partial(_sb_kernel, scale=HEAD_DIM ** -0.5, tb=tb),
        out_shape=jax.ShapeDtypeStruct((s, SB_W), BF16),
        grid=(s // rows,),
        in_specs=[pl.BlockSpec((rows, SB_W), lambda i: (i, col0)), resident(col0 + 1), resident(col0 + 2)],
        out_specs=pl.BlockSpec((rows, SB_W), lambda i: (i, 0)),
        compiler_params=_params("parallel"),
        name="sb_attn",
    )(u, u, u)


def _merge_kernel(x_ref, og_ref, of_ref, os_ref, wg0_ref, wg1_ref, wg2_ref, wbg_ref, wbf_ref, wbs_ref,
                  wo_ref, g_ref, b_ref, o_ref, xb_ref, *, alpha):
    j = pl.program_id(1)

    @pl.when(j == 0)
    def _():
        x = x_ref[...]
        xb_ref[...] = x.astype(BF16)
        o_ref[...] = alpha * x

    xb = xb_ref[...]
    m = _sigmoid(_dot(xb, wg0_ref[...])) * _dot(og_ref[...], wbg_ref[...])
    m += _sigmoid(_dot(xb, wg1_ref[...])) * _dot(of_ref[...], wbf_ref[...])
    m += _sigmoid(_dot(xb, wg2_ref[...])) * _dot(os_ref[...], wbs_ref[...])
    o_ref[...] += _dot(m.astype(BF16), wo_ref[...])

    @pl.when(j == pl.num_programs(1) - 1)
    def _():
        o_ref[...] = _layernorm(o_ref[...], g_ref[...], b_ref[...])


def _merge_ln(x, o_gla, o_fox, o_sb, w_all, w_bg, w_bf, w_bs, w_out, g, b, layer, alpha):
    s, d = x.shape
    tm, tn = min(MERGE_TM, s), min(MERGE_TN, d)
    nj = d // tn
    gate0 = MAIN_W // tn
    row_spec = lambda w: pl.BlockSpec((tm, w), lambda i, j: (i, 0))
    gate_spec = lambda br: pl.BlockSpec((None, d, tn), lambda i, j: (layer, 0, gate0 + br * nj + j))
    col_spec = lambda w: pl.BlockSpec((None, w, tn), lambda i, j: (layer, 0, j))
    vec_spec = pl.BlockSpec((None, 1, d), lambda i, j: (layer, 0, 0))
    return pl.pallas_call(
        functools.partial(_merge_kernel, alpha=alpha),
        out_shape=jax.ShapeDtypeStruct((s, d), F32),
        grid=(s // tm, nj),
        in_specs=[
            row_spec(d), row_spec(GLA_V), row_spec(FOX_W), row_spec(SB_W),
            gate_spec(0), gate_spec(1), gate_spec(2),
            col_spec(GLA_V), col_spec(FOX_W), col_spec(SB_W),
            pl.BlockSpec((None, tn, d), lambda i, j: (layer, j, 0)),
            vec_spec, vec_spec,
        ],
        out_specs=pl.BlockSpec((tm, d), lambda i, j: (i, 0)),
        scratch_shapes=[pltpu.VMEM((tm, d), BF16)],
        compiler_params=_params("parallel", "arbitrary"),
        name="merge_ln",
    )(x, o_gla, o_fox, o_sb, w_all, w_all, w_all, w_bg, w_bf, w_bs, w_out, g[:, None, :], b[:, None, :])


def _mixer_branches(x, w_all, w_tail, layer, gla_w_a2, gla_b_a, gla_norm_g, fox_b_f):
    s, d = x.shape
    u, tail = _proj(x, w_all, w_tail, layer)

    wa_pad = jnp.zeros((LANES, GLA_QK), F32).at[:GLA_RANK].set(gla_w_a2)
    o_gla = _gla(u, tail, wa_pad, gla_b_a, gla_norm_g)

    bf_pad = jnp.zeros((1, LANES), F32).at[0, FF_LANE:FF_LANE + FOX_HEADS].set(fox_b_f)
    c = _fox_cumsum(tail, bf_pad)
    ct = c[:, FF_LANE:FF_LANE + FOX_HEADS].T
    tb = min(FOX_BLOCK, s)
    o_fox = _fox(u, c, ct.reshape(FOX_HEADS, s // tb, 1, tb))
    o_sb = _stick_breaking(u)
    return o_gla, o_fox, o_sb


def _mixer_ln(x, w_all, w_tail, layer, gla_w_a2, gla_b_a, gla_norm_g, fox_b_f, w_bg, w_bf, w_bs, w_out,
              ln_g, ln_b, alpha):
    o_gla, o_fox, o_sb = _mixer_branches(x, w_all, w_tail, layer, gla_w_a2, gla_b_a, gla_norm_g, fox_b_f)
    return _merge_ln(x, o_gla, o_fox, o_sb, w_all, w_bg, w_bf, w_bs, w_out, ln_g, ln_b, layer, alpha)


def kernel(x, ffn1_w_gate, ffn1_w_up, ffn1_w_down, ln1_g, ln1_b, w_in, gla_w_a2, gla_b_a, gla_norm_g, fox_b_f, w_br_gla, w_br_fox, w_br_sb, w_out, ln2_g, ln2_b, ffn2_w_gate, ffn2_w_up, ffn2_w_down, ln3_g, ln3_b):
    batch, seq, d = x.shape
    depth = w_in.shape[0]
    alpha = (2 * depth) ** 0.25
    cast = lambda w: w.astype(BF16)
    ffn1 = (ffn1_w_gate, ffn1_w_up, ffn1_w_down)
    ffn2 = (ffn2_w_gate, ffn2_w_up, ffn2_w_down)
    branch = (cast(w_br_gla), cast(w_br_fox), cast(w_br_sb), cast(w_out))
    assert (N_BRANCH * d) % REGROUP_W == 0 and w_in.shape[2] == MAIN_W + GLA_RANK + FOX_HEADS + N_BRANCH * d
    w_all, w_tail = _regroup(w_in), _tail_weights(w_in)
    outs = []
    for bi in range(batch):
        h = x[bi]
        for l in range(depth):
            h = _ffn_ln(h, *ffn1, ln1_g, ln1_b, l, alpha)
            h = _mixer_ln(h, w_all, w_tail, l, gla_w_a2[l], gla_b_a[l], gla_norm_g[l], fox_b_f[l], *branch,
                          ln2_g, ln2_b, alpha)
            h = _ffn_ln(h, *ffn2, ln3_g, ln3_b, l, alpha)
        outs.append(h)
    return jnp.stack(outs, axis=0)
```
